```python
import math
import jax
import jax.numpy as jnp
from jax import lax
import numpy as np

D_MODEL = 1024
BATCH = 16
SEQ = 2048
DEPTH = 1
DEC_BATCH = 32
DEC_SEQ = 4
PAST_LEN = 16384
PAGE_SIZE = 128

GDN_HEADS = 4
GDN_DK = 128
GDN_DV = 128
GDN_CONV = 4
GDN_CHUNK = 64
GDN_QK = GDN_HEADS * GDN_DK
GDN_V = GDN_HEADS * GDN_DV
GDN_CONV_CH = 2 * GDN_QK + GDN_V
MOBA_HEADS = 8
MOBA_HD = 64
MOBA_BLOCK = 256
MOBA_TOPK = 3
Q_BLOCK = 32
MOBA_W = MOBA_HEADS * MOBA_HD
MIX_W = GDN_V + MOBA_W
IN_W = GDN_CONV_CH + 2 * GDN_HEADS + GDN_V + 3 * MOBA_W
D_FF = ((8 * D_MODEL // 3) + 127) // 128 * 128
N_MOD = 9
EPS = 1e-6

kernel_name = "hymba_gdn_moba_macaron_adaln_step"


def rmsnorm(x, w):
    xf = x.astype(jnp.float32)
    y = xf * lax.rsqrt(jnp.mean(xf * xf, axis=-1, keepdims=True) + EPS)
    return (y * w.astype(jnp.float32)).astype(x.dtype)


def l2norm(x):
    xf = x.astype(jnp.float32)
    return xf * lax.rsqrt(jnp.sum(xf * xf, axis=-1, keepdims=True) + EPS)


def swiglu(x, w_up, w_down):
    gate, up = jnp.split(x @ w_up, 2, axis=-1)
    return (jax.nn.silu(gate) * up) @ w_down


def alibi_slopes():
    return jnp.exp2(-8.0 * jnp.arange(1, MOBA_HEADS + 1, dtype=jnp.float32) / MOBA_HEADS)


def causal_dwconv(xc, w):
    return lax.conv_general_dilated(
        xc, w[:, None, :].astype(xc.dtype), window_strides=(1,), padding="VALID",
        dimension_numbers=("NWC", "WIO", "NWC"), feature_group_count=xc.shape[-1])


def gated_delta_rule(q, k, v, g, beta, s0):
    b, t_len, h, _ = q.shape
    dv = v.shape[-1]
    c = GDN_CHUNK
    pad = (-t_len) % c

    def prep(a):
        a = a.astype(jnp.float32)
        a = jnp.pad(a, [(0, 0), (0, pad)] + [(0, 0)] * (a.ndim - 2))
        a = a.reshape((b, a.shape[1] // c, c) + a.shape[2:])
        return jnp.moveaxis(a, (1, 3), (0, 2))

    qc, kc, vc, gc, bc = prep(q), prep(k), prep(v), prep(g), prep(beta)
    gcum = jnp.cumsum(gc, axis=-1)
    pos = jnp.arange(c)
    causal = pos[:, None] >= pos[None, :]
    strict = pos[:, None] > pos[None, :]
    decay = jnp.exp(jnp.where(causal, gcum[..., :, None] - gcum[..., None, :], -jnp.inf))
    kb = kc * bc[..., None]
    m = jnp.where(strict, jnp.einsum("nbhtd,nbhsd->nbhts", kb, kc) * decay, 0.0)
    a_mat = m + jnp.eye(c, dtype=jnp.float32)
    rhs = jnp.concatenate([vc * bc[..., None], kb * jnp.exp(gcum)[..., None]], axis=-1)
    sol = lax.linalg.triangular_solve(a_mat, rhs, left_side=True, lower=True, unit_diagonal=True)
    u, w = sol[..., :dv], sol[..., dv:]
    attn = jnp.einsum("nbhtd,nbhsd->nbhts", qc, kc) * decay
    q_dec = qc * jnp.exp(gcum)[..., None]
    g_last = gcum[..., -1]
    k_dec = kc * jnp.exp(g_last[..., None] - gcum)[..., None]

    def step(s, inp):
        u_i, w_i, attn_i, qd_i, kd_i, gl_i = inp
        v_new = u_i - jnp.einsum("bhcd,bhde->bhce", w_i, s)
        o = jnp.einsum("bhcd,bhde->bhce", qd_i, s) + jnp.einsum("bhts,bhse->bhte", attn_i, v_new)
        s = s * jnp.exp(gl_i)[..., None, None] + jnp.einsum("bhcd,bhce->bhde", kd_i, v_new)
        return s, o

    s_fin, o = lax.scan(step, s0.astype(jnp.float32), (u, w, attn, q_dec, k_dec, g_last))
    o = jnp.moveaxis(o, (0, 2), (1, 3)).reshape(b, -1, h, dv)[:, :t_len]
    return o, s_fin


def own_block_logits(q, k_own, t, s, slopes):
    dist = (t[:, None] - s[None, :]).astype(jnp.float32)
    lg = jnp.einsum("bqhd,bkhd->bqhk", q, k_own.astype(q.dtype), preferred_element_type=jnp.float32)
    lg = lg - slopes[None, None, :, None] * dist[None, :, None, :]
    return jnp.where(dist[None, :, None, :] >= 0, lg, -jnp.inf)


def selected_logits(q, k_sel, t, idx, slopes):
    s = idx[..., None] * MOBA_BLOCK + jnp.arange(MOBA_BLOCK)
    dist = (t[None, :, None, None, None] - s).astype(jnp.float32)
    lg = jnp.einsum("bqhd,bqhnkd->bqhnk", q, k_sel.astype(q.dtype), preferred_element_type=jnp.float32)
    return lg - slopes[None, None, :, None, None] * dist


def attend_merged(l_sel, v_sel, l_own, v_own):
    if l_sel is None:
        p = jax.nn.softmax(l_own, axis=-1).astype(v_own.dtype)
        return jnp.einsum("bqhk,bkhd->bqhd", p, v_own)
    b, q, h, n, k1 = l_sel.shape
    p = jax.nn.softmax(jnp.concatenate([l_sel.reshape(b, q, h, n * k1), l_own], axis=-1), axis=-1)
    p_sel = p[..., : n * k1].reshape(b, q, h, n, k1).astype(v_sel.dtype)
    p_own = p[..., n * k1:].astype(v_own.dtype)
    return (jnp.einsum("bqhnk,bqhnkd->bqhd", p_sel, v_sel)
            + jnp.einsum("bqhk,bkhd->bqhd", p_own, v_own.astype(v_sel.dtype)))


def moba_prompt(q, k, v):
    b, t_len, h, hd = q.shape
    nb = -(-t_len // MOBA_BLOCK)
    pad = nb * MOBA_BLOCK - t_len
    kp = jnp.pad(k, ((0, 0), (0, pad), (0, 0), (0, 0)))
    vp = jnp.pad(v, ((0, 0), (0, pad), (0, 0), (0, 0)))
    kblk = kp.reshape(b, nb, MOBA_BLOCK, h, hd)
    vblk = vp.reshape(b, nb, MOBA_BLOCK, h, hd)
    kmean = jnp.mean(kblk.astype(jnp.float32), axis=2)
    n_sel = min(MOBA_TOPK, (t_len - 1) // MOBA_BLOCK)
    slopes = alibi_slopes()
    bi = jnp.arange(b)[:, None, None, None]
    hi = jnp.arange(h)[None, None, :, None]
    nq = t_len // Q_BLOCK
    qb = q.reshape(b, nq, Q_BLOCK, h, hd).transpose(1, 0, 2, 3, 4)

    def one_block(args):
        i, qi = args
        q0 = i * Q_BLOCK
        j = q0 // MOBA_BLOCK
        t = q0 + jnp.arange(Q_BLOCK)
        k_own = lax.dynamic_slice_in_dim(kp, j * MOBA_BLOCK, MOBA_BLOCK, axis=1)
        v_own = lax.dynamic_slice_in_dim(vp, j * MOBA_BLOCK, MOBA_BLOCK, axis=1)
        l_own = own_block_logits(qi, k_own, t, j * MOBA_BLOCK + jnp.arange(MOBA_BLOCK), slopes)
        if n_sel == 0:
            return attend_merged(None, None, l_own, v_own)
        gate = jnp.einsum("bqhd,bnhd->bqhn", qi.astype(jnp.float32), kmean)
        gate = jnp.where(jnp.arange(nb) < j, gate, -jnp.inf)
        _, idx = lax.top_k(gate, n_sel)
        k_sel = kblk[bi, idx, :, hi, :]
        v_sel = vblk[bi, idx, :, hi, :]
        l_sel = selected_logits(qi, k_sel, t, idx, slopes)
        l_sel = jnp.where((idx < j)[..., None], l_sel, -jnp.inf)
        return attend_merged(l_sel, v_sel, l_own, v_own)

    out = lax.map(one_block, (jnp.arange(nq), qb))
    return out.transpose(1, 0, 2, 3, 4).reshape(b, t_len, h, hd)


def moba_sample(q, k, v, cache_k, cache_v, page_table):
    b, ds, h, hd = q.shape
    past = page_table.shape[1] * PAGE_SIZE
    ppb = MOBA_BLOCK // PAGE_SIZE
    j = past // MOBA_BLOCK
    own_start = j * MOBA_BLOCK
    slopes = alibi_slopes()
    t = past + jnp.arange(ds)
    k_own, v_own = k, v
    if past > own_start:
        own_pages = page_table[:, own_start // PAGE_SIZE:]
        k_own = jnp.concatenate([cache_k[own_pages].reshape(b, -1, h, hd).astype(k.dtype), k], axis=1)
        v_own = jnp.concatenate([cache_v[own_pages].reshape(b, -1, h, hd).astype(v.dtype), v], axis=1)
    l_own = own_block_logits(q, k_own, t, own_start + jnp.arange(k_own.shape[1]), slopes)
    n_sel = min(MOBA_TOPK, j)
    if n_sel == 0:
        return attend_merged(None, None, l_own, v_own)
    k_full = cache_k[page_table[:, : j * ppb]].reshape(b, j, MOBA_BLOCK, h, hd)
    kmean = jnp.mean(k_full.astype(jnp.float32), axis=2)
    gate = jnp.einsum("bqhd,bnhd->bqhn", q.astype(jnp.float32), kmean)
    _, idx = lax.top_k(gate, n_sel)
    lpage = idx[..., None] * ppb + jnp.arange(ppb)
    phys = page_table[jnp.arange(b)[:, None, None, None, None], lpage]
    hi = jnp.arange(h)[None, None, :, None, None]
    k_sel = cache_k[phys, :, hi, :].reshape(b, ds, h, n_sel, MOBA_BLOCK, hd)
    v_sel = cache_v[phys, :, hi, :].reshape(b, ds, h, n_sel, MOBA_BLOCK, hd).astype(v.dtype)
    l_sel = selected_logits(q, k_sel, t, idx, slopes)
    return attend_merged(l_sel, v_sel, l_own, v_own)


def token_mixing(hn, conv_state, s0, attend, w_in, conv_w, a_log, dt_bias, gdn_norm_w, w_out):
    b, t_len, _ = hn.shape
    proj = hn @ w_in
    o1 = GDN_CONV_CH
    o2 = o1 + GDN_HEADS
    o3 = o2 + GDN_HEADS
    o4 = o3 + GDN_V
    qkv_pre, a, bb, z = proj[..., :o1], proj[..., o1:o2], proj[..., o2:o3], proj[..., o3:o4]
    mq, mk, mv = jnp.split(proj[..., o4:], 3, axis=-1)
    xc = jnp.concatenate([conv_state.astype(qkv_pre.dtype), qkv_pre], axis=1)
    new_conv = xc[:, -(GDN_CONV - 1):]
    qkv = jax.nn.silu(causal_dwconv(xc, conv_w))
    gq, gk, gv = jnp.split(qkv, [GDN_QK, 2 * GDN_QK], axis=-1)
    gq = l2norm(gq.reshape(b, t_len, GDN_HEADS, GDN_DK)) * (GDN_DK ** -0.5)
    gk = l2norm(gk.reshape(b, t_len, GDN_HEADS, GDN_DK))
    gv = gv.reshape(b, t_len, GDN_HEADS, GDN_DV)
    g_log = -jnp.exp(a_log.astype(jnp.float32)) * jax.nn.softplus(a.astype(jnp.float32) + dt_bias.astype(jnp.float32))
    beta = jax.nn.sigmoid(bb.astype(jnp.float32))
    go, s_new = gated_delta_rule(gq, gk, gv, g_log, beta, s0)
    go = rmsnorm(go, gdn_norm_w) * jax.nn.silu(z.reshape(b, t_len, GDN_HEADS, GDN_DV).astype(jnp.float32))
    go = go.astype(hn.dtype).reshape(b, t_len, GDN_V)
    mq = mq.reshape(b, t_len, MOBA_HEADS, MOBA_HD) * (MOBA_HD ** -0.5)
    mk = mk.reshape(b, t_len, MOBA_HEADS, MOBA_HD)
    mv = mv.reshape(b, t_len, MOBA_HEADS, MOBA_HD)
    mo = attend(mq, mk, mv).astype(hn.dtype).reshape(b, t_len, MOBA_W)
    out = jnp.concatenate([go, mo], axis=-1) @ w_out
    return out, s_new, new_conv, mk, mv


def trunk_layer(x, c, mixing, w_ada, b_ada, norm_w, ffn_up, ffn_down):
    mod = (jax.nn.silu(c) @ w_ada + b_ada).reshape(c.shape[0], N_MOD, 1, D_MODEL)
    sh1, sc1, g1, sh2, sc2, g2, sh3, sc3, g3 = [mod[:, i] for i in range(N_MOD)]
    h = x + 0.5 * g1 * swiglu(rmsnorm(x, norm_w[0]) * (1 + sc1) + sh1, ffn_up[0], ffn_down[0])
    mix_out, *state = mixing(rmsnorm(h, norm_w[1]) * (1 + sc2) + sh2)
    h = h + g2 * mix_out
    h = h + 0.5 * g3 * swiglu(rmsnorm(h, norm_w[2]) * (1 + sc3) + sh3, ffn_up[1], ffn_down[1])
    return h, state


def setup_inputs(seed: int = 0) -> dict:
    key = jax.random.key(seed)
    ks = jax.random.split(key, 24)
    f32 = jnp.float32

    def nrm(k, shape, s=1.0):
        return jax.random.normal(k, shape, f32) * s

    n_pages = PAST_LEN // PAGE_SIZE
    n_phys = (5 * DEC_BATCH * n_pages + 3) // 4
    perm = jax.random.permutation(ks[6], n_phys)
    page_table = perm[: DEC_BATCH * n_pages].reshape(DEC_BATCH, n_pages).astype(jnp.int32)
    dt = jnp.exp(jax.random.uniform(ks[16], (DEPTH, GDN_HEADS), f32, math.log(1e-3), math.log(1e-1)))
    return {
        "x_prompt": nrm(ks[0], (BATCH, SEQ, D_MODEL)),
        "x_sample": nrm(ks[1], (DEC_BATCH, DEC_SEQ, D_MODEL)),
        "c_prompt": nrm(ks[2], (BATCH, D_MODEL)),
        "c_sample": nrm(ks[3], (DEC_BATCH, D_MODEL)),
        "cache_k": nrm(ks[4], (DEPTH, n_phys, PAGE_SIZE, MOBA_HEADS, MOBA_HD)),
        "cache_v": nrm(ks[5], (DEPTH, n_phys, PAGE_SIZE, MOBA_HEADS, MOBA_HD)),
        "page_table": page_table,
        "state_gdn": nrm(ks[7], (DEPTH, DEC_BATCH, GDN_HEADS, GDN_DK, GDN_DV), 0.5),
        "state_conv": nrm(ks[8], (DEPTH, DEC_BATCH, GDN_CONV - 1, GDN_CONV_CH)),
        "w_ada": nrm(ks[9], (DEPTH, D_MODEL, N_MOD * D_MODEL), 0.5 * D_MODEL ** -0.5),
        "b_ada": nrm(ks[10], (DEPTH, N_MOD * D_MODEL), 0.1),
        "norm_w": 1.0 + nrm(ks[11], (DEPTH, 3, D_MODEL), 0.05),
        "ffn_up": nrm(ks[12], (DEPTH, 2, D_MODEL, 2 * D_FF), D_MODEL ** -0.5),
        "ffn_down": nrm(ks[13], (DEPTH, 2, D_FF, D_MODEL), D_FF ** -0.5),
        "w_in": nrm(ks[14], (DEPTH, D_MODEL, IN_W), D_MODEL ** -0.5),
        "conv_w": nrm(ks[17], (DEPTH, GDN_CONV, GDN_CONV_CH), GDN_CONV ** -0.5),
        "a_log": jnp.log(jax.random.uniform(ks[15], (DEPTH, GDN_HEADS), f32, 1.0, 16.0)),
        "dt_bias": dt + jnp.log(-jnp.expm1(-dt)),
        "gdn_norm_w": 1.0 + nrm(ks[18], (DEPTH, GDN_DV), 0.05),
        "w_out": nrm(ks[19], (DEPTH, MIX_W, D_MODEL), MIX_W ** -0.5),
        "final_norm_w": 1.0 + nrm(ks[20], (D_MODEL,), 0.05),
    }


def reference(x_prompt, x_sample, c_prompt, c_sample, cache_k, cache_v, page_table, state_gdn,
              state_conv, w_ada, b_ada, norm_w, ffn_up, ffn_down, w_in, conv_w, a_log, dt_bias,
              gdn_norm_w, w_out, final_norm_w):
    hp, hs = x_prompt, x_sample
    kp_l, vp_l, sp_l, cp_l, ks_l, vs_l, ss_l, cs_l = [], [], [], [], [], [], [], []
    for l in range(DEPTH):
        mix_w = (w_in[l], conv_w[l], a_log[l], dt_bias[l], gdn_norm_w[l], w_out[l])
        layer_w = (w_ada[l], b_ada[l], norm_w[l], ffn_up[l], ffn_down[l])
        conv0 = jnp.zeros((hp.shape[0], GDN_CONV - 1, GDN_CONV_CH), hp.dtype)
        s0 = jnp.zeros((hp.shape[0], GDN_HEADS, GDN_DK, GDN_DV), jnp.float32)
        hp, (s_p, c_p, k_p, v_p) = trunk_layer(
            hp, c_prompt, lambda hn: token_mixing(hn, conv0, s0, moba_prompt, *mix_w), *layer_w)
        ck, cv = cache_k[l], cache_v[l]
        sample_attend = lambda q, k, v: moba_sample(q, k, v, ck, cv, page_table)
        hs, (s_s, c_s, k_s, v_s) = trunk_layer(
            hs, c_sample, lambda hn: token_mixing(hn, state_conv[l], state_gdn[l], sample_attend, *mix_w), *layer_w)
        kp_l.append(k_p); vp_l.append(v_p); sp_l.append(s_p); cp_l.append(c_p)
        ks_l.append(k_s); vs_l.append(v_s); ss_l.append(s_s); cs_l.append(c_s)
    y_prompt = rmsnorm(hp, final_norm_w)
    y_sample = rmsnorm(hs, final_norm_w)
    k_prompt = jnp.stack(kp_l)
    v_prompt = jnp.stack(vp_l)
    gdn_prompt = jnp.stack(sp_l)
    conv_prompt = jnp.stack(cp_l)
    k_sample = jnp.stack(ks_l)
    v_sample = jnp.stack(vs_l)
    gdn_sample = jnp.stack(ss_l)
    conv_sample = jnp.stack(cs_l)
    return (y_prompt, y_sample, k_prompt, v_prompt, gdn_prompt, conv_prompt, k_sample, v_sample, gdn_sample, conv_sample)
```

```python
import functools

import jax
import jax.numpy as jnp
from jax import lax
from jax.experimental import pallas as pl
from jax.experimental.pallas import tpu as pltpu

F32 = jnp.float32
BF16 = jnp.bfloat16
EPS = 1e-6
NEG = -1e30

GDN_HEADS = 4
GDN_DK = 128
GDN_CONV = 4
GDN_CHUNK = 64
MOBA_HEADS = 8
MOBA_HD = 64
MOBA_BLOCK = 256
MOBA_TOPK = 3
PAGE_SIZE = 128
N_MOD = 9
LANES = 128
VMEM_LIMIT = 56 * 1024 * 1024


def _sigmoid(x):
    return 1.0 / (1.0 + jnp.exp(-x))


def _silu(x):
    return x * _sigmoid(x)


def _bdot(a, b):
    return jnp.dot(a.astype(BF16), b.astype(BF16), preferred_element_type=F32)


def _bdot_nt(a, b):
    return lax.dot_general(a.astype(BF16), b.astype(BF16), (((1,), (1,)), ((), ())),
                           preferred_element_type=F32)


def _bdot_tn(a, b):
    return lax.dot_general(a.astype(BF16), b.astype(BF16), (((0,), (0,)), ((), ())),
                           preferred_element_type=F32)


def _split(x):
    hi = x.astype(BF16)
    lo = (x - hi.astype(F32)).astype(BF16)
    return hi, lo


def _dot3(a, b):
    a_hi, a_lo = _split(a)
    b_hi, b_lo = _split(b)
    d = functools.partial(jnp.dot, preferred_element_type=F32)
    return d(a_hi, b_hi) + (d(a_lo, b_hi) + d(a_hi, b_lo))


def _norm_mod(x, nw, sc, sh):
    y = x * lax.rsqrt(jnp.mean(x * x, axis=-1, keepdims=True) + EPS)
    return (y * nw) * (1.0 + sc) + sh


def _mod_spec(rows, tm, d):
    if rows == 1:
        return pl.BlockSpec((1, 1, d), lambda g, t: (g, 0, 0))
    return pl.BlockSpec((1, tm, d), lambda g, t: (g, t, 0))


def _resident(shape):
    nd = len(shape)
    return pl.BlockSpec(shape, lambda *_: (0,) * nd, pipeline_mode=pl.Buffered(1))


def _ada_kernel(c_ref, w_ref, b_ref, o_ref):
    o_ref[...] = _dot3(_silu(c_ref[...]), w_ref[...]) + b_ref[...]


def _ada_mod(c_all, w, b):
    n, d = c_all.shape
    dout = w.shape[1]
    tn = d
    return pl.pallas_call(
        _ada_kernel,
        grid=(dout // tn,),
        in_specs=[pl.BlockSpec((n, d), lambda i: (0, 0)),
                  pl.BlockSpec((d, tn), lambda i: (0, i)),
                  pl.BlockSpec((1, tn), lambda i: (0, i))],
        out_specs=pl.BlockSpec((n, tn), lambda i: (0, i)),
        out_shape=jax.ShapeDtypeStruct((n, dout), F32),
        compiler_params=pltpu.CompilerParams(dimension_semantics=("arbitrary",),
                                             vmem_limit_bytes=VMEM_LIMIT),
        name="ada_mod",
    )(c_all, w, b.reshape(1, dout))


def _ffn_kernel(*refs, tf, has_mix, has_final):
    it = iter(refs)
    x_ref, sh_ref, sc_ref, g_ref, nw_ref, wup_ref, wdn_ref = [next(it) for _ in range(7)]
    if has_mix:
        go_ref, mo_ref, g2_ref, wo_ref = [next(it) for _ in range(4)]
    if has_final:
        fw_ref = next(it)
    o_ref = next(it)
    a_scr = next(it)

    x = x_ref[0]
    if has_mix:
        half = go_ref.shape[-1]
        mix = _bdot(go_ref[0], wo_ref[0:half, :]) + _bdot(mo_ref[0], wo_ref[half:, :])
        x = x + g2_ref[0] * mix
    hb = _norm_mod(x, nw_ref[...], sc_ref[0], sh_ref[0]).astype(BF16)
    d_ff = wdn_ref.shape[0]
    for c in range(d_ff // tf):
        gate = jnp.dot(hb, wup_ref[:, c * tf:(c + 1) * tf], preferred_element_type=F32)
        up = jnp.dot(hb, wup_ref[:, d_ff + c * tf:d_ff + (c + 1) * tf], preferred_element_type=F32)
        a_scr[:, c * tf:(c + 1) * tf] = (_silu(gate) * up).astype(BF16)
    down = jnp.dot(a_scr[...], wdn_ref[...], preferred_element_type=F32)
    out = x + (0.5 * g_ref[0]) * down
    if has_final:
        out = out * lax.rsqrt(jnp.mean(out * out, axis=-1, keepdims=True) + EPS) * fw_ref[...]
    o_ref[0] = out


def _ffn(x, sh, sc, g, nw, wup, wdn, *, tm, mix=None, final_w=None):
    G, T, D = x.shape
    d_ff = wdn.shape[0]
    tf = d_ff // 2 if (d_ff // 2) % LANES == 0 else d_ff
    row = pl.BlockSpec((1, tm, D), lambda g_, t: (g_, t, 0))
    ins = [x, sh, sc, g, nw.reshape(1, D), wup, wdn]
    specs = [row, _mod_spec(sh.shape[1], tm, D), _mod_spec(sc.shape[1], tm, D),
             _mod_spec(g.shape[1], tm, D), _resident((1, D)), _resident(wup.shape), _resident(wdn.shape)]
    if mix is not None:
        go, mo, g2, wo = mix
        half = go.shape[-1]
        ins += [go, mo, g2, wo]
        specs += [pl.BlockSpec((1, tm, half), lambda g_, t: (g_, t, 0)),
                  pl.BlockSpec((1, tm, half), lambda g_, t: (g_, t, 0)),
                  _mod_spec(g2.shape[1], tm, D), _resident(wo.shape)]
    if final_w is not None:
        ins.append(final_w.reshape(1, D))
        specs.append(_resident((1, D)))
    return pl.pallas_call(
        functools.partial(_ffn_kernel, tf=tf, has_mix=mix is not None, has_final=final_w is not None),
        grid=(G, T // tm),
        in_specs=specs,
        out_specs=row,
        out_shape=jax.ShapeDtypeStruct((G, T, D), F32),
        scratch_shapes=[pltpu.VMEM((tm, d_ff), BF16)],
        compiler_params=pltpu.CompilerParams(dimension_semantics=("parallel", "parallel"),
                                             vmem_limit_bytes=VMEM_LIMIT),
        name="ffn_mix" if mix is not None else "ffn",
    )(*ins)


def _win_kernel(h_ref, sh_ref, sc_ref, nw_ref, wa_ref, wkv_ref, wabh_ref, wabl_ref,
                qkv_ref, z_ref, mq_ref, kt_ref, vt_ref, ab_ref, *, n_qkv, n_z, n_m):
    hn = _norm_mod(h_ref[0], nw_ref[...], sc_ref[0], sh_ref[0])
    hb = hn.astype(BF16)
    h_lo = (hn - hb.astype(F32)).astype(BF16)
    d = functools.partial(jnp.dot, preferred_element_type=F32)
    qkv_ref[0] = d(hb, wa_ref[:, 0:n_qkv])
    z_ref[0] = d(hb, wa_ref[:, n_qkv:n_qkv + n_z])
    mq_ref[0] = d(hb, wa_ref[:, n_qkv + n_z:n_qkv + n_z + n_m]) * (MOBA_HD ** -0.5)
    nt = functools.partial(lax.dot_general, dimension_numbers=(((1,), (1,)), ((), ())),
                           preferred_element_type=F32)
    kt_ref[0] = nt(wkv_ref[0:n_m, :], hb)
    vt_ref[0] = nt(wkv_ref[n_m:, :], hb)
    ab_ref[0] = d(hb, wabh_ref[...]) + (d(h_lo, wabh_ref[...]) + d(hb, wabl_ref[...]))


def _win(h, sh, sc, nw, wa, wkv_t, wab_hi, wab_lo, *, tm, n_qkv, n_z, n_m):
    G, T, D = h.shape
    row = lambda n: pl.BlockSpec((1, tm, n), lambda g_, t: (g_, t, 0))
    col = pl.BlockSpec((1, n_m, tm), lambda g_, t: (g_, 0, t))
    return pl.pallas_call(
        functools.partial(_win_kernel, n_qkv=n_qkv, n_z=n_z, n_m=n_m),
        grid=(G, T // tm),
        in_specs=[row(D), _mod_spec(sh.shape[1], tm, D), _mod_spec(sc.shape[1], tm, D),
                  _resident((1, D)), _resident(wa.shape), _resident(wkv_t.shape),
                  _resident(wab_hi.shape), _resident(wab_lo.shape)],
        out_specs=[row(n_qkv), row(n_z), row(n_m), col, col, row(LANES)],
        out_shape=[jax.ShapeDtypeStruct((G, T, n_qkv), F32), jax.ShapeDtypeStruct((G, T, n_z), F32),
                   jax.ShapeDtypeStruct((G, T, n_m), F32), jax.ShapeDtypeStruct((G, n_m, T), F32),
                   jax.ShapeDtypeStruct((G, n_m, T), F32), jax.ShapeDtypeStruct((G, T, LANES), F32)],
        compiler_params=pltpu.CompilerParams(dimension_semantics=("parallel", "parallel"),
                                             vmem_limit_bytes=VMEM_LIMIT),
        name="w_in",
    )(h, sh, sc, nw.reshape(1, D), wa, wkv_t, wab_hi, wab_lo)


TRI_BASE = 8


def _unit_lower_inverse(m, eye, ri, ci, c):
    size = TRI_BASE
    n = jnp.where(ri // size == ci // size, m, 0.0)
    t = eye - n
    p = n
    e = 2
    while e < size:
        p = _bdot(p, p)
        t = t + _bdot(t, p)
        e *= 2
    while size < c:
        f = jnp.where((ri // size != ci // size) & (ri // (2 * size) == ci // (2 * size)), m, 0.0)
        t = t - _bdot(t, _bdot(f, t))
        size *= 2
    return t


def _gdn_kernel(qp_ref, kp_ref, vp_ref, cq_ref, ck_ref, cv_ref, wq_ref, wk_ref, wv_ref, ab_ref, z_ref,
                s0_ref, gp_ref, nw_ref, go_ref, so_ref, ncq_ref, nck_ref, ncv_ref,
                xq, xk, xv, s_scr, *, ct, c, t_valid, n_t):
    h = pl.program_id(1)
    t = pl.program_id(2)
    halo = GDN_CONV - 1
    base = 8

    @pl.when(t == 0)
    def _():
        xq[base - halo:base, :] = cq_ref[0]
        xk[base - halo:base, :] = ck_ref[0]
        xv[base - halo:base, :] = cv_ref[0]
        s_scr[...] = s0_ref[0, 0]

    @pl.when(t > 0)
    def _():
        xq[base - halo:base, :] = xq[ct + base - halo:ct + base, :]
        xk[base - halo:base, :] = xk[ct + base - halo:ct + base, :]
        xv[base - halo:base, :] = xv[ct + base - halo:ct + base, :]

    xq[base:base + ct, :] = qp_ref[0]
    xk[base:base + ct, :] = kp_ref[0]
    xv[base:base + ct, :] = vp_ref[0]

    def conv(xb, w_ref):
        acc = w_ref[0:1, :] * xb[base - halo:base - halo + ct, :]
        for i in range(1, GDN_CONV):
            acc = acc + w_ref[i:i + 1, :] * xb[base - halo + i:base - halo + i + ct, :]
        return _silu(acc)

    q = conv(xq, wq_ref)
    k = conv(xk, wk_ref)
    v = conv(xv, wv_ref)
    q = q * lax.rsqrt(jnp.sum(q * q, axis=-1, keepdims=True) + EPS) * (GDN_DK ** -0.5)
    k = k * lax.rsqrt(jnp.sum(k * k, axis=-1, keepdims=True) + EPS)

    ab = ab_ref[0]
    lane = lax.broadcasted_iota(jnp.int32, (ct, LANES), 1)
    rowi = lax.broadcasted_iota(jnp.int32, (ct, LANES), 0)
    xg = ab + gp_ref[1:2, :]
    softplus = jnp.maximum(xg, 0.0) + jnp.log(1.0 + jnp.exp(-jnp.abs(xg)))
    g_all = -jnp.exp(gp_ref[0:1, :]) * softplus
    b_all = _sigmoid(ab)
    if t_valid % ct != 0:
        valid = (rowi + t * ct) < t_valid
        g_all = jnp.where(valid, g_all, 0.0)
        b_all = jnp.where(valid, b_all, 0.0)
    rin = rowi % c
    gc_all = g_all
    s = 1
    while s < c:
        gc_all = gc_all + jnp.where(rin >= s, pltpu.roll(gc_all, s, axis=0), 0.0)
        s *= 2
    gcum = jnp.sum(jnp.where(lane == h, gc_all, 0.0), axis=1, keepdims=True)
    beta = jnp.sum(jnp.where(lane == h + GDN_HEADS, b_all, 0.0), axis=1, keepdims=True)
    grow = jnp.transpose(jnp.broadcast_to(gcum, (ct, LANES)))[0:1, :]

    ri = lax.broadcasted_iota(jnp.int32, (c, c), 0)
    ci = lax.broadcasted_iota(jnp.int32, (c, c), 1)
    eye = jnp.where(ri == ci, 1.0, 0.0)
    z = z_ref[0]
    nw = nw_ref[...]
    S = s_scr[...]
    for cc in range(ct // c):
        r0 = cc * c
        if n_t == 1 and r0 >= t_valid:
            go_ref[0, r0:r0 + c, :] = jnp.zeros((c, LANES), F32)
            continue
        qc, kc, vc = q[r0:r0 + c], k[r0:r0 + c], v[r0:r0 + c]
        gc, bc = gcum[r0:r0 + c], beta[r0:r0 + c]
        decay = jnp.exp(jnp.where(ri >= ci, gc - grow[:, r0:r0 + c], -jnp.inf))
        kb = kc * bc
        m = jnp.where(ri > ci, _bdot_nt(kb, kc) * decay, 0.0)
        tm1 = _unit_lower_inverse(m, eye, ri, ci, c) - eye
        egc = jnp.exp(gc)
        ru = vc * bc
        rw = kb * egc
        u = ru + _bdot(tm1, ru)
        w = rw + _bdot(tm1, rw)
        attn = _bdot_nt(qc, kc) * decay
        gl = gc[c - 1:c, :]
        kd = kc * jnp.exp(gl - gc)
        v_new = u - _bdot(w, S)
        o = _bdot(qc * egc, S) + _bdot(attn, v_new)
        S = S * jnp.exp(gl) + _bdot_tn(kd, v_new)
        o = o * lax.rsqrt(jnp.mean(o * o, axis=-1, keepdims=True) + EPS) * nw
        go_ref[0, r0:r0 + c, :] = o * _silu(z[r0:r0 + c])
    s_scr[...] = S

    @pl.when(t == n_t - 1)
    def _():
        so_ref[0, 0] = S
        e0 = base - halo + (t_valid - (n_t - 1) * ct)
        ncq_ref[0] = xq[e0:e0 + halo, :]
        nck_ref[0] = xk[e0:e0 + halo, :]
        ncv_ref[0] = xv[e0:e0 + halo, :]


def _gdn(qkv_pre, ab, z, conv0, s0, conv_w, gp, gnw, *, ct, t_valid):
    G, T, _ = qkv_pre.shape
    H, dk = GDN_HEADS, GDN_DK
    n_t = T // ct
    halo = GDN_CONV - 1
    tile = lambda off: pl.BlockSpec((1, ct, dk), lambda b, h, t: (b, t, h + off))
    cst = lambda off: pl.BlockSpec((1, halo, dk), lambda b, h, t: (b, 0, h + off))
    cw = lambda off: pl.BlockSpec((GDN_CONV, dk), lambda b, h, t: (0, h + off))
    st = pl.BlockSpec((1, 1, dk, dk), lambda b, h, t: (b, h, 0, 0))
    nco = pl.BlockSpec((1, halo, dk), lambda b, h, t: (b, 0, h))
    return pl.pallas_call(
        functools.partial(_gdn_kernel, ct=ct, c=GDN_CHUNK, t_valid=t_valid, n_t=n_t),
        grid=(G, H, n_t),
        in_specs=[tile(0), tile(H), tile(2 * H), cst(0), cst(H), cst(2 * H), cw(0), cw(H), cw(2 * H),
                  pl.BlockSpec((1, ct, LANES), lambda b, h, t: (b, t, 0)), tile(0), st,
                  pl.BlockSpec((8, LANES), lambda b, h, t: (0, 0)),
                  pl.BlockSpec((1, dk), lambda b, h, t: (0, 0))],
        out_specs=[tile(0), st, nco, nco, nco],
        out_shape=[jax.ShapeDtypeStruct((G, T, H * dk), F32), jax.ShapeDtypeStruct((G, H, dk, dk), F32),
                   jax.ShapeDtypeStruct((G, halo, H * dk), F32), jax.ShapeDtypeStruct((G, halo, H * dk), F32),
                   jax.ShapeDtypeStruct((G, halo, H * dk), F32)],
        scratch_shapes=[pltpu.VMEM((ct + 8, dk), F32)] * 3 + [pltpu.VMEM((dk, dk), F32)],
        compiler_params=pltpu.CompilerParams(dimension_semantics=("parallel", "parallel", "arbitrary"),
                                             vmem_limit_bytes=VMEM_LIMIT),
        name="gdn",
    )(qkv_pre, qkv_pre, qkv_pre, conv0, conv0, conv0, conv_w, conv_w, conv_w, ab, z, s0, gp, gnw)


def _alibi_slope(head_f):
    return jnp.exp2(-8.0 * (head_f + 1.0) / MOBA_HEADS)


def _moba_prompt_kernel(q_ref, kt_ref, vt_ref, o_ref, km_ref, m_scr, l_scr, acc_scr, *, nb, blk):
    p = pl.program_id(1)
    j = pl.program_id(2)
    hd = MOBA_HD
    lane1 = lax.broadcasted_iota(jnp.int32, (1, LANES), 1)

    @pl.when(j == 0)
    def _():
        km = jnp.zeros((2 * hd, LANES), F32)
        for n in range(nb):
            col = jnp.sum(kt_ref[0, :, n * blk:(n + 1) * blk], axis=1, keepdims=True) * (1.0 / blk)
            km = jnp.where(lane1 == n, col, km)
        km_ref[...] = km

    q = q_ref[0]
    lane = lax.broadcasted_iota(jnp.int32, (blk, LANES), 1)
    lanef = lane.astype(F32)
    ri = lax.broadcasted_iota(jnp.int32, (blk, blk), 0)
    ci = lax.broadcasted_iota(jnp.int32, (blk, blk), 1)
    rel = (ri - ci).astype(F32)
    causal = jnp.where(ri >= ci, 1.0, 0.0)
    pf = jnp.full((1, 1), p, jnp.int32).astype(F32)
    qh, sel, slope = [], [], []
    for hh in range(2):
        in_head = (lane >= hh * hd) & (lane < (hh + 1) * hd)
        qx = jnp.where(in_head, q, 0.0)
        qh.append(qx.astype(BF16))
        slope.append(_alibi_slope(2.0 * pf + hh))
        gate = jnp.where(lane < j, _dot3(qx, km_ref[...]), -jnp.inf)
        s_h = jnp.zeros((blk, LANES), F32)
        for _ in range(MOBA_TOPK):
            mx = jnp.max(gate, axis=1, keepdims=True)
            idx = jnp.min(jnp.where(gate == mx, lanef, 1e9), axis=1, keepdims=True)
            pick = (lanef == idx) & (mx > -jnp.inf)
            s_h = jnp.where(pick, 1.0, s_h)
            gate = jnp.where(lanef == idx, -jnp.inf, gate)
        sel.append(s_h)
        m_scr[hh] = jnp.full((blk, 1), NEG, F32)
        l_scr[hh] = jnp.zeros((blk, 1), F32)
        acc_scr[hh] = jnp.zeros((blk, LANES), F32)

    for n in range(nb):
        @pl.when(n <= j)
        def _(n=n):
            kn = kt_ref[0, :, n * blk:(n + 1) * blk].astype(BF16)
            vn = vt_ref[0, :, n * blk:(n + 1) * blk].astype(BF16)
            dist = ((j - n) * blk).astype(F32) + rel
            own = n == j
            for hh in range(2):
                s = jnp.dot(qh[hh], kn, preferred_element_type=F32) - slope[hh] * dist
                seln = jnp.max(jnp.where(lane == n, sel[hh], 0.0), axis=1, keepdims=True)
                keep = jnp.where(own, causal, seln)
                s = jnp.where(keep > 0.5, s, NEG)
                m_old = m_scr[hh]
                m_new = jnp.maximum(m_old, jnp.max(s, axis=1, keepdims=True))
                alpha = jnp.exp(m_old - m_new)
                pr = jnp.exp(s - m_new)
                l_scr[hh] = alpha * l_scr[hh] + jnp.sum(pr, axis=1, keepdims=True)
                acc_scr[hh] = alpha * acc_scr[hh] + _bdot_nt(pr, vn)
                m_scr[hh] = m_new

    o_ref[0] = jnp.where(lane < hd, acc_scr[0] / l_scr[0], acc_scr[1] / l_scr[1])


def _moba_prompt(mq, kt, vt):
    B, T, W = mq.shape
    blk = MOBA_BLOCK
    nb = T // blk
    qspec = pl.BlockSpec((1, blk, LANES), lambda b, p, j: (b, j, p))
    kspec = pl.BlockSpec((1, LANES, T), lambda b, p, j: (b, p, 0))
    return pl.pallas_call(
        functools.partial(_moba_prompt_kernel, nb=nb, blk=blk),
        grid=(B, W // LANES, nb),
        in_specs=[qspec, kspec, kspec],
        out_specs=qspec,
        out_shape=jax.ShapeDtypeStruct((B, T, W), F32),
        scratch_shapes=[pltpu.VMEM((LANES, LANES), F32), pltpu.VMEM((2, blk, 1), F32),
                        pltpu.VMEM((2, blk, 1), F32), pltpu.VMEM((2, blk, LANES), F32)],
        compiler_params=pltpu.CompilerParams(dimension_semantics=("parallel", "parallel", "arbitrary"),
                                             vmem_limit_bytes=VMEM_LIMIT),
        name="moba_prompt",
    )(mq, kt, vt)


def _kmean_topk_kernel(pt_ref, *refs, n_pg, n_steps, ppb, nblk):
    pages = refs[:n_pg]
    q_ref, idx_ref, km_ref = refs[n_pg:]
    i = pl.program_id(1)
    W = MOBA_HEADS * MOBA_HD
    lane1 = lax.broadcasted_iota(jnp.int32, (1, LANES), 1)

    @pl.when(i == 0)
    def _():
        km_ref[...] = jnp.zeros((W, LANES), F32)

    km = km_ref[...]
    for bb in range(n_pg // ppb):
        tot = pages[bb * ppb][...]
        for r in range(1, ppb):
            tot = tot + pages[bb * ppb + r][...]
        col = jnp.sum(tot.reshape(W, PAGE_SIZE), axis=1, keepdims=True) * (1.0 / (ppb * PAGE_SIZE))
        km = jnp.where(lane1 == i * (n_pg // ppb) + bb, col, km)
    km_ref[...] = km

    @pl.when(i == n_steps - 1)
    def _():
        rows = q_ref.shape[1]
        lanef = lax.broadcasted_iota(jnp.int32, (rows, LANES), 1).astype(F32)
        for h in range(MOBA_HEADS):
            qh = q_ref[0, :, h * MOBA_HD:(h + 1) * MOBA_HD]
            gate = _dot3(qh, km[h * MOBA_HD:(h + 1) * MOBA_HD, :])
            gate = jnp.where(lanef < nblk, gate, -jnp.inf)
            out = jnp.zeros((rows, LANES), F32)
            for r in range(MOBA_TOPK):
                mx = jnp.max(gate, axis=1, keepdims=True)
                idx = jnp.min(jnp.where(gate == mx, lanef, 1e9), axis=1, keepdims=True)
                out = jnp.where(lanef == r, idx, out)
                gate = jnp.where(lanef == idx, -jnp.inf, gate)
            idx_ref[0, h] = out.astype(jnp.int32)


def _kmean_topk(page_table, cache_kt, mq_pad, *, n_pg=8):
    DB, n_pages = page_table.shape
    ppb = MOBA_BLOCK // PAGE_SIZE
    nblk = n_pages // ppb
    n_steps = n_pages // n_pg
    rows = mq_pad.shape[1]
    W = MOBA_HEADS * MOBA_HD

    def page_spec(r):
        return pl.BlockSpec((None, MOBA_HEADS, MOBA_HD, PAGE_SIZE),
                            lambda b, i, pt: (pt[b * n_pages + i * n_pg + r], 0, 0, 0))

    grid_spec = pltpu.PrefetchScalarGridSpec(
        num_scalar_prefetch=1,
        grid=(DB, n_steps),
        in_specs=[page_spec(r) for r in range(n_pg)] + [pl.BlockSpec((1, rows, W), lambda b, i, pt: (b, 0, 0))],
        out_specs=pl.BlockSpec((1, MOBA_HEADS, rows, LANES), lambda b, i, pt: (b, 0, 0, 0)),
        scratch_shapes=[pltpu.VMEM((W, LANES), F32)],
    )
    return pl.pallas_call(
        functools.partial(_kmean_topk_kernel, n_pg=n_pg, n_steps=n_steps, ppb=ppb, nblk=nblk),
        grid_spec=grid_spec,
        out_shape=jax.ShapeDtypeStruct((DB, MOBA_HEADS, rows, LANES), jnp.int32),
        compiler_params=pltpu.CompilerParams(dimension_semantics=("parallel", "arbitrary"),
                                             vmem_limit_bytes=VMEM_LIMIT),
        name="kmean_topk",
    )(page_table.reshape(-1), *([cache_kt] * n_pg), mq_pad)


def _decode_attn_kernel(phys_ref, idx_ref, qt_ref, kown_ref, vown_ref, ck_hbm, cv_hbm, o_ref,
                        kbuf, vbuf, sem, *, ds, past, n_slab):
    b = pl.program_id(0)
    nb = pl.num_programs(0)
    hd = MOBA_HD
    ppb = MOBA_BLOCK // PAGE_SIZE
    per_head = n_slab // MOBA_HEADS

    def copies(bi, slot, u):
        pg = phys_ref[bi * n_slab + u]
        hh = u // per_head
        return (pltpu.make_async_copy(ck_hbm.at[pg, hh], kbuf.at[slot, u], sem.at[slot, 0]),
                pltpu.make_async_copy(cv_hbm.at[pg, hh], vbuf.at[slot, u], sem.at[slot, 1]))

    def start_all(bi, slot):
        def body(u, carry):
            ck, cv = copies(bi, slot, u)
            ck.start()
            cv.start()
            return carry
        lax.fori_loop(0, n_slab, body, 0)

    def wait_all(bi, slot):
        def body(u, carry):
            ck, cv = copies(bi, slot, u)
            ck.wait()
            cv.wait()
            return carry
        lax.fori_loop(0, n_slab, body, 0)

    slot = b % 2

    @pl.when(b == 0)
    def _():
        start_all(b, slot)

    @pl.when(b + 1 < nb)
    def _():
        start_all(b + 1, 1 - slot)

    wait_all(b, slot)

    lane = lax.broadcasted_iota(jnp.int32, (1, LANES), 1)
    lanef = lane.astype(F32)

    def head_body(h, carry):
        r0 = pl.multiple_of(h * hd, hd)
        slope = _alibi_slope(jnp.full((1, 1), h, jnp.int32).astype(F32))
        kown = kown_ref[0, pl.ds(r0, hd), :]
        vown = vown_ref[0, pl.ds(r0, hd), :]
        qt = qt_ref[0, pl.ds(r0, hd), :]
        otile = jnp.zeros((hd, LANES), F32)
        for t in range(ds):
            qb = jnp.broadcast_to(qt[:, t:t + 1], (hd, LANES))
            s_own = jnp.sum(qb * kown, axis=0, keepdims=True)
            s_own = jnp.where(lane <= t, s_own - slope * (t - lanef), NEG)
            scores = [s_own]
            for s in range(MOBA_TOPK):
                blk_id = idx_ref[(b * MOBA_HEADS + h) * (ds * MOBA_TOPK) + t * MOBA_TOPK + s]
                for r in range(ppb):
                    u = ((h * ds + t) * MOBA_TOPK + s) * ppb + r
                    sc = jnp.sum(qb * kbuf[slot, u], axis=0, keepdims=True)
                    start = (blk_id * MOBA_BLOCK + r * PAGE_SIZE).astype(F32)
                    scores.append(sc - slope * ((past + t) - start - lanef))
            mx = scores[0]
            for sc in scores[1:]:
                mx = jnp.maximum(mx, sc)
            mx = jnp.max(mx, axis=1, keepdims=True)
            pr = jnp.exp(scores[0] - mx)
            den = pr
            acc = pr * vown
            i = 1
            for s in range(MOBA_TOPK):
                for r in range(ppb):
                    u = ((h * ds + t) * MOBA_TOPK + s) * ppb + r
                    pr = jnp.exp(scores[i] - mx)
                    den = den + pr
                    acc = acc + pr * vbuf[slot, u]
                    i += 1
            o_col = jnp.sum(acc, axis=1, keepdims=True) / jnp.sum(den, axis=1, keepdims=True)
            otile = jnp.where(lane == t, o_col, otile)
        o_ref[0, pl.ds(r0, hd), :] = otile
        return carry

    lax.fori_loop(0, MOBA_HEADS, head_body, 0)


def _decode_attn(phys, idx, qt, kown_t, vown_t, cache_kt, cache_vt, *, ds, past):
    DB, W, _ = qt.shape
    n_slab = MOBA_HEADS * ds * MOBA_TOPK * (MOBA_BLOCK // PAGE_SIZE)
    tile = pl.BlockSpec((1, W, LANES), lambda b, ph, ix: (b, 0, 0))
    grid_spec = pltpu.PrefetchScalarGridSpec(
        num_scalar_prefetch=2,
        grid=(DB,),
        in_specs=[tile, tile, tile, pl.BlockSpec(memory_space=pl.ANY), pl.BlockSpec(memory_space=pl.ANY)],
        out_specs=tile,
        scratch_shapes=[pltpu.VMEM((2, n_slab, MOBA_HD, PAGE_SIZE), F32),
                        pltpu.VMEM((2, n_slab, MOBA_HD, PAGE_SIZE), F32),
                        pltpu.SemaphoreType.DMA((2, 2))],
    )
    return pl.pallas_call(
        functools.partial(_decode_attn_kernel, ds=ds, past=past, n_slab=n_slab),
        grid_spec=grid_spec,
        out_shape=jax.ShapeDtypeStruct((DB, W, LANES), F32),
        compiler_params=pltpu.CompilerParams(dimension_semantics=("arbitrary",),
                                             vmem_limit_bytes=VMEM_LIMIT),
        name="decode_attn",
    )(phys, idx, qt, kown_t, vown_t, cache_kt, cache_vt)


def _tokens_on_lanes(a, db, ds):
    w = a.shape[0]
    a = a.reshape(w, db, ds).transpose(1, 0, 2)
    return jnp.pad(a, ((0, 0), (0, 0), (0, LANES - ds)))


def _layer_weights(w_in, conv_w, a_log, dt_bias, ffn_up, ffn_down, w_out):
    hq = GDN_HEADS * GDN_DK
    n_qkv = 3 * hq
    o_a = n_qkv
    o_z = o_a + 2 * GDN_HEADS
    o_m = o_z + hq
    n_m = MOBA_HEADS * MOBA_HD
    wa = jnp.concatenate([w_in[:, :n_qkv], w_in[:, o_z:o_m + n_m]], axis=1).astype(BF16)
    wkv_t = w_in[:, o_m + n_m:].T.astype(BF16)
    wab = jnp.pad(w_in[:, o_a:o_z], ((0, 0), (0, LANES - 2 * GDN_HEADS)))
    wab_hi = wab.astype(BF16)
    wab_lo = (wab - wab_hi.astype(F32)).astype(BF16)
    gp = jnp.zeros((8, LANES), F32).at[0, :GDN_HEADS].set(a_log).at[1, :GDN_HEADS].set(dt_bias)
    return dict(wa=wa, wkv_t=wkv_t, wab_hi=wab_hi, wab_lo=wab_lo, gp=gp, conv_w=conv_w,
                up=[ffn_up[i].astype(BF16) for i in range(2)],
                down=[ffn_down[i].astype(BF16) for i in range(2)],
                wo=w_out.astype(BF16), n_qkv=n_qkv, n_z=hq, n_m=n_m)


def _trunk(x, mods, lw, norm_w, gdn_norm_w, conv0, s0, attend, *, tm, ct, t_valid, final_w):
    sh1, sc1, g1, sh2, sc2, g2, sh3, sc3, g3 = mods
    h = _ffn(x, sh1, sc1, g1, norm_w[0], lw["up"][0], lw["down"][0], tm=tm)
    qkv_pre, z, mq, kt, vt, ab = _win(h, sh2, sc2, norm_w[1], lw["wa"], lw["wkv_t"], lw["wab_hi"],
                                      lw["wab_lo"], tm=tm, n_qkv=lw["n_qkv"], n_z=lw["n_z"], n_m=lw["n_m"])
    go, s_new, ncq, nck, ncv = _gdn(*attend["gdn_in"](qkv_pre, ab, z), conv0, s0, lw["conv_w"], lw["gp"],
                                    gdn_norm_w.reshape(1, -1), ct=ct, t_valid=t_valid)
    go = attend["gdn_out"](go)
    mo = attend["moba"](mq, kt, vt)
    h = _ffn(h, sh3, sc3, g3, norm_w[2], lw["up"][1], lw["down"][1], tm=tm, mix=(go, mo, g2, lw["wo"]),
             final_w=final_w)
    return h, s_new, jnp.concatenate([ncq, nck, ncv], axis=-1), kt, vt


def kernel(x_prompt, x_sample, c_prompt, c_sample, cache_k, cache_v, page_table, state_gdn, state_conv,
           w_ada, b_ada, norm_w, ffn_up, ffn_down, w_in, conv_w, a_log, dt_bias, gdn_norm_w, w_out,
           final_norm_w):
    B, T, D = x_prompt.shape
    DB, DS, _ = x_sample.shape
    depth = w_ada.shape[0]
    past = page_table.shape[1] * PAGE_SIZE
    ppb = MOBA_BLOCK // PAGE_SIZE
    assert past % MOBA_BLOCK == 0 and past // MOBA_BLOCK >= MOBA_TOPK and DS <= PAGE_SIZE
    assert T % MOBA_BLOCK == 0 and (T - 1) // MOBA_BLOCK >= MOBA_TOPK
    NS = DB * DS
    hq = GDN_HEADS * GDN_DK
    n_m = MOBA_HEADS * MOBA_HD
    ct_s = LANES

    hp = x_prompt
    hs = x_sample.reshape(1, NS, D)
    outs = [[] for _ in range(8)]
    for l in range(depth):
        lw = _layer_weights(w_in[l], conv_w[l], a_log[l], dt_bias[l], ffn_up[l], ffn_down[l], w_out[l])
        mod = _ada_mod(jnp.concatenate([c_prompt, c_sample], axis=0), w_ada[l], b_ada[l])
        mod_p = mod[:B].reshape(B, N_MOD, 1, D)
        mod_s = jnp.repeat(mod[B:].reshape(DB, N_MOD, D), DS, axis=0).reshape(1, NS, N_MOD, D)
        mods_p = [mod_p[:, i] for i in range(N_MOD)]
        mods_s = [mod_s[:, :, i] for i in range(N_MOD)]
        last = l == depth - 1
        fw = final_norm_w if last else None

        attend_p = dict(gdn_in=lambda qkv, ab, z: (qkv, ab, z), gdn_out=lambda go: go, moba=_moba_prompt)
        hp, s_p, c_p, kt_p, vt_p = _trunk(
            hp, mods_p, lw, norm_w[l], gdn_norm_w[l], jnp.zeros((B, GDN_CONV - 1, 3 * hq), F32),
            jnp.zeros((B, GDN_HEADS, GDN_DK, GDN_DK), F32), attend_p, tm=512, ct=256, t_valid=T, final_w=fw)

        cache_kt = jnp.transpose(cache_k[l], (0, 2, 3, 1))
        cache_vt = jnp.transpose(cache_v[l], (0, 2, 3, 1))
        page_tbl = page_table

        def pad_seq(a):
            return jnp.pad(a.reshape(DB, DS, a.shape[-1]), ((0, 0), (0, ct_s - DS), (0, 0)))

        def moba_s(mq, kt, vt, cache_kt=cache_kt, cache_vt=cache_vt, page_tbl=page_tbl):
            mq_pad = jnp.pad(mq.reshape(DB, DS, n_m), ((0, 0), (0, 16 - DS), (0, 0)))
            idx = _kmean_topk(page_tbl, cache_kt, mq_pad)[:, :, :DS, :MOBA_TOPK]
            lpage = idx[..., None] * ppb + jnp.arange(ppb, dtype=jnp.int32)
            phys = jnp.take_along_axis(page_tbl, lpage.reshape(DB, -1), axis=1)
            qt = _tokens_on_lanes(mq[0].T, DB, DS)
            ot = _decode_attn(phys.reshape(-1), idx.reshape(-1), qt, _tokens_on_lanes(kt[0], DB, DS),
                              _tokens_on_lanes(vt[0], DB, DS), cache_kt, cache_vt, ds=DS, past=past)
            return ot[:, :, :DS].transpose(0, 2, 1).reshape(1, NS, n_m)

        attend_s = dict(gdn_in=lambda qkv, ab, z: (pad_seq(qkv), pad_seq(ab), pad_seq(z)),
                        gdn_out=lambda go: go[:, :DS].reshape(1, NS, hq), moba=moba_s)
        hs, s_s, c_s, kt_s, vt_s = _trunk(
            hs, mods_s, lw, norm_w[l], gdn_norm_w[l], state_conv[l], state_gdn[l], attend_s,
            tm=NS, ct=ct_s, t_valid=DS, final_w=fw)

        def heads_last(a_t, b, t):
            return a_t.reshape(b, MOBA_HEADS, MOBA_HD, t).transpose(0, 3, 1, 2)

        new = [heads_last(kt_p, B, T), heads_last(vt_p, B, T), s_p, c_p,
               heads_last(kt_s.reshape(n_m, DB, DS).transpose(1, 0, 2), DB, DS),
               heads_last(vt_s.reshape(n_m, DB, DS).transpose(1, 0, 2), DB, DS), s_s, c_s]
        for o, v in zip(outs, new):
            o.append(v)
    y_prompt = hp
    y_sample = hs.reshape(DB, DS, D)
    return (y_prompt, y_sample) + tuple(jnp.stack(o) for o in outs)
```

```python
import functools

import jax
import jax.numpy as jnp
from jax import lax
from jax.experimental import pallas as pl
from jax.experimental.pallas import tpu as pltpu

F32 = jnp.float32
BF16 = jnp.bfloat16
EPS = 1e-6
NEG = -1e30

GDN_HEADS = 4
GDN_DK = 128
GDN_CONV = 4
GDN_CHUNK = 64
MOBA_HEADS = 8
MOBA_HD = 64
MOBA_BLOCK = 256
MOBA_TOPK = 3
PAGE_SIZE = 128
N_MOD = 9
LANES = 128
VMEM_LIMIT = 56 * 1024 * 1024


def _sigmoid(x):
    return 1.0 / (1.0 + jnp.exp(-x))


def _silu(x):
    return x * _sigmoid(x)


def _bdot(a, b):
    return jnp.dot(a.astype(BF16), b.astype(BF16), preferred_element_type=F32)


def _bdot_nt(a, b):
    return lax.dot_general(a.astype(BF16), b.astype(BF16), (((1,), (1,)), ((), ())),
                           preferred_element_type=F32)


def _split(x):
    hi = x.astype(BF16)
    lo = (x - hi.astype(F32)).astype(BF16)
    return hi, lo


def _dot3(a, b):
    a_hi, a_lo = _split(a)
    b_hi, b_lo = _split(b)
    d = functools.partial(jnp.dot, preferred_element_type=F32)
    return d(a_hi, b_hi) + (d(a_lo, b_hi) + d(a_hi, b_lo))


def _norm_mod(x, nw, sc, sh):
    y = x * lax.rsqrt(jnp.mean(x * x, axis=-1, keepdims=True) + EPS)
    return (y * nw) * (1.0 + sc) + sh


def _mod_spec(rows, tm, d):
    if rows == 1:
        return pl.BlockSpec((1, 1, d), lambda g, t: (g, 0, 0))
    return pl.BlockSpec((1, tm, d), lambda g, t: (g, t, 0))


def _resident(shape):
    nd = len(shape)
    return pl.BlockSpec(shape, lambda *_: (0,) * nd, pipeline_mode=pl.Buffered(1))


def _ada_kernel(c_ref, w_ref, b_ref, o_ref):
    o_ref[...] = _dot3(_silu(c_ref[...]), w_ref[...]) + b_ref[...]


def _ada_mod(c_all, w, b):
    n, d = c_all.shape
    dout = w.shape[1]
    tn = d
    return pl.pallas_call(
        _ada_kernel,
        grid=(dout // tn,),
        in_specs=[pl.BlockSpec((n, d), lambda i: (0, 0)),
                  pl.BlockSpec((d, tn), lambda i: (0, i)),
                  pl.BlockSpec((1, tn), lambda i: (0, i))],
        out_specs=pl.BlockSpec((n, tn), lambda i: (0, i)),
        out_shape=jax.ShapeDtypeStruct((n, dout), F32),
        compiler_params=pltpu.CompilerParams(dimension_semantics=("arbitrary",),
                                             vmem_limit_bytes=VMEM_LIMIT),
        name="ada_mod",
    )(c_all, w, b.reshape(1, dout))


def _ffn_kernel(*refs, tf, has_mix, has_final):
    it = iter(refs)
    x_ref, sh_ref, sc_ref, g_ref, nw_ref, wup_ref, wdn_ref = [next(it) for _ in range(7)]
    if has_mix:
        go_ref, mo_ref, g2_ref, wo_ref = [next(it) for _ in range(4)]
    if has_final:
        fw_ref = next(it)
    o_ref = next(it)
    a_scr = next(it)

    x = x_ref[0]
    if has_mix:
        half = go_ref.shape[-1]
        mix = _bdot(go_ref[0], wo_ref[0:half, :]) + _bdot(mo_ref[0], wo_ref[half:, :])
        x = x + g2_ref[0] * mix
    hb = _norm_mod(x, nw_ref[...], sc_ref[0], sh_ref[0]).astype(BF16)
    d_ff = wdn_ref.shape[0]
    for c in range(d_ff // tf):
        gate = jnp.dot(hb, wup_ref[:, c * tf:(c + 1) * tf], preferred_element_type=F32)
        up = jnp.dot(hb, wup_ref[:, d_ff + c * tf:d_ff + (c + 1) * tf], preferred_element_type=F32)
        a_scr[:, c * tf:(c + 1) * tf] = (_silu(gate) * up).astype(BF16)
    down = jnp.dot(a_scr[...], wdn_ref[...], preferred_element_type=F32)
    out = x + (0.5 * g_ref[0]) * down
    if has_final:
        out = out * lax.rsqrt(jnp.mean(out * out, axis=-1, keepdims=True) + EPS) * fw_ref[...]
    o_ref[0] = out


def _ffn(x, sh, sc, g, nw, wup, wdn, *, tm, mix=None, final_w=None):
    G, T, D = x.shape
    d_ff = wdn.shape[0]
    tf = d_ff // 2 if (d_ff // 2) % LANES == 0 else d_ff
    row = pl.BlockSpec((1, tm, D), lambda g_, t: (g_, t, 0))
    ins = [x, sh, sc, g, nw.reshape(1, D), wup, wdn]
    specs = [row, _mod_spec(sh.shape[1], tm, D), _mod_spec(sc.shape[1], tm, D),
             _mod_spec(g.shape[1], tm, D), _resident((1, D)), _resident(wup.shape), _resident(wdn.shape)]
    if mix is not None:
        go, mo, g2, wo = mix
        half = go.shape[-1]
        ins += [go, mo, g2, wo]
        specs += [pl.BlockSpec((1, tm, half), lambda g_, t: (g_, t, 0)),
                  pl.BlockSpec((1, tm, half), lambda g_, t: (g_, t, 0)),
                  _mod_spec(g2.shape[1], tm, D), _resident(wo.shape)]
    if final_w is not None:
        ins.append(final_w.reshape(1, D))
        specs.append(_resident((1, D)))
    return pl.pallas_call(
        functools.partial(_ffn_kernel, tf=tf, has_mix=mix is not None, has_final=final_w is not None),
        grid=(G, T // tm),
        in_specs=specs,
        out_specs=row,
        out_shape=jax.ShapeDtypeStruct((G, T, D), F32),
        scratch_shapes=[pltpu.VMEM((tm, d_ff), BF16)],
        compiler_params=pltpu.CompilerParams(dimension_semantics=("parallel", "parallel"),
                                             vmem_limit_bytes=VMEM_LIMIT),
        name="ffn_mix" if mix is not None else "ffn",
    )(*ins)


def _win_kernel(h_ref, sh_ref, sc_ref, nw_ref, wa_ref, wkv_ref, wabh_ref, wabl_ref,
                qkv_ref, z_ref, mq_ref, kt_ref, vt_ref, ab_ref, *, n_qkv, n_z, n_m):
    hn = _norm_mod(h_ref[0], nw_ref[...], sc_ref[0], sh_ref[0])
    hb = hn.astype(BF16)
    h_lo = (hn - hb.astype(F32)).astype(BF16)
    d = functools.partial(jnp.dot, preferred_element_type=F32)
    qkv_ref[0] = d(hb, wa_ref[:, 0:n_qkv])
    z_ref[0] = d(hb, wa_ref[:, n_qkv:n_qkv + n_z])
    mq_ref[0] = d(hb, wa_ref[:, n_qkv + n_z:n_qkv + n_z + n_m]) * (MOBA_HD ** -0.5)
    nt = functools.partial(lax.dot_general, dimension_numbers=(((1,), (1,)), ((), ())),
                           preferred_element_type=F32)
    kt_ref[0] = nt(wkv_ref[0:n_m, :], hb)
    vt_ref[0] = nt(wkv_ref[n_m:, :], hb)
    ab_ref[0] = d(hb, wabh_ref[...]) + (d(h_lo, wabh_ref[...]) + d(hb, wabl_ref[...]))


def _win(h, sh, sc, nw, wa, wkv_t, wab_hi, wab_lo, *, tm, n_qkv, n_z, n_m):
    G, T, D = h.shape
    row = lambda n: pl.BlockSpec((1, tm, n), lambda g_, t: (g_, t, 0))
    col = pl.BlockSpec((1, n_m, tm), lambda g_, t: (g_, 0, t))
    return pl.pallas_call(
        functools.partial(_win_kernel, n_qkv=n_qkv, n_z=n_z, n_m=n_m),
        grid=(G, T // tm),
        in_specs=[row(D), _mod_spec(sh.shape[1], tm, D), _mod_spec(sc.shape[1], tm, D),
                  _resident((1, D)), _resident(wa.shape), _resident(wkv_t.shape),
                  _resident(wab_hi.shape), _resident(wab_lo.shape)],
        out_specs=[row(n_qkv), row(n_z), row(n_m), col, col, row(LANES)],
        out_shape=[jax.ShapeDtypeStruct((G, T, n_qkv), F32), jax.ShapeDtypeStruct((G, T, n_z), F32),
                   jax.ShapeDtypeStruct((G, T, n_m), F32), jax.ShapeDtypeStruct((G, n_m, T), F32),
                   jax.ShapeDtypeStruct((G, n_m, T), F32), jax.ShapeDtypeStruct((G, T, LANES), F32)],
        compiler_params=pltpu.CompilerParams(dimension_semantics=("parallel", "parallel"),
                                             vmem_limit_bytes=VMEM_LIMIT),
        name="w_in",
    )(h, sh, sc, nw.reshape(1, D), wa, wkv_t, wab_hi, wab_lo)


TRI_BASE = 8


def _bmm(a, b):
    return lax.dot_general(a.astype(BF16), b.astype(BF16), (((2,), (1,)), ((0,), (0,))),
                           preferred_element_type=F32)


def _bmm_nt(a, b):
    return lax.dot_general(a.astype(BF16), b.astype(BF16), (((2,), (2,)), ((0,), (0,))),
                           preferred_element_type=F32)


def _bmm_tn(a, b):
    return lax.dot_general(a.astype(BF16), b.astype(BF16), (((1,), (1,)), ((0,), (0,))),
                           preferred_element_type=F32)


def _unit_lower_inverse(m, eye, ri, ci, c):
    size = TRI_BASE
    a = jnp.where(ri // size == ci // size, m, 0.0)
    t = eye - a
    p = a
    e = 2
    while e < size:
        p = _bmm(p, p)
        t = t + _bmm(t, p)
        e *= 2
    while size < c:
        f = jnp.where((ri // size != ci // size) & (ri // (2 * size) == ci // (2 * size)), m, 0.0)
        t = t - _bmm(t, _bmm(f, t))
        size *= 2
    return t


def _gdn_kernel(x_ref, c0_ref, cw_ref, ab_ref, z_ref, s0_ref, gp_ref, nw_ref, go_ref, so_ref, nc_ref,
                xbuf, s_scr, *, ct, c, t_valid, n_t):
    t = pl.program_id(1)
    H, dk = GDN_HEADS, GDN_DK
    halo = GDN_CONV - 1
    base = 8

    @pl.when(t == 0)
    def _():
        xbuf[base - halo:base, :] = c0_ref[0]
        s_scr[...] = s0_ref[0]

    @pl.when(t > 0)
    def _():
        xbuf[base - halo:base, :] = xbuf[ct + base - halo:ct + base, :]

    xbuf[base:base + ct, :] = x_ref[0]
    acc = cw_ref[0:1, :] * xbuf[base - halo:base - halo + ct, :]
    for i in range(1, GDN_CONV):
        acc = acc + cw_ref[i:i + 1, :] * xbuf[base - halo + i:base - halo + i + ct, :]
    qkv = _silu(acc)

    def heads(off):
        return jnp.stack([qkv[:, (off + h) * dk:(off + h + 1) * dk] for h in range(H)])

    q, k, v = heads(0), heads(H), heads(2 * H)
    q = q * lax.rsqrt(jnp.sum(q * q, axis=-1, keepdims=True) + EPS) * (dk ** -0.5)
    k = k * lax.rsqrt(jnp.sum(k * k, axis=-1, keepdims=True) + EPS)

    ab = ab_ref[0]
    rowi = lax.broadcasted_iota(jnp.int32, (ct, LANES), 0)
    xg = ab + gp_ref[1:2, :]
    softplus = jnp.maximum(xg, 0.0) + jnp.log(1.0 + jnp.exp(-jnp.abs(xg)))
    g_all = -jnp.exp(gp_ref[0:1, :]) * softplus
    b_all = _sigmoid(ab)
    if t_valid % ct != 0:
        valid = (rowi + t * ct) < t_valid
        g_all = jnp.where(valid, g_all, 0.0)
        b_all = jnp.where(valid, b_all, 0.0)
    rin = rowi % c
    gc_all = g_all
    s = 1
    while s < c:
        gc_all = gc_all + jnp.where(rin >= s, pltpu.roll(gc_all, s, axis=0), 0.0)
        s *= 2
    gct = jnp.transpose(gc_all)
    gcol = jnp.stack([gc_all[:, h:h + 1] for h in range(H)])
    grow = jnp.stack([gct[h:h + 1, :] for h in range(H)])
    beta = jnp.stack([b_all[:, H + h:H + h + 1] for h in range(H)])

    ri = lax.broadcasted_iota(jnp.int32, (ct, ct), 0)
    ci = lax.broadcasted_iota(jnp.int32, (ct, ct), 1)
    same = ri // c == ci // c
    eye = jnp.where(ri == ci, 1.0, 0.0)
    decay = jnp.exp(jnp.where(same & (ri >= ci), gcol - grow, -jnp.inf))
    kb = k * beta
    m = jnp.where(same & (ri > ci), _bmm_nt(kb, k) * decay, 0.0)
    tm1 = _unit_lower_inverse(m, eye, ri, ci, c) - eye
    egc = jnp.exp(gcol)
    rhs = jnp.concatenate([v * beta, kb * egc], axis=-1)
    uw = rhs + _bmm(tm1, rhs)
    u, w = uw[..., :dk], uw[..., dk:]
    attn = _bmm_nt(q, k) * decay
    qd = q * egc

    S = s_scr[...]
    o_state, v_new = [], []
    for cc in range(ct // c):
        r0 = cc * c
        if n_t == 1 and r0 >= t_valid:
            o_state.append(jnp.zeros((H, c, dk), F32))
            v_new.append(jnp.zeros((H, c, dk), F32))
            continue
        gl = gcol[:, r0 + c - 1:r0 + c, :]
        kd = k[:, r0:r0 + c] * jnp.exp(gl - gcol[:, r0:r0 + c])
        xs = _bmm(jnp.concatenate([w[:, r0:r0 + c], qd[:, r0:r0 + c]], axis=1), S)
        vn = u[:, r0:r0 + c] - xs[:, :c]
        o_state.append(xs[:, c:])
        v_new.append(vn)
        S = S * jnp.exp(gl) + _bmm_tn(kd, vn)
    s_scr[...] = S
    o = jnp.concatenate(o_state, axis=1) + _bmm(attn, jnp.concatenate(v_new, axis=1))
    o = o * lax.rsqrt(jnp.mean(o * o, axis=-1, keepdims=True) + EPS) * nw_ref[...]
    z = z_ref[0]
    for h in range(H):
        go_ref[0, :, h * dk:(h + 1) * dk] = o[h] * _silu(z[:, h * dk:(h + 1) * dk])

    @pl.when(t == n_t - 1)
    def _():
        so_ref[0] = S
        e0 = base - halo + (t_valid - (n_t - 1) * ct)
        nc_ref[0] = xbuf[e0:e0 + halo, :]


def _gdn(qkv_pre, ab, z, conv0, s0, conv_w, gp, gnw, *, ct, t_valid):
    G, T, W3 = qkv_pre.shape
    H, dk = GDN_HEADS, GDN_DK
    n_t = T // ct
    halo = GDN_CONV - 1
    tile = lambda n: pl.BlockSpec((1, ct, n), lambda b, t: (b, t, 0))
    per_seq = lambda shape: pl.BlockSpec((1,) + shape, lambda b, t: (b,) + (0,) * len(shape))
    whole = lambda shape: pl.BlockSpec(shape, lambda b, t: (0,) * len(shape))
    return pl.pallas_call(
        functools.partial(_gdn_kernel, ct=ct, c=GDN_CHUNK, t_valid=t_valid, n_t=n_t),
        grid=(G, n_t),
        in_specs=[tile(W3), per_seq((halo, W3)), whole((GDN_CONV, W3)), tile(LANES), tile(H * dk),
                  per_seq((H, dk, dk)), whole((8, LANES)), whole((1, dk))],
        out_specs=[tile(H * dk), per_seq((H, dk, dk)), per_seq((halo, W3))],
        out_shape=[jax.ShapeDtypeStruct((G, T, H * dk), F32), jax.ShapeDtypeStruct((G, H, dk, dk), F32),
                   jax.ShapeDtypeStruct((G, halo, W3), F32)],
        scratch_shapes=[pltpu.VMEM((ct + 8, W3), F32), pltpu.VMEM((H, dk, dk), F32)],
        compiler_params=pltpu.CompilerParams(dimension_semantics=("parallel", "arbitrary"),
                                             vmem_limit_bytes=VMEM_LIMIT),
        name="gdn",
    )(qkv_pre, conv0, conv_w, ab, z, s0, gp, gnw)


def _alibi_slope(head_f):
    return jnp.exp2(-8.0 * (head_f + 1.0) / MOBA_HEADS)


def _moba_prompt_kernel(q_ref, kt_ref, vt_ref, o_ref, km_ref, ka_ref, kb_ref, vb_ref, *, nb, blk):
    p = pl.program_id(1)
    j = pl.program_id(2)
    hd = MOBA_HD
    lane1 = lax.broadcasted_iota(jnp.int32, (1, LANES), 1)

    @pl.when(j == 0)
    def _():
        km = jnp.zeros((2 * hd, LANES), F32)
        rowk = lax.broadcasted_iota(jnp.int32, (2 * hd, blk), 0)
        for n in range(nb):
            kn = kt_ref[0, :, n * blk:(n + 1) * blk]
            km = jnp.where(lane1 == n, jnp.sum(kn, axis=1, keepdims=True) * (1.0 / blk), km)
            ka_ref[n] = jnp.where(rowk < hd, kn, jnp.where(rowk == hd + n, 1.0, 0.0)).astype(BF16)
            kb_ref[n] = jnp.where(rowk >= hd, kn, jnp.where(rowk == n, 1.0, 0.0)).astype(BF16)
            vb_ref[n] = vt_ref[0, :, n * blk:(n + 1) * blk].astype(BF16)
        km_ref[...] = km

    q = q_ref[0]
    lane = lax.broadcasted_iota(jnp.int32, (blk, LANES), 1)
    lanef = lane.astype(F32)
    colf = lax.broadcasted_iota(jnp.int32, (1, blk), 1).astype(F32)
    ri = lax.broadcasted_iota(jnp.int32, (blk, blk), 0)
    ci = lax.broadcasted_iota(jnp.int32, (blk, blk), 1)
    pf = jnp.full((1, 1), p, jnp.int32).astype(F32)
    k_refs = (ka_ref, kb_ref)
    qaug, slope = [], []
    for hh in range(2):
        in_head = (lane >= hh * hd) & (lane < (hh + 1) * hd)
        qx = jnp.where(in_head, q, 0.0)
        slope.append(_alibi_slope(2.0 * pf + hh))
        gate = jnp.where(lane < j, _dot3(qx, km_ref[...]), -jnp.inf)
        bias = jnp.where((lane == j) | (lane >= nb), 0.0, NEG)
        for _ in range(MOBA_TOPK):
            mx = jnp.max(gate, axis=1, keepdims=True)
            idx = jnp.min(jnp.where(gate == mx, lanef, 1e9), axis=1, keepdims=True)
            bias = jnp.where((lanef == idx) & (mx > -jnp.inf), 0.0, bias)
            gate = jnp.where(lanef == idx, -jnp.inf, gate)
        if hh == 0:
            bias = pltpu.roll(bias, hd, axis=1)
        qaug.append(jnp.where(in_head, q, bias).astype(BF16))

    def scores(hh, n, rel_blocks):
        s = jnp.dot(qaug[hh], k_refs[hh][n], preferred_element_type=F32)
        return s + slope[hh] * (rel_blocks * blk + colf)

    def attend(s, vn, m_old, l_old, acc_old):
        m_new = jnp.maximum(m_old, jnp.max(s, axis=1, keepdims=True))
        alpha = jnp.exp(m_old - m_new)
        pr = jnp.exp(s - m_new)
        l_new = alpha * l_old + jnp.sum(pr, axis=1, keepdims=True)
        return m_new, l_new, alpha * acc_old + _bdot_nt(pr, vn)

    vj = vb_ref[j]
    carry = []
    for hh in range(2):
        s = jnp.where(ri >= ci, scores(hh, j, 0.0), NEG)
        m0 = jnp.max(s, axis=1, keepdims=True)
        pr = jnp.exp(s - m0)
        carry += [m0, jnp.sum(pr, axis=1, keepdims=True), _bdot_nt(pr, vj)]

    def body(n, carry):
        vn = vb_ref[n]
        rel_blocks = (n - j).astype(F32)
        s0 = scores(0, n, rel_blocks)
        s1 = scores(1, n, rel_blocks)
        return attend(s0, vn, *carry[0:3]) + attend(s1, vn, *carry[3:6])

    carry = lax.fori_loop(0, j, body, tuple(carry))
    o_ref[0] = jnp.where(lane < hd, carry[2] / carry[1], carry[5] / carry[4])


def _moba_prompt(mq, kt, vt):
    B, T, W = mq.shape
    blk = MOBA_BLOCK
    nb = T // blk
    assert nb <= MOBA_HD
    qspec = pl.BlockSpec((1, blk, LANES), lambda b, p, j: (b, j, p))
    kspec = pl.BlockSpec((1, LANES, T), lambda b, p, j: (b, p, 0))
    return pl.pallas_call(
        functools.partial(_moba_prompt_kernel, nb=nb, blk=blk),
        grid=(B, W // LANES, nb),
        in_specs=[qspec, kspec, kspec],
        out_specs=qspec,
        out_shape=jax.ShapeDtypeStruct((B, T, W), F32),
        scratch_shapes=[pltpu.VMEM((LANES, LANES), F32)] + [pltpu.VMEM((nb, LANES, blk), BF16)] * 3,
        compiler_params=pltpu.CompilerParams(dimension_semantics=("parallel", "parallel", "arbitrary"),
                                             vmem_limit_bytes=VMEM_LIMIT),
        name="moba_prompt",
    )(mq, kt, vt)


def _kmean_topk_kernel(pt_ref, *refs, n_pg, n_steps, ppb, nblk):
    pages = refs[:n_pg]
    q_ref, idx_ref, km_ref = refs[n_pg:]
    i = pl.program_id(1)
    W = MOBA_HEADS * MOBA_HD
    lane1 = lax.broadcasted_iota(jnp.int32, (1, LANES), 1)

    @pl.when(i == 0)
    def _():
        km_ref[...] = jnp.zeros((W, LANES), F32)

    km = km_ref[...]
    for bb in range(n_pg // ppb):
        tot = pages[bb * ppb][...]
        for r in range(1, ppb):
            tot = tot + pages[bb * ppb + r][...]
        col = jnp.sum(tot.reshape(W, PAGE_SIZE), axis=1, keepdims=True) * (1.0 / (ppb * PAGE_SIZE))
        km = jnp.where(lane1 == i * (n_pg // ppb) + bb, col, km)
    km_ref[...] = km

    @pl.when(i == n_steps - 1)
    def _():
        rows = q_ref.shape[1]
        lanef = lax.broadcasted_iota(jnp.int32, (rows, LANES), 1).astype(F32)
        for h in range(MOBA_HEADS):
            qh = q_ref[0, :, h * MOBA_HD:(h + 1) * MOBA_HD]
            gate = _dot3(qh, km[h * MOBA_HD:(h + 1) * MOBA_HD, :])
            gate = jnp.where(lanef < nblk, gate, -jnp.inf)
            out = jnp.zeros((rows, LANES), F32)
            for r in range(MOBA_TOPK):
                mx = jnp.max(gate, axis=1, keepdims=True)
                idx = jnp.min(jnp.where(gate == mx, lanef, 1e9), axis=1, keepdims=True)
                out = jnp.where(lanef == r, idx, out)
                gate = jnp.where(lanef == idx, -jnp.inf, gate)
            idx_ref[0, h] = out.astype(jnp.int32)


def _kmean_topk(page_table, cache_kt, mq_pad, *, n_pg=8):
    DB, n_pages = page_table.shape
    ppb = MOBA_BLOCK // PAGE_SIZE
    nblk = n_pages // ppb
    n_steps = n_pages // n_pg
    rows = mq_pad.shape[1]
    W = MOBA_HEADS * MOBA_HD

    def page_spec(r):
        return pl.BlockSpec((None, MOBA_HEADS, MOBA_HD, PAGE_SIZE),
                            lambda b, i, pt: (pt[b * n_pages + i * n_pg + r], 0, 0, 0))

    grid_spec = pltpu.PrefetchScalarGridSpec(
        num_scalar_prefetch=1,
        grid=(DB, n_steps),
        in_specs=[page_spec(r) for r in range(n_pg)] + [pl.BlockSpec((1, rows, W), lambda b, i, pt: (b, 0, 0))],
        out_specs=pl.BlockSpec((1, MOBA_HEADS, rows, LANES), lambda b, i, pt: (b, 0, 0, 0)),
        scratch_shapes=[pltpu.VMEM((W, LANES), F32)],
    )
    return pl.pallas_call(
        functools.partial(_kmean_topk_kernel, n_pg=n_pg, n_steps=n_steps, ppb=ppb, nblk=nblk),
        grid_spec=grid_spec,
        out_shape=jax.ShapeDtypeStruct((DB, MOBA_HEADS, rows, LANES), jnp.int32),
        compiler_params=pltpu.CompilerParams(dimension_semantics=("parallel", "arbitrary"),
                                             vmem_limit_bytes=VMEM_LIMIT),
        name="kmean_topk",
    )(page_table.reshape(-1), *([cache_kt] * n_pg), mq_pad)


def _decode_attn_kernel(phys_ref, idx_ref, qt_ref, kown_ref, vown_ref, ck_hbm, cv_hbm, o_ref,
                        kbuf, vbuf, sem, *, ds, past, n_slab):
    b = pl.program_id(0)
    nb = pl.num_programs(0)
    hd = MOBA_HD
    ppb = MOBA_BLOCK // PAGE_SIZE
    per_head = n_slab // MOBA_HEADS

    def copies(bi, slot, u):
        pg = phys_ref[bi * n_slab + u]
        hh = u // per_head
        return (pltpu.make_async_copy(ck_hbm.at[pg, hh], kbuf.at[slot, u], sem.at[slot, 0]),
                pltpu.make_async_copy(cv_hbm.at[pg, hh], vbuf.at[slot, u], sem.at[slot, 1]))

    def start_all(bi, slot):
        def body(u, carry):
            ck, cv = copies(bi, slot, u)
            ck.start()
            cv.start()
            return carry
        lax.fori_loop(0, n_slab, body, 0)

    def wait_all(bi, slot):
        def body(u, carry):
            ck, cv = copies(bi, slot, u)
            ck.wait()
            cv.wait()
            return carry
        lax.fori_loop(0, n_slab, body, 0)

    slot = b % 2

    @pl.when(b == 0)
    def _():
        start_all(b, slot)

    @pl.when(b + 1 < nb)
    def _():
        start_all(b + 1, 1 - slot)

    wait_all(b, slot)

    lane = lax.broadcasted_iota(jnp.int32, (1, LANES), 1)
    lanef = lane.astype(F32)

    def head_body(h, carry):
        r0 = pl.multiple_of(h * hd, hd)
        slope = _alibi_slope(jnp.full((1, 1), h, jnp.int32).astype(F32))
        kown = kown_ref[0, pl.ds(r0, hd), :]
        vown = vown_ref[0, pl.ds(r0, hd), :]
        qt = qt_ref[0, pl.ds(r0, hd), :]
        otile = jnp.zeros((hd, LANES), F32)
        for t in range(ds):
            qb = jnp.broadcast_to(qt[:, t:t + 1], (hd, LANES))
            s_own = jnp.sum(qb * kown, axis=0, keepdims=True)
            s_own = jnp.where(lane <= t, s_own - slope * (t - lanef), NEG)
            scores = [s_own]
            for s in range(MOBA_TOPK):
                blk_id = idx_ref[(b * MOBA_HEADS + h) * (ds * MOBA_TOPK) + t * MOBA_TOPK + s]
                for r in range(ppb):
                    u = ((h * ds + t) * MOBA_TOPK + s) * ppb + r
                    sc = jnp.sum(qb * kbuf[slot, u], axis=0, keepdims=True)
                    start = (blk_id * MOBA_BLOCK + r * PAGE_SIZE).astype(F32)
                    scores.append(sc - slope * ((past + t) - start - lanef))
            mx = scores[0]
            for sc in scores[1:]:
                mx = jnp.maximum(mx, sc)
            mx = jnp.max(mx, axis=1, keepdims=True)
            pr = jnp.exp(scores[0] - mx)
            den = pr
            acc = pr * vown
            i = 1
            for s in range(MOBA_TOPK):
                for r in range(ppb):
                    u = ((h * ds + t) * MOBA_TOPK + s) * ppb + r
                    pr = jnp.exp(scores[i] - mx)
                    den = den + pr
                    acc = acc + pr * vbuf[slot, u]
                    i += 1
            o_col = jnp.sum(acc, axis=1, keepdims=True) / jnp.sum(den, axis=1, keepdims=True)
            otile = jnp.where(lane == t, o_col, otile)
        o_ref[0, pl.ds(r0, hd), :] = otile
        return carry

    lax.fori_loop(0, MOBA_HEADS, head_body, 0)


def _decode_attn(phys, idx, qt, kown_t, vown_t, cache_kt, cache_vt, *, ds, past):
    DB, W, _ = qt.shape
    n_slab = MOBA_HEADS * ds * MOBA_TOPK * (MOBA_BLOCK // PAGE_SIZE)
    tile = pl.BlockSpec((1, W, LANES), lambda b, ph, ix: (b, 0, 0))
    grid_spec = pltpu.PrefetchScalarGridSpec(
        num_scalar_prefetch=2,
        grid=(DB,),
        in_specs=[tile, tile, tile, pl.BlockSpec(memory_space=pl.ANY), pl.BlockSpec(memory_space=pl.ANY)],
        out_specs=tile,
        scratch_shapes=[pltpu.VMEM((2, n_slab, MOBA_HD, PAGE_SIZE), F32),
                        pltpu.VMEM((2, n_slab, MOBA_HD, PAGE_SIZE), F32),
                        pltpu.SemaphoreType.DMA((2, 2))],
    )
    return pl.pallas_call(
        functools.partial(_decode_attn_kernel, ds=ds, past=past, n_slab=n_slab),
        grid_spec=grid_spec,
        out_shape=jax.ShapeDtypeStruct((DB, W, LANES), F32),
        compiler_params=pltpu.CompilerParams(dimension_semantics=("arbitrary",),
                                             vmem_limit_bytes=VMEM_LIMIT),
        name="decode_attn",
    )(phys, idx, qt, kown_t, vown_t, cache_kt, cache_vt)


def _tokens_on_lanes(a, db, ds):
    w = a.shape[0]
    a = a.reshape(w, db, ds).transpose(1, 0, 2)
    return jnp.pad(a, ((0, 0), (0, 0), (0, LANES - ds)))


def _layer_weights(w_in, conv_w, a_log, dt_bias, ffn_up, ffn_down, w_out):
    hq = GDN_HEADS * GDN_DK
    n_qkv = 3 * hq
    o_a = n_qkv
    o_z = o_a + 2 * GDN_HEADS
    o_m = o_z + hq
    n_m = MOBA_HEADS * MOBA_HD
    wa = jnp.concatenate([w_in[:, :n_qkv], w_in[:, o_z:o_m + n_m]], axis=1).astype(BF16)
    wkv_t = w_in[:, o_m + n_m:].T.astype(BF16)
    wab = jnp.pad(w_in[:, o_a:o_z], ((0, 0), (0, LANES - 2 * GDN_HEADS)))
    wab_hi = wab.astype(BF16)
    wab_lo = (wab - wab_hi.astype(F32)).astype(BF16)
    gp = jnp.zeros((8, LANES), F32).at[0, :GDN_HEADS].set(a_log).at[1, :GDN_HEADS].set(dt_bias)
    return dict(wa=wa, wkv_t=wkv_t, wab_hi=wab_hi, wab_lo=wab_lo, gp=gp, conv_w=conv_w,
                up=[ffn_up[i].astype(BF16) for i in range(2)],
                down=[ffn_down[i].astype(BF16) for i in range(2)],
                wo=w_out.astype(BF16), n_qkv=n_qkv, n_z=hq, n_m=n_m)


def _trunk(x, mods, lw, norm_w, gdn_norm_w, conv0, s0, attend, *, tm, ct, t_valid, final_w):
    sh1, sc1, g1, sh2, sc2, g2, sh3, sc3, g3 = mods
    h = _ffn(x, sh1, sc1, g1, norm_w[0], lw["up"][0], lw["down"][0], tm=tm)
    qkv_pre, z, mq, kt, vt, ab = _win(h, sh2, sc2, norm_w[1], lw["wa"], lw["wkv_t"], lw["wab_hi"],
                                      lw["wab_lo"], tm=tm, n_qkv=lw["n_qkv"], n_z=lw["n_z"], n_m=lw["n_m"])
    go, s_new, conv_new = _gdn(*attend["gdn_in"](qkv_pre, ab, z), conv0, s0, lw["conv_w"], lw["gp"],
                               gdn_norm_w.reshape(1, -1), ct=ct, t_valid=t_valid)
    go = attend["gdn_out"](go)
    mo = attend["moba"](mq, kt, vt)
    h = _ffn(h, sh3, sc3, g3, norm_w[2], lw["up"][1], lw["down"][1], tm=tm, mix=(go, mo, g2, lw["wo"]),
             final_w=final_w)
    return h, s_new, conv_new, kt, vt


def kernel(x_prompt, x_sample, c_prompt, c_sample, cache_k, cache_v, page_table, state_gdn, state_conv,
           w_ada, b_ada, norm_w, ffn_up, ffn_down, w_in, conv_w, a_log, dt_bias, gdn_norm_w, w_out,
           final_norm_w):
    B, T, D = x_prompt.shape
    DB, DS, _ = x_sample.shape
    depth = w_ada.shape[0]
    past = page_table.shape[1] * PAGE_SIZE
    ppb = MOBA_BLOCK // PAGE_SIZE
    assert past % MOBA_BLOCK == 0 and past // MOBA_BLOCK >= MOBA_TOPK and DS <= PAGE_SIZE
    assert T % MOBA_BLOCK == 0 and (T - 1) // MOBA_BLOCK >= MOBA_TOPK
    NS = DB * DS
    hq = GDN_HEADS * GDN_DK
    n_m = MOBA_HEADS * MOBA_HD
    ct_s = LANES

    hp = x_prompt
    hs = x_sample.reshape(1, NS, D)
    outs = [[] for _ in range(8)]
    for l in range(depth):
        lw = _layer_weights(w_in[l], conv_w[l], a_log[l], dt_bias[l], ffn_up[l], ffn_down[l], w_out[l])
        mod = _ada_mod(jnp.concatenate([c_prompt, c_sample], axis=0), w_ada[l], b_ada[l])
        mod_p = mod[:B].reshape(B, N_MOD, 1, D)
        mod_s = jnp.repeat(mod[B:].reshape(DB, N_MOD, D), DS, axis=0).reshape(1, NS, N_MOD, D)
        mods_p = [mod_p[:, i] for i in range(N_MOD)]
        mods_s = [mod_s[:, :, i] for i in range(N_MOD)]
        last = l == depth - 1
        fw = final_norm_w if last else None

        attend_p = dict(gdn_in=lambda qkv, ab, z: (qkv, ab, z), gdn_out=lambda go: go, moba=_moba_prompt)
        hp, s_p, c_p, kt_p, vt_p = _trunk(
            hp, mods_p, lw, norm_w[l], gdn_norm_w[l], jnp.zeros((B, GDN_CONV - 1, 3 * hq), F32),
            jnp.zeros((B, GDN_HEADS, GDN_DK, GDN_DK), F32), attend_p, tm=512, ct=256, t_valid=T, final_w=fw)

        cache_kt = jnp.transpose(cache_k[l], (0, 2, 3, 1))
        cache_vt = jnp.transpose(cache_v[l], (0, 2, 3, 1))
        page_tbl = page_table

        def pad_seq(a):
            return jnp.pad(a.reshape(DB, DS, a.shape[-1]), ((0, 0), (0, ct_s - DS), (0, 0)))

        def moba_s(mq, kt, vt, cache_kt=cache_kt, cache_vt=cache_vt, page_tbl=page_tbl):
            mq_pad = jnp.pad(mq.reshape(DB, DS, n_m), ((0, 0), (0, 16 - DS), (0, 0)))
            idx = _kmean_topk(page_tbl, cache_kt, mq_pad)[:, :, :DS, :MOBA_TOPK]
            lpage = idx[..., None] * ppb + jnp.arange(ppb, dtype=jnp.int32)
            phys = jnp.take_along_axis(page_tbl, lpage.reshape(DB, -1), axis=1)
            qt = _tokens_on_lanes(mq[0].T, DB, DS)
            ot = _decode_attn(phys.reshape(-1), idx.reshape(-1), qt, _tokens_on_lanes(kt[0], DB, DS),
                              _tokens_on_lanes(vt[0], DB, DS), cache_kt, cache_vt, ds=DS, past=past)
            return ot[:, :, :DS].transpose(0, 2, 1).reshape(1, NS, n_m)

        attend_s = dict(gdn_in=lambda qkv, ab, z: (pad_seq(qkv), pad_seq(ab), pad_seq(z)),
                        gdn_out=lambda go: go[:, :DS].reshape(1, NS, hq), moba=moba_s)
        hs, s_s, c_s, kt_s, vt_s = _trunk(
            hs, mods_s, lw, norm_w[l], gdn_norm_w[l], state_conv[l], state_gdn[l], attend_s,
            tm=NS, ct=ct_s, t_valid=DS, final_w=fw)

        def heads_last(a_t, b, t):
            return a_t.reshape(b, MOBA_HEADS, MOBA_HD, t).transpose(0, 3, 1, 2)

        new = [heads_last(kt_p, B, T), heads_last(vt_p, B, T), s_p, c_p,
               heads_last(kt_s.reshape(n_m, DB, DS).transpose(1, 0, 2), DB, DS),
               heads_last(vt_s.reshape(n_m, DB, DS).transpose(1, 0, 2), DB, DS), s_s, c_s]
        for o, v in zip(outs, new):
            o.append(v)
    y_prompt = hp
    y_sample = hs.reshape(DB, DS, D)
    return (y_prompt, y_sample) + tuple(jnp.stack(o) for o in outs)
```

```python
import functools

import jax
import jax.numpy as jnp
from jax import lax
from jax.experimental import pallas as pl
from jax.experimental.pallas import tpu as pltpu

F32 = jnp.float32
BF16 = jnp.bfloat16
EPS = 1e-6
NEG = -1e30

GDN_HEADS = 4
GDN_DK = 128
GDN_CONV = 4
GDN_CHUNK = 64
MOBA_HEADS = 8
MOBA_HD = 64
MOBA_BLOCK = 256
MOBA_TOPK = 3
PAGE_SIZE = 128
N_MOD = 9
LANES = 128
VMEM_LIMIT = 56 * 1024 * 1024


def _sigmoid(x):
    return 1.0 / (1.0 + jnp.exp(-x))


def _silu(x):
    return x * _sigmoid(x)


def _bdot(a, b):
    return jnp.dot(a.astype(BF16), b.astype(BF16), preferred_element_type=F32)


def _bdot_nt(a, b):
    return lax.dot_general(a.astype(BF16), b.astype(BF16), (((1,), (1,)), ((), ())),
                           preferred_element_type=F32)


def _split(x):
    hi = x.astype(BF16)
    lo = (x - hi.astype(F32)).astype(BF16)
    return hi, lo


def _dot3(a, b):
    a_hi, a_lo = _split(a)
    b_hi, b_lo = _split(b)
    d = functools.partial(jnp.dot, preferred_element_type=F32)
    return d(a_hi, b_hi) + (d(a_lo, b_hi) + d(a_hi, b_lo))


def _norm_mod(x, nw, sc, sh):
    y = x * lax.rsqrt(jnp.mean(x * x, axis=-1, keepdims=True) + EPS)
    return (y * nw) * (1.0 + sc) + sh


def _mod_spec(rows, tm, d):
    if rows == 1:
        return pl.BlockSpec((1, 1, d), lambda g, t: (g, 0, 0))
    return pl.BlockSpec((1, tm, d), lambda g, t: (g, t, 0))


def _resident(shape):
    nd = len(shape)
    return pl.BlockSpec(shape, lambda *_: (0,) * nd, pipeline_mode=pl.Buffered(1))


def _ada_kernel(c_ref, w_ref, b_ref, o_ref):
    o_ref[...] = _dot3(_silu(c_ref[...]), w_ref[...]) + b_ref[...]


def _ada_mod(c_all, w, b):
    n, d = c_all.shape
    dout = w.shape[1]
    tn = d
    return pl.pallas_call(
        _ada_kernel,
        grid=(dout // tn,),
        in_specs=[pl.BlockSpec((n, d), lambda i: (0, 0)),
                  pl.BlockSpec((d, tn), lambda i: (0, i)),
                  pl.BlockSpec((1, tn), lambda i: (0, i))],
        out_specs=pl.BlockSpec((n, tn), lambda i: (0, i)),
        out_shape=jax.ShapeDtypeStruct((n, dout), F32),
        compiler_params=pltpu.CompilerParams(dimension_semantics=("arbitrary",),
                                             vmem_limit_bytes=VMEM_LIMIT),
        name="ada_mod",
    )(c_all, w, b.reshape(1, dout))


def _ffn_kernel(*refs, tf, sub, has_mix, has_final):
    it = iter(refs)
    x_ref, sh_ref, sc_ref, g_ref, nw_ref, wup_ref, wdn_ref = [next(it) for _ in range(7)]
    if has_mix:
        go_ref, mo_ref, g2_ref, wo_ref = [next(it) for _ in range(4)]
    if has_final:
        fw_ref = next(it)
    o_ref = next(it)
    a_scr = next(it)

    tm = x_ref.shape[1]
    d_ff = wdn_ref.shape[0]

    def rows(ref, r0):
        return ref[0] if ref.shape[1] == 1 else ref[0, r0:r0 + sub, :]

    for r0 in range(0, tm, sub):
        x = x_ref[0, r0:r0 + sub, :]
        if has_mix:
            half = go_ref.shape[-1]
            mix = (_bdot(go_ref[0, r0:r0 + sub, :], wo_ref[0:half, :])
                   + _bdot(mo_ref[0, r0:r0 + sub, :], wo_ref[half:, :]))
            x = x + rows(g2_ref, r0) * mix
        hb = _norm_mod(x, nw_ref[...], rows(sc_ref, r0), rows(sh_ref, r0)).astype(BF16)
        for c in range(d_ff // tf):
            gate = jnp.dot(hb, wup_ref[:, c * tf:(c + 1) * tf], preferred_element_type=F32)
            up = jnp.dot(hb, wup_ref[:, d_ff + c * tf:d_ff + (c + 1) * tf], preferred_element_type=F32)
            a_scr[r0:r0 + sub, c * tf:(c + 1) * tf] = (_silu(gate) * up).astype(BF16)
        down = jnp.dot(a_scr[r0:r0 + sub, :], wdn_ref[...], preferred_element_type=F32)
        out = x + (0.5 * rows(g_ref, r0)) * down
        if has_final:
            out = out * lax.rsqrt(jnp.mean(out * out, axis=-1, keepdims=True) + EPS) * fw_ref[...]
        o_ref[0, r0:r0 + sub, :] = out


def _ffn(x, sh, sc, g, nw, wup, wdn, *, tm, sub=None, mix=None, final_w=None):
    G, T, D = x.shape
    d_ff = wdn.shape[0]
    tf = d_ff // 2 if (d_ff // 2) % LANES == 0 else d_ff
    row = pl.BlockSpec((1, tm, D), lambda g_, t: (g_, t, 0))
    ins = [x, sh, sc, g, nw.reshape(1, D), wup, wdn]
    specs = [row, _mod_spec(sh.shape[1], tm, D), _mod_spec(sc.shape[1], tm, D),
             _mod_spec(g.shape[1], tm, D), _resident((1, D)), _resident(wup.shape), _resident(wdn.shape)]
    if mix is not None:
        go, mo, g2, wo = mix
        half = go.shape[-1]
        ins += [go, mo, g2, wo]
        specs += [pl.BlockSpec((1, tm, half), lambda g_, t: (g_, t, 0)),
                  pl.BlockSpec((1, tm, half), lambda g_, t: (g_, t, 0)),
                  _mod_spec(g2.shape[1], tm, D), _resident(wo.shape)]
    if final_w is not None:
        ins.append(final_w.reshape(1, D))
        specs.append(_resident((1, D)))
    return pl.pallas_call(
        functools.partial(_ffn_kernel, tf=tf, sub=sub or tm, has_mix=mix is not None,
                          has_final=final_w is not None),
        grid=(G, T // tm),
        in_specs=specs,
        out_specs=row,
        out_shape=jax.ShapeDtypeStruct((G, T, D), F32),
        scratch_shapes=[pltpu.VMEM((tm, d_ff), BF16)],
        compiler_params=pltpu.CompilerParams(dimension_semantics=("parallel", "parallel"),
                                             vmem_limit_bytes=VMEM_LIMIT),
        name="ffn_mix" if mix is not None else "ffn",
    )(*ins)


def _win_kernel(h_ref, sh_ref, sc_ref, nw_ref, wa_ref, wkv_ref, wabh_ref, wabl_ref,
                qkv_ref, z_ref, mq_ref, kt_ref, vt_ref, ab_ref, *, n_qkv, n_z, n_m):
    hn = _norm_mod(h_ref[0], nw_ref[...], sc_ref[0], sh_ref[0])
    hb = hn.astype(BF16)
    h_lo = (hn - hb.astype(F32)).astype(BF16)
    d = functools.partial(jnp.dot, preferred_element_type=F32)
    qkv_ref[0] = d(hb, wa_ref[:, 0:n_qkv])
    z_ref[0] = d(hb, wa_ref[:, n_qkv:n_qkv + n_z])
    mq_ref[0] = d(hb, wa_ref[:, n_qkv + n_z:n_qkv + n_z + n_m]) * (MOBA_HD ** -0.5)
    nt = functools.partial(lax.dot_general, dimension_numbers=(((1,), (1,)), ((), ())),
                           preferred_element_type=F32)
    kt_ref[0] = nt(wkv_ref[0:n_m, :], hb)
    vt_ref[0] = nt(wkv_ref[n_m:, :], hb)
    ab_ref[0] = d(hb, wabh_ref[...]) + (d(h_lo, wabh_ref[...]) + d(hb, wabl_ref[...]))


def _win(h, sh, sc, nw, wa, wkv_t, wab_hi, wab_lo, *, tm, n_qkv, n_z, n_m):
    G, T, D = h.shape
    row = lambda n: pl.BlockSpec((1, tm, n), lambda g_, t: (g_, t, 0))
    col = pl.BlockSpec((1, n_m, tm), lambda g_, t: (g_, 0, t))
    return pl.pallas_call(
        functools.partial(_win_kernel, n_qkv=n_qkv, n_z=n_z, n_m=n_m),
        grid=(G, T // tm),
        in_specs=[row(D), _mod_spec(sh.shape[1], tm, D), _mod_spec(sc.shape[1], tm, D),
                  _resident((1, D)), _resident(wa.shape), _resident(wkv_t.shape),
                  _resident(wab_hi.shape), _resident(wab_lo.shape)],
        out_specs=[row(n_qkv), row(n_z), row(n_m), col, col, row(LANES)],
        out_shape=[jax.ShapeDtypeStruct((G, T, n_qkv), F32), jax.ShapeDtypeStruct((G, T, n_z), F32),
                   jax.ShapeDtypeStruct((G, T, n_m), F32), jax.ShapeDtypeStruct((G, n_m, T), F32),
                   jax.ShapeDtypeStruct((G, n_m, T), F32), jax.ShapeDtypeStruct((G, T, LANES), F32)],
        compiler_params=pltpu.CompilerParams(dimension_semantics=("parallel", "parallel"),
                                             vmem_limit_bytes=VMEM_LIMIT),
        name="w_in",
    )(h, sh, sc, nw.reshape(1, D), wa, wkv_t, wab_hi, wab_lo)


TRI_BASE = 8


def _bmm(a, b):
    return lax.dot_general(a.astype(BF16), b.astype(BF16), (((2,), (1,)), ((0,), (0,))),
                           preferred_element_type=F32)


def _bmm_nt(a, b):
    return lax.dot_general(a.astype(BF16), b.astype(BF16), (((2,), (2,)), ((0,), (0,))),
                           preferred_element_type=F32)


def _bmm_tn(a, b):
    return lax.dot_general(a.astype(BF16), b.astype(BF16), (((1,), (1,)), ((0,), (0,))),
                           preferred_element_type=F32)


def _unit_lower_inverse(m, eye, ri, ci, c):
    size = TRI_BASE
    a = jnp.where(ri // size == ci // size, m, 0.0)
    t = eye - a
    p = a
    e = 2
    while e < size:
        p = _bmm(p, p)
        t = t + _bmm(t, p)
        e *= 2
    while size < c:
        f = jnp.where((ri // size != ci // size) & (ri // (2 * size) == ci // (2 * size)), m, 0.0)
        t = t - _bmm(t, _bmm(f, t))
        size *= 2
    return t


def _gdn_kernel(x_ref, c0_ref, cw_ref, ab_ref, z_ref, s0_ref, gp_ref, nw_ref, go_ref, so_ref, nc_ref,
                xbuf, s_scr, *, ct, c, t_valid, n_t):
    t = pl.program_id(1)
    H, dk = GDN_HEADS, GDN_DK
    halo = GDN_CONV - 1
    base = 8

    @pl.when(t == 0)
    def _():
        xbuf[base - halo:base, :] = c0_ref[0]
        s_scr[...] = s0_ref[0]

    @pl.when(t > 0)
    def _():
        xbuf[base - halo:base, :] = xbuf[ct + base - halo:ct + base, :]

    xbuf[base:base + ct, :] = x_ref[0]
    acc = cw_ref[0:1, :] * xbuf[base - halo:base - halo + ct, :]
    for i in range(1, GDN_CONV):
        acc = acc + cw_ref[i:i + 1, :] * xbuf[base - halo + i:base - halo + i + ct, :]
    qkv = _silu(acc)

    def heads(off):
        return jnp.stack([qkv[:, (off + h) * dk:(off + h + 1) * dk] for h in range(H)])

    q, k, v = heads(0), heads(H), heads(2 * H)
    q = q * lax.rsqrt(jnp.sum(q * q, axis=-1, keepdims=True) + EPS) * (dk ** -0.5)
    k = k * lax.rsqrt(jnp.sum(k * k, axis=-1, keepdims=True) + EPS)

    ab = ab_ref[0]
    rowi = lax.broadcasted_iota(jnp.int32, (ct, LANES), 0)
    xg = ab + gp_ref[1:2, :]
    softplus = jnp.maximum(xg, 0.0) + jnp.log(1.0 + jnp.exp(-jnp.abs(xg)))
    g_all = -jnp.exp(gp_ref[0:1, :]) * softplus
    b_all = _sigmoid(ab)
    if t_valid % ct != 0:
        valid = (rowi + t * ct) < t_valid
        g_all = jnp.where(valid, g_all, 0.0)
        b_all = jnp.where(valid, b_all, 0.0)
    rin = rowi % c
    gc_all = g_all
    s = 1
    while s < c:
        gc_all = gc_all + jnp.where(rin >= s, pltpu.roll(gc_all, s, axis=0), 0.0)
        s *= 2
    gct = jnp.transpose(gc_all)
    gcol = jnp.stack([gc_all[:, h:h + 1] for h in range(H)])
    grow = jnp.stack([gct[h:h + 1, :] for h in range(H)])
    beta = jnp.stack([b_all[:, H + h:H + h + 1] for h in range(H)])

    ri = lax.broadcasted_iota(jnp.int32, (ct, ct), 0)
    ci = lax.broadcasted_iota(jnp.int32, (ct, ct), 1)
    same = ri // c == ci // c
    eye = jnp.where(ri == ci, 1.0, 0.0)
    decay = jnp.exp(jnp.where(same & (ri >= ci), gcol - grow, -jnp.inf))
    kb = k * beta
    m = jnp.where(same & (ri > ci), _bmm_nt(kb, k) * decay, 0.0)
    tm1 = _unit_lower_inverse(m, eye, ri, ci, c) - eye
    egc = jnp.exp(gcol)
    rhs = jnp.concatenate([v * beta, kb * egc], axis=-1)
    uw = rhs + _bmm(tm1, rhs)
    u, w = uw[..., :dk], uw[..., dk:]
    attn = _bmm_nt(q, k) * decay
    qd = q * egc

    S = s_scr[...]
    o_state, v_new = [], []
    for cc in range(ct // c):
        r0 = cc * c
        if n_t == 1 and r0 >= t_valid:
            o_state.append(jnp.zeros((H, c, dk), F32))
            v_new.append(jnp.zeros((H, c, dk), F32))
            continue
        gl = gcol[:, r0 + c - 1:r0 + c, :]
        kd = k[:, r0:r0 + c] * jnp.exp(gl - gcol[:, r0:r0 + c])
        xs = _bmm(jnp.concatenate([w[:, r0:r0 + c], qd[:, r0:r0 + c]], axis=1), S)
        vn = u[:, r0:r0 + c] - xs[:, :c]
        o_state.append(xs[:, c:])
        v_new.append(vn)
        S = S * jnp.exp(gl) + _bmm_tn(kd, vn)
    s_scr[...] = S
    o = jnp.concatenate(o_state, axis=1) + _bmm(attn, jnp.concatenate(v_new, axis=1))
    o = o * lax.rsqrt(jnp.mean(o * o, axis=-1, keepdims=True) + EPS) * nw_ref[...]
    z = z_ref[0]
    for h in range(H):
        go_ref[0, :, h * dk:(h + 1) * dk] = o[h] * _silu(z[:, h * dk:(h + 1) * dk])

    @pl.when(t == n_t - 1)
    def _():
        so_ref[0] = S
        e0 = base - halo + (t_valid - (n_t - 1) * ct)
        nc_ref[0] = xbuf[e0:e0 + halo, :]


def _gdn(qkv_pre, ab, z, conv0, s0, conv_w, gp, gnw, *, ct, t_valid):
    G, T, W3 = qkv_pre.shape
    H, dk = GDN_HEADS, GDN_DK
    n_t = T // ct
    halo = GDN_CONV - 1
    tile = lambda n: pl.BlockSpec((1, ct, n), lambda b, t: (b, t, 0))
    per_seq = lambda shape: pl.BlockSpec((1,) + shape, lambda b, t: (b,) + (0,) * len(shape))
    whole = lambda shape: pl.BlockSpec(shape, lambda b, t: (0,) * len(shape))
    return pl.pallas_call(
        functools.partial(_gdn_kernel, ct=ct, c=GDN_CHUNK, t_valid=t_valid, n_t=n_t),
        grid=(G, n_t),
        in_specs=[tile(W3), per_seq((halo, W3)), whole((GDN_CONV, W3)), tile(LANES), tile(H * dk),
                  per_seq((H, dk, dk)), whole((8, LANES)), whole((1, dk))],
        out_specs=[tile(H * dk), per_seq((H, dk, dk)), per_seq((halo, W3))],
        out_shape=[jax.ShapeDtypeStruct((G, T, H * dk), F32), jax.ShapeDtypeStruct((G, H, dk, dk), F32),
                   jax.ShapeDtypeStruct((G, halo, W3), F32)],
        scratch_shapes=[pltpu.VMEM((ct + 8, W3), F32), pltpu.VMEM((H, dk, dk), F32)],
        compiler_params=pltpu.CompilerParams(dimension_semantics=("parallel", "arbitrary"),
                                             vmem_limit_bytes=VMEM_LIMIT),
        name="gdn",
    )(qkv_pre, conv0, conv_w, ab, z, s0, gp, gnw)


def _alibi_slope(head_f):
    return jnp.exp2(-8.0 * (head_f + 1.0) / MOBA_HEADS)


def _moba_prompt_kernel(q_ref, kt_ref, vt_ref, o_ref, ka_ref, kb_ref, vb_ref, qa_ref, *, nb, blk):
    p = pl.program_id(1)
    hd = MOBA_HD
    T = nb * blk
    nr = 16
    nt = functools.partial(lax.dot_general, dimension_numbers=(((1,), (1,)), ((), ())),
                           preferred_element_type=F32)

    kt = kt_ref[0]
    rowk = lax.broadcasted_iota(jnp.int32, (2 * hd, T), 0)
    blk_of_col = lax.broadcasted_iota(jnp.int32, (2 * hd, T), 1) // blk
    ka_ref[...] = jnp.where(rowk < hd, kt, jnp.where(rowk - hd == blk_of_col, 1.0, 0.0)).astype(BF16)
    kb_ref[...] = jnp.where(rowk >= hd, kt, jnp.where(rowk == blk_of_col, 1.0, 0.0)).astype(BF16)
    vb_ref[...] = vt_ref[0].astype(BF16)
    rown = lax.broadcasted_iota(jnp.int32, (nr, T), 0)
    coln = lax.broadcasted_iota(jnp.int32, (nr, T), 1) // blk
    avg = jnp.where(rown == coln, 1.0 / blk, 0.0).astype(BF16)
    k1 = kt.astype(BF16)
    r1 = kt - k1.astype(F32)
    k2 = r1.astype(BF16)
    k3 = (r1 - k2.astype(F32)).astype(BF16)
    kmt = nt(avg, k1) + (nt(avg, k2) + nt(avg, k3))
    kmt_hi, kmt_lo = _split(kmt)

    q = q_ref[0]
    lane = lax.broadcasted_iota(jnp.int32, (T, LANES), 1)
    pf = jnp.full((1, 1), p, jnp.int32).astype(F32)
    e_row = lax.broadcasted_iota(jnp.int32, (nr, LANES), 0)
    e_lane = lax.broadcasted_iota(jnp.int32, (nr, LANES), 1)
    slope = []
    for hh in range(2):
        in_head = (lane >= hh * hd) & (lane < (hh + 1) * hd)
        qx = jnp.where(in_head, q, 0.0)
        slope.append(_alibi_slope(2.0 * pf + hh))
        q_hi, q_lo = _split(qx)
        gt = nt(kmt_hi, q_hi) + (nt(kmt_lo, q_hi) + nt(kmt_hi, q_lo))
        gt = jnp.where(rown < coln, gt, -jnp.inf)
        sel = jnp.where(rown == coln, 1.0, 0.0)
        for n in range(nb):
            gn = gt[n:n + 1, :]
            ahead = jnp.where((gt > gn) | ((gt == gn) & (rown < n)), 1.0, 0.0)
            cnt = jnp.sum(ahead, axis=0, keepdims=True)
            sel = jnp.where((rown == n) & (cnt < MOBA_TOPK) & (gn > -jnp.inf), 1.0, sel)
        off = hd if hh == 0 else 0
        place = jnp.where((e_lane == e_row + off) & (e_row < nb), 1.0, 0.0).astype(BF16)
        placed = lax.dot_general(sel.astype(BF16), place, (((0,), (0,)), ((), ())),
                                 preferred_element_type=F32)
        qa_ref[hh] = jnp.where(in_head, q, jnp.where(placed > 0.5, 0.0, NEG)).astype(BF16)

    colf = lax.broadcasted_iota(jnp.int32, (1, T), 1).astype(F32)
    lane_blk = lax.broadcasted_iota(jnp.int32, (blk, LANES), 1)
    ri = lax.broadcasted_iota(jnp.int32, (blk, blk), 0)
    ci = lax.broadcasted_iota(jnp.int32, (blk, blk), 1)
    k_refs = (ka_ref, kb_ref)
    for j in range(nb):
        w = (j + 1) * blk
        outs = []
        for hh in range(2):
            s = jnp.dot(qa_ref[hh, j * blk:(j + 1) * blk, :], k_refs[hh][:, :w], preferred_element_type=F32)
            s = s + slope[hh] * (colf[:, :w] - j * blk)
            own = jnp.where(ri >= ci, s[:, j * blk:], NEG)
            s = own if j == 0 else jnp.concatenate([s[:, :j * blk], own], axis=1)
            pr = jnp.exp(s - jnp.max(s, axis=1, keepdims=True))
            den = jnp.sum(pr, axis=1, keepdims=True)
            outs.append(nt(pr.astype(BF16), vb_ref[:, :w]) / den)
        o_ref[0, j * blk:(j + 1) * blk, :] = jnp.where(lane_blk < hd, outs[0], outs[1])


def _moba_prompt(mq, kt, vt):
    B, T, W = mq.shape
    blk = MOBA_BLOCK
    nb = T // blk
    assert nb <= 16
    qspec = pl.BlockSpec((1, T, LANES), lambda b, p: (b, 0, p))
    kspec = pl.BlockSpec((1, LANES, T), lambda b, p: (b, p, 0))
    return pl.pallas_call(
        functools.partial(_moba_prompt_kernel, nb=nb, blk=blk),
        grid=(B, W // LANES),
        in_specs=[qspec, kspec, kspec],
        out_specs=qspec,
        out_shape=jax.ShapeDtypeStruct((B, T, W), F32),
        scratch_shapes=[pltpu.VMEM((LANES, T), BF16)] * 3 + [pltpu.VMEM((2, T, LANES), BF16)],
        compiler_params=pltpu.CompilerParams(dimension_semantics=("parallel", "parallel"),
                                             vmem_limit_bytes=VMEM_LIMIT),
        name="moba_prompt",
    )(mq, kt, vt)


def _kmean_topk_kernel(pt_ref, *refs, n_pg, n_steps, ppb, nblk):
    pages = refs[:n_pg]
    q_ref, idx_ref, km_ref = refs[n_pg:]
    i = pl.program_id(1)
    W = MOBA_HEADS * MOBA_HD
    lane1 = lax.broadcasted_iota(jnp.int32, (1, LANES), 1)

    @pl.when(i == 0)
    def _():
        km_ref[...] = jnp.zeros((W, LANES), F32)

    km = km_ref[...]
    for bb in range(n_pg // ppb):
        tot = pages[bb * ppb][...]
        for r in range(1, ppb):
            tot = tot + pages[bb * ppb + r][...]
        col = jnp.sum(tot.reshape(W, PAGE_SIZE), axis=1, keepdims=True) * (1.0 / (ppb * PAGE_SIZE))
        km = jnp.where(lane1 == i * (n_pg // ppb) + bb, col, km)
    km_ref[...] = km

    @pl.when(i == n_steps - 1)
    def _():
        rows = q_ref.shape[1]
        lanef = lax.broadcasted_iota(jnp.int32, (rows, LANES), 1).astype(F32)
        for h in range(MOBA_HEADS):
            qh = q_ref[0, :, h * MOBA_HD:(h + 1) * MOBA_HD]
            gate = _dot3(qh, km[h * MOBA_HD:(h + 1) * MOBA_HD, :])
            gate = jnp.where(lanef < nblk, gate, -jnp.inf)
            out = jnp.zeros((rows, LANES), F32)
            for r in range(MOBA_TOPK):
                mx = jnp.max(gate, axis=1, keepdims=True)
                idx = jnp.min(jnp.where(gate == mx, lanef, 1e9), axis=1, keepdims=True)
                out = jnp.where(lanef == r, idx, out)
                gate = jnp.where(lanef == idx, -jnp.inf, gate)
            idx_ref[0, h] = out.astype(jnp.int32)


def _kmean_topk(page_table, cache_kt, mq_pad, *, n_pg=8):
    DB, n_pages = page_table.shape
    ppb = MOBA_BLOCK // PAGE_SIZE
    nblk = n_pages // ppb
    n_steps = n_pages // n_pg
    rows = mq_pad.shape[1]
    W = MOBA_HEADS * MOBA_HD

    def page_spec(r):
        return pl.BlockSpec((None, MOBA_HEADS, MOBA_HD, PAGE_SIZE),
                            lambda b, i, pt: (pt[b * n_pages + i * n_pg + r], 0, 0, 0))

    grid_spec = pltpu.PrefetchScalarGridSpec(
        num_scalar_prefetch=1,
        grid=(DB, n_steps),
        in_specs=[page_spec(r) for r in range(n_pg)] + [pl.BlockSpec((1, rows, W), lambda b, i, pt: (b, 0, 0))],
        out_specs=pl.BlockSpec((1, MOBA_HEADS, rows, LANES), lambda b, i, pt: (b, 0, 0, 0)),
        scratch_shapes=[pltpu.VMEM((W, LANES), F32)],
    )
    return pl.pallas_call(
        functools.partial(_kmean_topk_kernel, n_pg=n_pg, n_steps=n_steps, ppb=ppb, nblk=nblk),
        grid_spec=grid_spec,
        out_shape=jax.ShapeDtypeStruct((DB, MOBA_HEADS, rows, LANES), jnp.int32),
        compiler_params=pltpu.CompilerParams(dimension_semantics=("parallel", "arbitrary"),
                                             vmem_limit_bytes=VMEM_LIMIT),
        name="kmean_topk",
    )(page_table.reshape(-1), *([cache_kt] * n_pg), mq_pad)


def _decode_attn_kernel(phys_ref, idx_ref, qt_ref, kown_ref, vown_ref, ck_hbm, cv_hbm, o_ref,
                        kbuf, vbuf, sem, *, ds, past, n_slab):
    b = pl.program_id(0)
    nb = pl.num_programs(0)
    hd = MOBA_HD
    ppb = MOBA_BLOCK // PAGE_SIZE
    per_head = n_slab // MOBA_HEADS

    def copies(bi, slot, u):
        pg = phys_ref[bi * n_slab + u]
        hh = u // per_head
        return (pltpu.make_async_copy(ck_hbm.at[pg, hh], kbuf.at[slot, u], sem.at[slot, 0]),
                pltpu.make_async_copy(cv_hbm.at[pg, hh], vbuf.at[slot, u], sem.at[slot, 1]))

    def start_all(bi, slot):
        def body(u, carry):
            ck, cv = copies(bi, slot, u)
            ck.start()
            cv.start()
            return carry
        lax.fori_loop(0, n_slab, body, 0)

    def wait_all(bi, slot):
        def body(u, carry):
            ck, cv = copies(bi, slot, u)
            ck.wait()
            cv.wait()
            return carry
        lax.fori_loop(0, n_slab, body, 0)

    slot = b % 2

    @pl.when(b == 0)
    def _():
        start_all(b, slot)

    @pl.when(b + 1 < nb)
    def _():
        start_all(b + 1, 1 - slot)

    wait_all(b, slot)

    lane = lax.broadcasted_iota(jnp.int32, (1, LANES), 1)
    lanef = lane.astype(F32)

    def head_body(h, carry):
        r0 = pl.multiple_of(h * hd, hd)
        slope = _alibi_slope(jnp.full((1, 1), h, jnp.int32).astype(F32))
        kown = kown_ref[0, pl.ds(r0, hd), :]
        vown = vown_ref[0, pl.ds(r0, hd), :]
        qt = qt_ref[0, pl.ds(r0, hd), :]
        otile = jnp.zeros((hd, LANES), F32)
        for t in range(ds):
            qb = jnp.broadcast_to(qt[:, t:t + 1], (hd, LANES))
            s_own = jnp.sum(qb * kown, axis=0, keepdims=True)
            s_own = jnp.where(lane <= t, s_own - slope * (t - lanef), NEG)
            scores = [s_own]
            for s in range(MOBA_TOPK):
                blk_id = idx_ref[(b * MOBA_HEADS + h) * (ds * MOBA_TOPK) + t * MOBA_TOPK + s]
                for r in range(ppb):
                    u = ((h * ds + t) * MOBA_TOPK + s) * ppb + r
                    sc = jnp.sum(qb * kbuf[slot, u], axis=0, keepdims=True)
                    start = (blk_id * MOBA_BLOCK + r * PAGE_SIZE).astype(F32)
                    scores.append(sc - slope * ((past + t) - start - lanef))
            mx = scores[0]
            for sc in scores[1:]:
                mx = jnp.maximum(mx, sc)
            mx = jnp.max(mx, axis=1, keepdims=True)
            pr = jnp.exp(scores[0] - mx)
            den = pr
            acc = pr * vown
            i = 1
            for s in range(MOBA_TOPK):
                for r in range(ppb):
                    u = ((h * ds + t) * MOBA_TOPK + s) * ppb + r
                    pr = jnp.exp(scores[i] - mx)
                    den = den + pr
                    acc = acc + pr * vbuf[slot, u]
                    i += 1
            o_col = jnp.sum(acc, axis=1, keepdims=True) / jnp.sum(den, axis=1, keepdims=True)
            otile = jnp.where(lane == t, o_col, otile)
        o_ref[0, pl.ds(r0, hd), :] = otile
        return carry

    lax.fori_loop(0, MOBA_HEADS, head_body, 0)


def _decode_attn(phys, idx, qt, kown_t, vown_t, cache_kt, cache_vt, *, ds, past):
    DB, W, _ = qt.shape
    n_slab = MOBA_HEADS * ds * MOBA_TOPK * (MOBA_BLOCK // PAGE_SIZE)
    tile = pl.BlockSpec((1, W, LANES), lambda b, ph, ix: (b, 0, 0))
    grid_spec = pltpu.PrefetchScalarGridSpec(
        num_scalar_prefetch=2,
        grid=(DB,),
        in_specs=[tile, tile, tile, pl.BlockSpec(memory_space=pl.ANY), pl.BlockSpec(memory_space=pl.ANY)],
        out_specs=tile,
        scratch_shapes=[pltpu.VMEM((2, n_slab, MOBA_HD, PAGE_SIZE), F32),
                        pltpu.VMEM((2, n_slab, MOBA_HD, PAGE_SIZE), F32),
                        pltpu.SemaphoreType.DMA((2, 2))],
    )
    return pl.pallas_call(
        functools.partial(_decode_attn_kernel, ds=ds, past=past, n_slab=n_slab),
        grid_spec=grid_spec,
        out_shape=jax.ShapeDtypeStruct((DB, W, LANES), F32),
        compiler_params=pltpu.CompilerParams(dimension_semantics=("arbitrary",),
                                             vmem_limit_bytes=VMEM_LIMIT),
        name="decode_attn",
    )(phys, idx, qt, kown_t, vown_t, cache_kt, cache_vt)


def _tokens_on_lanes(a, db, ds):
    w = a.shape[0]
    a = a.reshape(w, db, ds).transpose(1, 0, 2)
    return jnp.pad(a, ((0, 0), (0, 0), (0, LANES - ds)))


def _layer_weights(w_in, conv_w, a_log, dt_bias, ffn_up, ffn_down, w_out):
    hq = GDN_HEADS * GDN_DK
    n_qkv = 3 * hq
    o_a = n_qkv
    o_z = o_a + 2 * GDN_HEADS
    o_m = o_z + hq
    n_m = MOBA_HEADS * MOBA_HD
    wa = jnp.concatenate([w_in[:, :n_qkv], w_in[:, o_z:o_m + n_m]], axis=1).astype(BF16)
    wkv_t = w_in[:, o_m + n_m:].T.astype(BF16)
    wab = jnp.pad(w_in[:, o_a:o_z], ((0, 0), (0, LANES - 2 * GDN_HEADS)))
    wab_hi = wab.astype(BF16)
    wab_lo = (wab - wab_hi.astype(F32)).astype(BF16)
    gp = jnp.zeros((8, LANES), F32).at[0, :GDN_HEADS].set(a_log).at[1, :GDN_HEADS].set(dt_bias)
    return dict(wa=wa, wkv_t=wkv_t, wab_hi=wab_hi, wab_lo=wab_lo, gp=gp, conv_w=conv_w,
                up=[ffn_up[i].astype(BF16) for i in range(2)],
                down=[ffn_down[i].astype(BF16) for i in range(2)],
                wo=w_out.astype(BF16), n_qkv=n_qkv, n_z=hq, n_m=n_m)


def _trunk(x, mods, lw, norm_w, gdn_norm_w, conv0, s0, attend, *, tm, ct, t_valid, final_w):
    sh1, sc1, g1, sh2, sc2, g2, sh3, sc3, g3 = mods
    h = _ffn(x, sh1, sc1, g1, norm_w[0], lw["up"][0], lw["down"][0], tm=tm)
    qkv_pre, z, mq, kt, vt, ab = _win(h, sh2, sc2, norm_w[1], lw["wa"], lw["wkv_t"], lw["wab_hi"],
                                      lw["wab_lo"], tm=tm, n_qkv=lw["n_qkv"], n_z=lw["n_z"], n_m=lw["n_m"])
    go, s_new, conv_new = _gdn(*attend["gdn_in"](qkv_pre, ab, z), conv0, s0, lw["conv_w"], lw["gp"],
                               gdn_norm_w.reshape(1, -1), ct=ct, t_valid=t_valid)
    go = attend["gdn_out"](go)
    mo = attend["moba"](mq, kt, vt)
    h = _ffn(h, sh3, sc3, g3, norm_w[2], lw["up"][1], lw["down"][1], tm=tm, mix=(go, mo, g2, lw["wo"]),
             final_w=final_w)
    return h, s_new, conv_new, kt, vt


def kernel(x_prompt, x_sample, c_prompt, c_sample, cache_k, cache_v, page_table, state_gdn, state_conv,
           w_ada, b_ada, norm_w, ffn_up, ffn_down, w_in, conv_w, a_log, dt_bias, gdn_norm_w, w_out,
           final_norm_w):
    B, T, D = x_prompt.shape
    DB, DS, _ = x_sample.shape
    depth = w_ada.shape[0]
    past = page_table.shape[1] * PAGE_SIZE
    ppb = MOBA_BLOCK // PAGE_SIZE
    assert past % MOBA_BLOCK == 0 and past // MOBA_BLOCK >= MOBA_TOPK and DS <= PAGE_SIZE
    assert T % MOBA_BLOCK == 0 and (T - 1) // MOBA_BLOCK >= MOBA_TOPK
    NS = DB * DS
    hq = GDN_HEADS * GDN_DK
    n_m = MOBA_HEADS * MOBA_HD
    ct_s = LANES

    hp = x_prompt
    hs = x_sample.reshape(1, NS, D)
    outs = [[] for _ in range(8)]
    for l in range(depth):
        lw = _layer_weights(w_in[l], conv_w[l], a_log[l], dt_bias[l], ffn_up[l], ffn_down[l], w_out[l])
        mod = _ada_mod(jnp.concatenate([c_prompt, c_sample], axis=0), w_ada[l], b_ada[l])
        mod_p = mod[:B].reshape(B, N_MOD, 1, D)
        mod_s = jnp.repeat(mod[B:].reshape(DB, N_MOD, D), DS, axis=0).reshape(1, NS, N_MOD, D)
        mods_p = [mod_p[:, i] for i in range(N_MOD)]
        mods_s = [mod_s[:, :, i] for i in range(N_MOD)]
        last = l == depth - 1
        fw = final_norm_w if last else None

        attend_p = dict(gdn_in=lambda qkv, ab, z: (qkv, ab, z), gdn_out=lambda go: go, moba=_moba_prompt)
        hp, s_p, c_p, kt_p, vt_p = _trunk(
            hp, mods_p, lw, norm_w[l], gdn_norm_w[l], jnp.zeros((B, GDN_CONV - 1, 3 * hq), F32),
            jnp.zeros((B, GDN_HEADS, GDN_DK, GDN_DK), F32), attend_p, tm=512, ct=256, t_valid=T, final_w=fw)

        cache_kt = jnp.transpose(cache_k[l], (0, 2, 3, 1))
        cache_vt = jnp.transpose(cache_v[l], (0, 2, 3, 1))
        page_tbl = page_table

        def pad_seq(a):
            return jnp.pad(a.reshape(DB, DS, a.shape[-1]), ((0, 0), (0, ct_s - DS), (0, 0)))

        def moba_s(mq, kt, vt, cache_kt=cache_kt, cache_vt=cache_vt, page_tbl=page_tbl):
            mq_pad = jnp.pad(mq.reshape(DB, DS, n_m), ((0, 0), (0, 16 - DS), (0, 0)))
            idx = _kmean_topk(page_tbl, cache_kt, mq_pad)[:, :, :DS, :MOBA_TOPK]
            lpage = idx[..., None] * ppb + jnp.arange(ppb, dtype=jnp.int32)
            phys = jnp.take_along_axis(page_tbl, lpage.reshape(DB, -1), axis=1)
            qt = _tokens_on_lanes(mq[0].T, DB, DS)
            ot = _decode_attn(phys.reshape(-1), idx.reshape(-1), qt, _tokens_on_lanes(kt[0], DB, DS),
                              _tokens_on_lanes(vt[0], DB, DS), cache_kt, cache_vt, ds=DS, past=past)
            return ot[:, :, :DS].transpose(0, 2, 1).reshape(1, NS, n_m)

        attend_s = dict(gdn_in=lambda qkv, ab, z: (pad_seq(qkv), pad_seq(ab), pad_seq(z)),
                        gdn_out=lambda go: go[:, :DS].reshape(1, NS, hq), moba=moba_s)
        hs, s_s, c_s, kt_s, vt_s = _trunk(
            hs, mods_s, lw, norm_w[l], gdn_norm_w[l], state_conv[l], state_gdn[l], attend_s,
            tm=NS, ct=ct_s, t_valid=DS, final_w=fw)

        def heads_last(a_t, b, t):
            return a_t.reshape(b, MOBA_HEADS, MOBA_HD, t).transpose(0, 3, 1, 2)

        new = [heads_last(kt_p, B, T), heads_last(vt_p, B, T), s_p, c_p,
               heads_last(kt_s.reshape(n_m, DB, DS).transpose(1, 0, 2), DB, DS),
               heads_last(vt_s.reshape(n_m, DB, DS).transpose(1, 0, 2), DB, DS), s_s, c_s]
        for o, v in zip(outs, new):
            o.append(v)
    y_prompt = hp
    y_sample = hs.reshape(DB, DS, D)
    return (y_prompt, y_sample) + tuple(jnp.stack(o) for o in outs)
```

```python
import functools

import jax
import jax.numpy as jnp
from jax import lax
from jax.experimental import pallas as pl
from jax.experimental.pallas import tpu as pltpu

F32 = jnp.float32
BF16 = jnp.bfloat16
EPS = 1e-6
NEG = -1e30
LOG2E = 1.4426950408889634

GDN_HEADS = 4
GDN_DK = 128
GDN_CONV = 4
GDN_CHUNK = 64
MOBA_HEADS = 8
MOBA_HD = 64
MOBA_BLOCK = 256
MOBA_TOPK = 3
PAGE_SIZE = 128
N_MOD = 9
LANES = 128
VMEM_LIMIT = 56 * 1024 * 1024


def _sigmoid(x):
    return 1.0 / (1.0 + jnp.exp(-x))


def _silu(x):
    return x * _sigmoid(x)


def _bdot(a, b):
    return jnp.dot(a.astype(BF16), b.astype(BF16), preferred_element_type=F32)


def _bdot_nt(a, b):
    return lax.dot_general(a.astype(BF16), b.astype(BF16), (((1,), (1,)), ((), ())),
                           preferred_element_type=F32)


def _split(x):
    hi = x.astype(BF16)
    lo = (x - hi.astype(F32)).astype(BF16)
    return hi, lo


def _dot3(a, b):
    a_hi, a_lo = _split(a)
    b_hi, b_lo = _split(b)
    d = functools.partial(jnp.dot, preferred_element_type=F32)
    return d(a_hi, b_hi) + (d(a_lo, b_hi) + d(a_hi, b_lo))


def _norm_mod(x, nw, sc, sh):
    y = x * lax.rsqrt(jnp.mean(x * x, axis=-1, keepdims=True) + EPS)
    return (y * nw) * (1.0 + sc) + sh


def _mod_spec(rows, tm, d):
    if rows == 1:
        return pl.BlockSpec((1, 1, d), lambda g, t: (g, 0, 0))
    return pl.BlockSpec((1, tm, d), lambda g, t: (g, t, 0))


def _resident(shape):
    nd = len(shape)
    return pl.BlockSpec(shape, lambda *_: (0,) * nd, pipeline_mode=pl.Buffered(1))


def _ada_kernel(c_ref, w_ref, b_ref, o_ref):
    o_ref[...] = _dot3(_silu(c_ref[...]), w_ref[...]) + b_ref[...]


def _ada_mod(c_all, w, b):
    n, d = c_all.shape
    dout = w.shape[1]
    tn = d
    return pl.pallas_call(
        _ada_kernel,
        grid=(dout // tn,),
        in_specs=[pl.BlockSpec((n, d), lambda i: (0, 0)),
                  pl.BlockSpec((d, tn), lambda i: (0, i)),
                  pl.BlockSpec((1, tn), lambda i: (0, i))],
        out_specs=pl.BlockSpec((n, tn), lambda i: (0, i)),
        out_shape=jax.ShapeDtypeStruct((n, dout), F32),
        compiler_params=pltpu.CompilerParams(dimension_semantics=("arbitrary",),
                                             vmem_limit_bytes=VMEM_LIMIT),
        name="ada_mod",
    )(c_all, w, b.reshape(1, dout))


def _ffn_kernel(*refs, tf, sub, has_mix, has_final):
    it = iter(refs)
    x_ref, sh_ref, sc_ref, g_ref, nw_ref, wup_ref, wdn_ref = [next(it) for _ in range(7)]
    if has_mix:
        go_ref, mo_ref, g2_ref, wo_ref = [next(it) for _ in range(4)]
    if has_final:
        fw_ref = next(it)
    o_ref = next(it)
    a_scr = next(it)

    tm = x_ref.shape[1]
    d_ff = wdn_ref.shape[0]

    def rows(ref, r0):
        return ref[0] if ref.shape[1] == 1 else ref[0, r0:r0 + sub, :]

    for r0 in range(0, tm, sub):
        x = x_ref[0, r0:r0 + sub, :]
        if has_mix:
            half = go_ref.shape[-1]
            mix = (_bdot(go_ref[0, r0:r0 + sub, :], wo_ref[0:half, :])
                   + _bdot(mo_ref[0, r0:r0 + sub, :], wo_ref[half:, :]))
            x = x + rows(g2_ref, r0) * mix
        hb = _norm_mod(x, nw_ref[...], rows(sc_ref, r0), rows(sh_ref, r0)).astype(BF16)
        for c in range(d_ff // tf):
            gate = jnp.dot(hb, wup_ref[:, c * tf:(c + 1) * tf], preferred_element_type=F32)
            up = jnp.dot(hb, wup_ref[:, d_ff + c * tf:d_ff + (c + 1) * tf], preferred_element_type=F32)
            a_scr[r0:r0 + sub, c * tf:(c + 1) * tf] = (_silu(gate) * up).astype(BF16)
        down = jnp.dot(a_scr[r0:r0 + sub, :], wdn_ref[...], preferred_element_type=F32)
        out = x + (0.5 * rows(g_ref, r0)) * down
        if has_final:
            out = out * lax.rsqrt(jnp.mean(out * out, axis=-1, keepdims=True) + EPS) * fw_ref[...]
        o_ref[0, r0:r0 + sub, :] = out


def _ffn(x, sh, sc, g, nw, wup, wdn, *, tm, sub=None, mix=None, final_w=None):
    G, T, D = x.shape
    d_ff = wdn.shape[0]
    tf = d_ff // 2 if (d_ff // 2) % LANES == 0 else d_ff
    row = pl.BlockSpec((1, tm, D), lambda g_, t: (g_, t, 0))
    ins = [x, sh, sc, g, nw.reshape(1, D), wup, wdn]
    specs = [row, _mod_spec(sh.shape[1], tm, D), _mod_spec(sc.shape[1], tm, D),
             _mod_spec(g.shape[1], tm, D), _resident((1, D)), _resident(wup.shape), _resident(wdn.shape)]
    if mix is not None:
        go, mo, g2, wo = mix
        half = go.shape[-1]
        ins += [go, mo, g2, wo]
        specs += [pl.BlockSpec((1, tm, half), lambda g_, t: (g_, t, 0)),
                  pl.BlockSpec((1, tm, half), lambda g_, t: (g_, t, 0)),
                  _mod_spec(g2.shape[1], tm, D), _resident(wo.shape)]
    if final_w is not None:
        ins.append(final_w.reshape(1, D))
        specs.append(_resident((1, D)))
    return pl.pallas_call(
        functools.partial(_ffn_kernel, tf=tf, sub=sub or tm, has_mix=mix is not None,
                          has_final=final_w is not None),
        grid=(G, T // tm),
        in_specs=specs,
        out_specs=row,
        out_shape=jax.ShapeDtypeStruct((G, T, D), F32),
        scratch_shapes=[pltpu.VMEM((tm, d_ff), BF16)],
        compiler_params=pltpu.CompilerParams(dimension_semantics=("parallel", "parallel"),
                                             vmem_limit_bytes=VMEM_LIMIT),
        name="ffn_mix" if mix is not None else "ffn",
    )(*ins)


def _win_kernel(h_ref, sh_ref, sc_ref, nw_ref, wa_ref, wkv_ref, wabh_ref, wabl_ref,
                qkv_ref, z_ref, mq_ref, kt_ref, vt_ref, ab_ref, *, n_qkv, n_z, n_m):
    hn = _norm_mod(h_ref[0], nw_ref[...], sc_ref[0], sh_ref[0])
    hb = hn.astype(BF16)
    h_lo = (hn - hb.astype(F32)).astype(BF16)
    d = functools.partial(jnp.dot, preferred_element_type=F32)
    qkv_ref[0] = d(hb, wa_ref[:, 0:n_qkv])
    z_ref[0] = d(hb, wa_ref[:, n_qkv:n_qkv + n_z])
    mq_ref[0] = d(hb, wa_ref[:, n_qkv + n_z:n_qkv + n_z + n_m]) * (MOBA_HD ** -0.5)
    nt = functools.partial(lax.dot_general, dimension_numbers=(((1,), (1,)), ((), ())),
                           preferred_element_type=F32)
    kt_ref[0] = nt(wkv_ref[0:n_m, :], hb)
    vt_ref[0] = nt(wkv_ref[n_m:, :], hb)
    ab_ref[0] = d(hb, wabh_ref[...]) + (d(h_lo, wabh_ref[...]) + d(hb, wabl_ref[...]))


def _win(h, sh, sc, nw, wa, wkv_t, wab_hi, wab_lo, *, tm, n_qkv, n_z, n_m):
    G, T, D = h.shape
    row = lambda n: pl.BlockSpec((1, tm, n), lambda g_, t: (g_, t, 0))
    col = pl.BlockSpec((1, n_m, tm), lambda g_, t: (g_, 0, t))
    return pl.pallas_call(
        functools.partial(_win_kernel, n_qkv=n_qkv, n_z=n_z, n_m=n_m),
        grid=(G, T // tm),
        in_specs=[row(D), _mod_spec(sh.shape[1], tm, D), _mod_spec(sc.shape[1], tm, D),
                  _resident((1, D)), _resident(wa.shape), _resident(wkv_t.shape),
                  _resident(wab_hi.shape), _resident(wab_lo.shape)],
        out_specs=[row(n_qkv), row(n_z), row(n_m), col, col, row(LANES)],
        out_shape=[jax.ShapeDtypeStruct((G, T, n_qkv), F32), jax.ShapeDtypeStruct((G, T, n_z), F32),
                   jax.ShapeDtypeStruct((G, T, n_m), F32), jax.ShapeDtypeStruct((G, n_m, T), F32),
                   jax.ShapeDtypeStruct((G, n_m, T), F32), jax.ShapeDtypeStruct((G, T, LANES), F32)],
        compiler_params=pltpu.CompilerParams(dimension_semantics=("parallel", "parallel"),
                                             vmem_limit_bytes=VMEM_LIMIT),
        name="w_in",
    )(h, sh, sc, nw.reshape(1, D), wa, wkv_t, wab_hi, wab_lo)


TRI_BASE = 8


def _bmm(a, b):
    return lax.dot_general(a.astype(BF16), b.astype(BF16), (((2,), (1,)), ((0,), (0,))),
                           preferred_element_type=F32)


def _bmm_nt(a, b):
    return lax.dot_general(a.astype(BF16), b.astype(BF16), (((2,), (2,)), ((0,), (0,))),
                           preferred_element_type=F32)


def _bmm_tn(a, b):
    return lax.dot_general(a.astype(BF16), b.astype(BF16), (((1,), (1,)), ((0,), (0,))),
                           preferred_element_type=F32)


def _unit_lower_inverse(m, eye, ri, ci, c):
    size = TRI_BASE
    a = jnp.where(ri // size == ci // size, m, 0.0)
    t = eye - a
    p = a
    e = 2
    while e < size:
        p = _bmm(p, p)
        t = t + _bmm(t, p)
        e *= 2
    while size < c:
        f = jnp.where((ri // size != ci // size) & (ri // (2 * size) == ci // (2 * size)), m, 0.0)
        t = t - _bmm(t, _bmm(f, t))
        size *= 2
    return t


def _gdn_kernel(x_ref, c0_ref, cw_ref, ab_ref, z_ref, s0_ref, gp_ref, nw_ref, go_ref, so_ref, nc_ref,
                xbuf, s_scr, *, ct, c, t_valid, n_t):
    t = pl.program_id(1)
    H, dk = GDN_HEADS, GDN_DK
    halo = GDN_CONV - 1
    base = 8

    @pl.when(t == 0)
    def _():
        xbuf[base - halo:base, :] = c0_ref[0]
        s_scr[...] = s0_ref[0]

    @pl.when(t > 0)
    def _():
        xbuf[base - halo:base, :] = xbuf[ct + base - halo:ct + base, :]

    xbuf[base:base + ct, :] = x_ref[0]
    acc = cw_ref[0:1, :] * xbuf[base - halo:base - halo + ct, :]
    for i in range(1, GDN_CONV):
        acc = acc + cw_ref[i:i + 1, :] * xbuf[base - halo + i:base - halo + i + ct, :]
    qkv = _silu(acc)

    def heads(off):
        return jnp.stack([qkv[:, (off + h) * dk:(off + h + 1) * dk] for h in range(H)])

    q, k, v = heads(0), heads(H), heads(2 * H)
    q = q * lax.rsqrt(jnp.sum(q * q, axis=-1, keepdims=True) + EPS) * (dk ** -0.5)
    k = k * lax.rsqrt(jnp.sum(k * k, axis=-1, keepdims=True) + EPS)

    ab = ab_ref[0]
    rowi = lax.broadcasted_iota(jnp.int32, (ct, LANES), 0)
    xg = ab + gp_ref[1:2, :]
    softplus = jnp.maximum(xg, 0.0) + jnp.log(1.0 + jnp.exp(-jnp.abs(xg)))
    g_all = -jnp.exp(gp_ref[0:1, :]) * softplus
    b_all = _sigmoid(ab)
    if t_valid % ct != 0:
        valid = (rowi + t * ct) < t_valid
        g_all = jnp.where(valid, g_all, 0.0)
        b_all = jnp.where(valid, b_all, 0.0)
    rin = rowi % c
    gc_all = g_all
    s = 1
    while s < c:
        gc_all = gc_all + jnp.where(rin >= s, pltpu.roll(gc_all, s, axis=0), 0.0)
        s *= 2
    gct = jnp.transpose(gc_all)
    gcol = jnp.stack([gc_all[:, h:h + 1] for h in range(H)])
    grow = jnp.stack([gct[h:h + 1, :] for h in range(H)])
    beta = jnp.stack([b_all[:, H + h:H + h + 1] for h in range(H)])

    ri = lax.broadcasted_iota(jnp.int32, (ct, ct), 0)
    ci = lax.broadcasted_iota(jnp.int32, (ct, ct), 1)
    same = ri // c == ci // c
    eye = jnp.where(ri == ci, 1.0, 0.0)
    decay = jnp.exp(jnp.where(same & (ri >= ci), gcol - grow, -jnp.inf))
    kb = k * beta
    m = jnp.where(same & (ri > ci), _bmm_nt(kb, k) * decay, 0.0)
    tm1 = _unit_lower_inverse(m, eye, ri, ci, c) - eye
    egc = jnp.exp(gcol)
    rhs = jnp.concatenate([v * beta, kb * egc], axis=-1)
    uw = rhs + _bmm(tm1, rhs)
    u, w = uw[..., :dk], uw[..., dk:]
    attn = _bmm_nt(q, k) * decay
    qd = q * egc

    S = s_scr[...]
    o_state, v_new = [], []
    for cc in range(ct // c):
        r0 = cc * c
        if n_t == 1 and r0 >= t_valid:
            o_state.append(jnp.zeros((H, c, dk), F32))
            v_new.append(jnp.zeros((H, c, dk), F32))
            continue
        gl = gcol[:, r0 + c - 1:r0 + c, :]
        kd = k[:, r0:r0 + c] * jnp.exp(gl - gcol[:, r0:r0 + c])
        xs = _bmm(jnp.concatenate([w[:, r0:r0 + c], qd[:, r0:r0 + c]], axis=1), S)
        vn = u[:, r0:r0 + c] - xs[:, :c]
        o_state.append(xs[:, c:])
        v_new.append(vn)
        S = S * jnp.exp(gl) + _bmm_tn(kd, vn)
    s_scr[...] = S
    o = jnp.concatenate(o_state, axis=1) + _bmm(attn, jnp.concatenate(v_new, axis=1))
    o = o * lax.rsqrt(jnp.mean(o * o, axis=-1, keepdims=True) + EPS) * nw_ref[...]
    z = z_ref[0]
    for h in range(H):
        go_ref[0, :, h * dk:(h + 1) * dk] = o[h] * _silu(z[:, h * dk:(h + 1) * dk])

    @pl.when(t == n_t - 1)
    def _():
        so_ref[0] = S
        e0 = base - halo + (t_valid - (n_t - 1) * ct)
        nc_ref[0] = xbuf[e0:e0 + halo, :]


def _gdn(qkv_pre, ab, z, conv0, s0, conv_w, gp, gnw, *, ct, t_valid):
    G, T, W3 = qkv_pre.shape
    H, dk = GDN_HEADS, GDN_DK
    n_t = T // ct
    halo = GDN_CONV - 1
    tile = lambda n: pl.BlockSpec((1, ct, n), lambda b, t: (b, t, 0))
    per_seq = lambda shape: pl.BlockSpec((1,) + shape, lambda b, t: (b,) + (0,) * len(shape))
    whole = lambda shape: pl.BlockSpec(shape, lambda b, t: (0,) * len(shape))
    return pl.pallas_call(
        functools.partial(_gdn_kernel, ct=ct, c=GDN_CHUNK, t_valid=t_valid, n_t=n_t),
        grid=(G, n_t),
        in_specs=[tile(W3), per_seq((halo, W3)), whole((GDN_CONV, W3)), tile(LANES), tile(H * dk),
                  per_seq((H, dk, dk)), whole((8, LANES)), whole((1, dk))],
        out_specs=[tile(H * dk), per_seq((H, dk, dk)), per_seq((halo, W3))],
        out_shape=[jax.ShapeDtypeStruct((G, T, H * dk), F32), jax.ShapeDtypeStruct((G, H, dk, dk), F32),
                   jax.ShapeDtypeStruct((G, halo, W3), F32)],
        scratch_shapes=[pltpu.VMEM((ct + 8, W3), F32), pltpu.VMEM((H, dk, dk), F32)],
        compiler_params=pltpu.CompilerParams(dimension_semantics=("parallel", "arbitrary"),
                                             vmem_limit_bytes=VMEM_LIMIT),
        name="gdn",
    )(qkv_pre, conv0, conv_w, ab, z, s0, gp, gnw)


def _alibi_slope(head_f):
    return jnp.exp2(-8.0 * (head_f + 1.0) / MOBA_HEADS)


def _moba_prompt_kernel(q_ref, kt_ref, vt_ref, o_ref, ka_ref, kb_ref, vb_ref, qa_ref, *, nb, blk):
    p = pl.program_id(1)
    hd = MOBA_HD
    T = nb * blk
    nr = 16
    nt = functools.partial(lax.dot_general, dimension_numbers=(((1,), (1,)), ((), ())),
                           preferred_element_type=F32)

    kt = kt_ref[0]
    rowk = lax.broadcasted_iota(jnp.int32, (2 * hd, T), 0)
    blk_of_col = lax.broadcasted_iota(jnp.int32, (2 * hd, T), 1) // blk
    ka_ref[...] = jnp.where(rowk < hd, kt, jnp.where(rowk - hd == blk_of_col, 1.0, 0.0)).astype(BF16)
    kb_ref[...] = jnp.where(rowk >= hd, kt, jnp.where(rowk == blk_of_col, 1.0, 0.0)).astype(BF16)
    vb_ref[...] = vt_ref[0].astype(BF16)
    rown = lax.broadcasted_iota(jnp.int32, (nr, T), 0)
    coln = lax.broadcasted_iota(jnp.int32, (nr, T), 1) // blk
    avg = jnp.where(rown == coln, 1.0 / blk, 0.0).astype(BF16)
    k1 = kt.astype(BF16)
    r1 = kt - k1.astype(F32)
    k2 = r1.astype(BF16)
    k3 = (r1 - k2.astype(F32)).astype(BF16)
    kmt = nt(avg, k1) + (nt(avg, k2) + nt(avg, k3))
    kmt_hi, kmt_lo = _split(kmt)

    q = q_ref[0]
    lane = lax.broadcasted_iota(jnp.int32, (T, LANES), 1)
    pf = jnp.full((1, 1), p, jnp.int32).astype(F32)
    e_row = lax.broadcasted_iota(jnp.int32, (nr, LANES), 0)
    e_lane = lax.broadcasted_iota(jnp.int32, (nr, LANES), 1)
    slope = []
    for hh in range(2):
        in_head = (lane >= hh * hd) & (lane < (hh + 1) * hd)
        qx = jnp.where(in_head, q, 0.0)
        slope.append(_alibi_slope(2.0 * pf + hh))
        q_hi, q_lo = _split(qx)
        gt = nt(kmt_hi, q_hi) + (nt(kmt_lo, q_hi) + nt(kmt_hi, q_lo))
        gt = jnp.where(rown < coln, gt, -jnp.inf)
        sel = jnp.where(rown == coln, 1.0, 0.0)
        for n in range(nb):
            gn = gt[n:n + 1, :]
            ahead = jnp.where((gt > gn) | ((gt == gn) & (rown < n)), 1.0, 0.0)
            cnt = jnp.sum(ahead, axis=0, keepdims=True)
            sel = jnp.where((rown == n) & (cnt < MOBA_TOPK) & (gn > -jnp.inf), 1.0, sel)
        off = hd if hh == 0 else 0
        place = jnp.where((e_lane == e_row + off) & (e_row < nb), 1.0, 0.0).astype(BF16)
        placed = lax.dot_general(sel.astype(BF16), place, (((0,), (0,)), ((), ())),
                                 preferred_element_type=F32)
        qa_ref[hh] = jnp.where(in_head, q * LOG2E, jnp.where(placed > 0.5, 0.0, NEG)).astype(BF16)

    colf = lax.broadcasted_iota(jnp.int32, (1, T), 1).astype(F32)
    lane_blk = lax.broadcasted_iota(jnp.int32, (blk, LANES), 1)
    ri = lax.broadcasted_iota(jnp.int32, (blk, blk), 0)
    ci = lax.broadcasted_iota(jnp.int32, (blk, blk), 1)
    k_refs = (ka_ref, kb_ref)
    for j in range(nb):
        w = (j + 1) * blk
        outs = []
        for hh in range(2):
            s = jnp.dot(qa_ref[hh, j * blk:(j + 1) * blk, :], k_refs[hh][:, :w], preferred_element_type=F32)
            s = s + (slope[hh] * LOG2E) * (colf[:, :w] - j * blk)
            own = jnp.where(ri >= ci, s[:, j * blk:], NEG)
            s = own if j == 0 else jnp.concatenate([s[:, :j * blk], own], axis=1)
            pr = jnp.exp2(s - jnp.max(s, axis=1, keepdims=True))
            den = jnp.sum(pr, axis=1, keepdims=True)
            outs.append(nt(pr.astype(BF16), vb_ref[:, :w]) / den)
        o_ref[0, j * blk:(j + 1) * blk, :] = jnp.where(lane_blk < hd, outs[0], outs[1])


def _moba_prompt(mq, kt, vt):
    B, T, W = mq.shape
    blk = MOBA_BLOCK
    nb = T // blk
    assert nb <= 16
    qspec = pl.BlockSpec((1, T, LANES), lambda b, p: (b, 0, p))
    kspec = pl.BlockSpec((1, LANES, T), lambda b, p: (b, p, 0))
    return pl.pallas_call(
        functools.partial(_moba_prompt_kernel, nb=nb, blk=blk),
        grid=(B, W // LANES),
        in_specs=[qspec, kspec, kspec],
        out_specs=qspec,
        out_shape=jax.ShapeDtypeStruct((B, T, W), F32),
        scratch_shapes=[pltpu.VMEM((LANES, T), BF16)] * 3 + [pltpu.VMEM((2, T, LANES), BF16)],
        compiler_params=pltpu.CompilerParams(dimension_semantics=("parallel", "parallel"),
                                             vmem_limit_bytes=VMEM_LIMIT),
        name="moba_prompt",
    )(mq, kt, vt)


def _kmean_topk_kernel(pt_ref, *refs, n_pg, n_steps, ppb, nblk):
    pages = refs[:n_pg]
    q_ref, idx_ref, km_ref = refs[n_pg:]
    i = pl.program_id(1)
    W = MOBA_HEADS * MOBA_HD
    lane1 = lax.broadcasted_iota(jnp.int32, (1, LANES), 1)

    @pl.when(i == 0)
    def _():
        km_ref[...] = jnp.zeros((W, LANES), F32)

    km = km_ref[...]
    for bb in range(n_pg // ppb):
        tot = pages[bb * ppb][...]
        for r in range(1, ppb):
            tot = tot + pages[bb * ppb + r][...]
        col = jnp.sum(tot.reshape(W, PAGE_SIZE), axis=1, keepdims=True) * (1.0 / (ppb * PAGE_SIZE))
        km = jnp.where(lane1 == i * (n_pg // ppb) + bb, col, km)
    km_ref[...] = km

    @pl.when(i == n_steps - 1)
    def _():
        rows = q_ref.shape[1]
        lanef = lax.broadcasted_iota(jnp.int32, (rows, LANES), 1).astype(F32)
        for h in range(MOBA_HEADS):
            qh = q_ref[0, :, h * MOBA_HD:(h + 1) * MOBA_HD]
            gate = _dot3(qh, km[h * MOBA_HD:(h + 1) * MOBA_HD, :])
            gate = jnp.where(lanef < nblk, gate, -jnp.inf)
            out = jnp.zeros((rows, LANES), F32)
            for r in range(MOBA_TOPK):
                mx = jnp.max(gate, axis=1, keepdims=True)
                idx = jnp.min(jnp.where(gate == mx, lanef, 1e9), axis=1, keepdims=True)
                out = jnp.where(lanef == r, idx, out)
                gate = jnp.where(lanef == idx, -jnp.inf, gate)
            idx_ref[0, h] = out.astype(jnp.int32)


def _kmean_topk(page_table, cache_kt, mq_pad, *, n_pg=32):
    DB, n_pages = page_table.shape
    ppb = MOBA_BLOCK // PAGE_SIZE
    nblk = n_pages // ppb
    n_pg = min(n_pg, n_pages)
    assert n_pages % n_pg == 0 and n_pg % ppb == 0 and nblk <= LANES
    n_steps = n_pages // n_pg
    rows = mq_pad.shape[1]
    W = MOBA_HEADS * MOBA_HD

    def page_spec(r):
        return pl.BlockSpec((None, MOBA_HEADS, MOBA_HD, PAGE_SIZE),
                            lambda b, i, pt: (pt[b * n_pages + i * n_pg + r], 0, 0, 0))

    grid_spec = pltpu.PrefetchScalarGridSpec(
        num_scalar_prefetch=1,
        grid=(DB, n_steps),
        in_specs=[page_spec(r) for r in range(n_pg)] + [pl.BlockSpec((1, rows, W), lambda b, i, pt: (b, 0, 0))],
        out_specs=pl.BlockSpec((1, MOBA_HEADS, rows, LANES), lambda b, i, pt: (b, 0, 0, 0)),
        scratch_shapes=[pltpu.VMEM((W, LANES), F32)],
    )
    return pl.pallas_call(
        functools.partial(_kmean_topk_kernel, n_pg=n_pg, n_steps=n_steps, ppb=ppb, nblk=nblk),
        grid_spec=grid_spec,
        out_shape=jax.ShapeDtypeStruct((DB, MOBA_HEADS, rows, LANES), jnp.int32),
        compiler_params=pltpu.CompilerParams(dimension_semantics=("parallel", "arbitrary"),
                                             vmem_limit_bytes=VMEM_LIMIT),
        name="kmean_topk",
    )(page_table.reshape(-1), *([cache_kt] * n_pg), mq_pad)


def _decode_attn_kernel(phys_ref, idx_ref, qt_ref, kown_ref, vown_ref, ck_hbm, cv_hbm, o_ref,
                        kbuf, vbuf, sem, *, ds, past, n_slab):
    b = pl.program_id(0)
    nb = pl.num_programs(0)
    hd = MOBA_HD
    ppb = MOBA_BLOCK // PAGE_SIZE

    def copies(bi, slot, u):
        slab = phys_ref[bi * n_slab + u]
        return (pltpu.make_async_copy(ck_hbm.at[slab], kbuf.at[slot, u], sem.at[slot, 0]),
                pltpu.make_async_copy(cv_hbm.at[slab], vbuf.at[slot, u], sem.at[slot, 1]))

    def start_all(bi, slot):
        def body(u, carry):
            ck, cv = copies(bi, slot, u)
            ck.start()
            cv.start()
            return carry
        lax.fori_loop(0, n_slab, body, 0, unroll=8)

    def wait_all(bi, slot):
        def body(u, carry):
            ck, cv = copies(bi, slot, u)
            ck.wait()
            cv.wait()
            return carry
        lax.fori_loop(0, n_slab, body, 0, unroll=8)

    slot = b % 2

    @pl.when(b == 0)
    def _():
        start_all(b, slot)

    @pl.when(b + 1 < nb)
    def _():
        start_all(b + 1, 1 - slot)

    wait_all(b, slot)

    lane = lax.broadcasted_iota(jnp.int32, (1, LANES), 1)
    lanef = lane.astype(F32)

    def head_body(h, carry):
        r0 = pl.multiple_of(h * hd, hd)
        slope = _alibi_slope(jnp.full((1, 1), h, jnp.int32).astype(F32))
        kown = kown_ref[0, pl.ds(r0, hd), :]
        vown = vown_ref[0, pl.ds(r0, hd), :]
        qt = qt_ref[0, pl.ds(r0, hd), :]
        otile = jnp.zeros((hd, LANES), F32)
        for t in range(ds):
            qb = jnp.broadcast_to(qt[:, t:t + 1], (hd, LANES))
            s_own = jnp.sum(qb * kown, axis=0, keepdims=True)
            s_own = jnp.where(lane <= t, s_own - slope * (t - lanef), NEG)
            scores = [s_own]
            for s in range(MOBA_TOPK):
                blk_id = idx_ref[(b * MOBA_HEADS + h) * (ds * MOBA_TOPK) + t * MOBA_TOPK + s]
                for r in range(ppb):
                    u = ((h * ds + t) * MOBA_TOPK + s) * ppb + r
                    sc = jnp.sum(qb * kbuf[slot, u], axis=0, keepdims=True)
                    start = (blk_id * MOBA_BLOCK + r * PAGE_SIZE).astype(F32)
                    scores.append(sc - slope * ((past + t) - start - lanef))
            mx = scores[0]
            for sc in scores[1:]:
                mx = jnp.maximum(mx, sc)
            mx = jnp.max(mx, axis=1, keepdims=True)
            pr = jnp.exp(scores[0] - mx)
            den = pr
            acc = pr * vown
            i = 1
            for s in range(MOBA_TOPK):
                for r in range(ppb):
                    u = ((h * ds + t) * MOBA_TOPK + s) * ppb + r
                    pr = jnp.exp(scores[i] - mx)
                    den = den + pr
                    acc = acc + pr * vbuf[slot, u]
                    i += 1
            o_col = jnp.sum(acc, axis=1, keepdims=True) / jnp.sum(den, axis=1, keepdims=True)
            otile = jnp.where(lane == t, o_col, otile)
        o_ref[0, pl.ds(r0, hd), :] = otile
        return carry

    lax.fori_loop(0, MOBA_HEADS, head_body, 0)


def _decode_attn(phys, idx, qt, kown_t, vown_t, cache_kt, cache_vt, *, ds, past):
    DB, W, _ = qt.shape
    n_slab = MOBA_HEADS * ds * MOBA_TOPK * (MOBA_BLOCK // PAGE_SIZE)
    tile = pl.BlockSpec((1, W, LANES), lambda b, ph, ix: (b, 0, 0))
    grid_spec = pltpu.PrefetchScalarGridSpec(
        num_scalar_prefetch=2,
        grid=(DB,),
        in_specs=[tile, tile, tile, pl.BlockSpec(memory_space=pl.ANY), pl.BlockSpec(memory_space=pl.ANY)],
        out_specs=tile,
        scratch_shapes=[pltpu.VMEM((2, n_slab, MOBA_HD, PAGE_SIZE), F32),
                        pltpu.VMEM((2, n_slab, MOBA_HD, PAGE_SIZE), F32),
                        pltpu.SemaphoreType.DMA((2, 2))],
    )
    return pl.pallas_call(
        functools.partial(_decode_attn_kernel, ds=ds, past=past, n_slab=n_slab),
        grid_spec=grid_spec,
        out_shape=jax.ShapeDtypeStruct((DB, W, LANES), F32),
        compiler_params=pltpu.CompilerParams(dimension_semantics=("arbitrary",),
                                             vmem_limit_bytes=VMEM_LIMIT),
        name="decode_attn",
    )(phys, idx, qt, kown_t, vown_t, cache_kt, cache_vt)


def _tokens_on_lanes(a, db, ds):
    w = a.shape[0]
    a = a.reshape(w, db, ds).transpose(1, 0, 2)
    return jnp.pad(a, ((0, 0), (0, 0), (0, LANES - ds)))


def _layer_weights(w_in, conv_w, a_log, dt_bias, ffn_up, ffn_down, w_out):
    hq = GDN_HEADS * GDN_DK
    n_qkv = 3 * hq
    o_a = n_qkv
    o_z = o_a + 2 * GDN_HEADS
    o_m = o_z + hq
    n_m = MOBA_HEADS * MOBA_HD
    wa = jnp.concatenate([w_in[:, :n_qkv], w_in[:, o_z:o_m + n_m]], axis=1).astype(BF16)
    wkv_t = w_in[:, o_m + n_m:].T.astype(BF16)
    wab = jnp.pad(w_in[:, o_a:o_z], ((0, 0), (0, LANES - 2 * GDN_HEADS)))
    wab_hi = wab.astype(BF16)
    wab_lo = (wab - wab_hi.astype(F32)).astype(BF16)
    gp = jnp.zeros((8, LANES), F32).at[0, :GDN_HEADS].set(a_log).at[1, :GDN_HEADS].set(dt_bias)
    return dict(wa=wa, wkv_t=wkv_t, wab_hi=wab_hi, wab_lo=wab_lo, gp=gp, conv_w=conv_w,
                up=[ffn_up[i].astype(BF16) for i in range(2)],
                down=[ffn_down[i].astype(BF16) for i in range(2)],
                wo=w_out.astype(BF16), n_qkv=n_qkv, n_z=hq, n_m=n_m)


def _trunk(x, mods, lw, norm_w, gdn_norm_w, conv0, s0, attend, *, tm, ct, t_valid, final_w):
    sh1, sc1, g1, sh2, sc2, g2, sh3, sc3, g3 = mods
    h = _ffn(x, sh1, sc1, g1, norm_w[0], lw["up"][0], lw["down"][0], tm=tm)
    qkv_pre, z, mq, kt, vt, ab = _win(h, sh2, sc2, norm_w[1], lw["wa"], lw["wkv_t"], lw["wab_hi"],
                                      lw["wab_lo"], tm=tm, n_qkv=lw["n_qkv"], n_z=lw["n_z"], n_m=lw["n_m"])
    go, s_new, conv_new = _gdn(*attend["gdn_in"](qkv_pre, ab, z), conv0, s0, lw["conv_w"], lw["gp"],
                               gdn_norm_w.reshape(1, -1), ct=ct, t_valid=t_valid)
    go = attend["gdn_out"](go)
    mo = attend["moba"](mq, kt, vt)
    h = _ffn(h, sh3, sc3, g3, norm_w[2], lw["up"][1], lw["down"][1], tm=tm, mix=(go, mo, g2, lw["wo"]),
             final_w=final_w)
    return h, s_new, conv_new, kt, vt


def kernel(x_prompt, x_sample, c_prompt, c_sample, cache_k, cache_v, page_table, state_gdn, state_conv,
           w_ada, b_ada, norm_w, ffn_up, ffn_down, w_in, conv_w, a_log, dt_bias, gdn_norm_w, w_out,
           final_norm_w):
    B, T, D = x_prompt.shape
    DB, DS, _ = x_sample.shape
    depth = w_ada.shape[0]
    past = page_table.shape[1] * PAGE_SIZE
    ppb = MOBA_BLOCK // PAGE_SIZE
    assert past % MOBA_BLOCK == 0 and past // MOBA_BLOCK >= MOBA_TOPK and DS <= PAGE_SIZE
    assert T % MOBA_BLOCK == 0 and (T - 1) // MOBA_BLOCK >= MOBA_TOPK
    NS = DB * DS
    hq = GDN_HEADS * GDN_DK
    n_m = MOBA_HEADS * MOBA_HD
    ct_s = GDN_CHUNK

    hp = x_prompt
    hs = x_sample.reshape(1, NS, D)
    outs = [[] for _ in range(8)]
    for l in range(depth):
        lw = _layer_weights(w_in[l], conv_w[l], a_log[l], dt_bias[l], ffn_up[l], ffn_down[l], w_out[l])
        mod = _ada_mod(jnp.concatenate([c_prompt, c_sample], axis=0), w_ada[l], b_ada[l])
        mod_p = mod[:B].reshape(B, N_MOD, 1, D)
        mod_s = jnp.repeat(mod[B:].reshape(DB, N_MOD, D), DS, axis=0).reshape(1, NS, N_MOD, D)
        mods_p = [mod_p[:, i] for i in range(N_MOD)]
        mods_s = [mod_s[:, :, i] for i in range(N_MOD)]
        last = l == depth - 1
        fw = final_norm_w if last else None

        attend_p = dict(gdn_in=lambda qkv, ab, z: (qkv, ab, z), gdn_out=lambda go: go, moba=_moba_prompt)
        hp, s_p, c_p, kt_p, vt_p = _trunk(
            hp, mods_p, lw, norm_w[l], gdn_norm_w[l], jnp.zeros((B, GDN_CONV - 1, 3 * hq), F32),
            jnp.zeros((B, GDN_HEADS, GDN_DK, GDN_DK), F32), attend_p, tm=512, ct=256, t_valid=T, final_w=fw)

        cache_kt = jnp.transpose(cache_k[l], (0, 2, 3, 1))
        cache_vt = jnp.transpose(cache_v[l], (0, 2, 3, 1))
        page_tbl = page_table

        def pad_seq(a):
            return jnp.pad(a.reshape(DB, DS, a.shape[-1]), ((0, 0), (0, ct_s - DS), (0, 0)))

        def moba_s(mq, kt, vt, cache_kt=cache_kt, cache_vt=cache_vt, page_tbl=page_tbl):
            mq_pad = jnp.pad(mq.reshape(DB, DS, n_m), ((0, 0), (0, 16 - DS), (0, 0)))
            idx = _kmean_topk(page_tbl, cache_kt, mq_pad)[:, :, :DS, :MOBA_TOPK]
            lpage = idx[..., None] * ppb + jnp.arange(ppb, dtype=jnp.int32)
            phys = jnp.take_along_axis(page_tbl, lpage.reshape(DB, -1), axis=1).reshape(lpage.shape)
            slab = phys * MOBA_HEADS + jnp.arange(MOBA_HEADS, dtype=jnp.int32)[None, :, None, None, None]
            qt = _tokens_on_lanes(mq[0].T, DB, DS)
            slabs = lambda c: c.reshape((-1,) + c.shape[2:])
            ot = _decode_attn(slab.reshape(-1), idx.reshape(-1), qt, _tokens_on_lanes(kt[0], DB, DS),
                              _tokens_on_lanes(vt[0], DB, DS), slabs(cache_kt), slabs(cache_vt),
                              ds=DS, past=past)
            return ot[:, :, :DS].transpose(0, 2, 1).reshape(1, NS, n_m)

        attend_s = dict(gdn_in=lambda qkv, ab, z: (pad_seq(qkv), pad_seq(ab), pad_seq(z)),
                        gdn_out=lambda go: go[:, :DS].reshape(1, NS, hq), moba=moba_s)
        hs, s_s, c_s, kt_s, vt_s = _trunk(
            hs, mods_s, lw, norm_w[l], gdn_norm_w[l], state_conv[l], state_gdn[l], attend_s,
            tm=NS, ct=ct_s, t_valid=DS, final_w=fw)

        def heads_last(a_t, b, t):
            return a_t.reshape(b, MOBA_HEADS, MOBA_HD, t).transpose(0, 3, 1, 2)

        new = [heads_last(kt_p, B, T), heads_last(vt_p, B, T), s_p, c_p,
               heads_last(kt_s.reshape(n_m, DB, DS).transpose(1, 0, 2), DB, DS),
               heads_last(vt_s.reshape(n_m, DB, DS).transpose(1, 0, 2), DB, DS), s_s, c_s]
        for o, v in zip(outs, new):
            o.append(v)
    y_prompt = hp
    y_sample = hs.reshape(DB, DS, D)
    return (y_prompt, y_sample) + tuple(jnp.stack(o) for o in outs)
```

```python
import functools

import jax
import jax.numpy as jnp
from jax import lax
from jax.experimental import pallas as pl
from jax.experimental.pallas import tpu as pltpu

F32 = jnp.float32
BF16 = jnp.bfloat16
EPS = 1e-6
NEG = -1e30

GDN_HEADS = 4
GDN_DK = 128
GDN_CONV = 4
GDN_CHUNK = 64
MOBA_HEADS = 8
MOBA_HD = 64
MOBA_BLOCK = 256
MOBA_TOPK = 3
PAGE_SIZE = 128
N_MOD = 9
LANES = 128
VMEM_LIMIT = 56 * 1024 * 1024


def _sigmoid(x):
    return 1.0 / (1.0 + jnp.exp(-x))


def _silu(x):
    return x * _sigmoid(x)


def _bdot(a, b):
    return jnp.dot(a.astype(BF16), b.astype(BF16), preferred_element_type=F32)


def _bdot_nt(a, b):
    return lax.dot_general(a.astype(BF16), b.astype(BF16), (((1,), (1,)), ((), ())),
                           preferred_element_type=F32)


def _split(x):
    hi = x.astype(BF16)
    lo = (x - hi.astype(F32)).astype(BF16)
    return hi, lo


def _dot3(a, b):
    a_hi, a_lo = _split(a)
    b_hi, b_lo = _split(b)
    d = functools.partial(jnp.dot, preferred_element_type=F32)
    return d(a_hi, b_hi) + (d(a_lo, b_hi) + d(a_hi, b_lo))


def _norm_mod(x, nw, sc, sh):
    y = x * lax.rsqrt(jnp.mean(x * x, axis=-1, keepdims=True) + EPS)
    return (y * nw) * (1.0 + sc) + sh


def _mod_spec(rows, tm, d):
    if rows == 1:
        return pl.BlockSpec((1, 1, d), lambda g, t: (g, 0, 0))
    return pl.BlockSpec((1, tm, d), lambda g, t: (g, t, 0))


def _resident(shape):
    nd = len(shape)
    return pl.BlockSpec(shape, lambda *_: (0,) * nd, pipeline_mode=pl.Buffered(1))


def _ada_kernel(c_ref, w_ref, b_ref, o_ref):
    o_ref[...] = _dot3(_silu(c_ref[...]), w_ref[...]) + b_ref[...]


def _ada_mod(c_all, w, b):
    n, d = c_all.shape
    dout = w.shape[1]
    tn = d
    return pl.pallas_call(
        _ada_kernel,
        grid=(dout // tn,),
        in_specs=[pl.BlockSpec((n, d), lambda i: (0, 0)),
                  pl.BlockSpec((d, tn), lambda i: (0, i)),
                  pl.BlockSpec((1, tn), lambda i: (0, i))],
        out_specs=pl.BlockSpec((n, tn), lambda i: (0, i)),
        out_shape=jax.ShapeDtypeStruct((n, dout), F32),
        compiler_params=pltpu.CompilerParams(dimension_semantics=("arbitrary",),
                                             vmem_limit_bytes=VMEM_LIMIT),
        name="ada_mod",
    )(c_all, w, b.reshape(1, dout))


def _ffn_kernel(*refs, tf, sub, has_mix, has_final):
    it = iter(refs)
    x_ref, sh_ref, sc_ref, g_ref, nw_ref, wup_ref, wdn_ref = [next(it) for _ in range(7)]
    if has_mix:
        go_ref, mo_ref, g2_ref, wo_ref = [next(it) for _ in range(4)]
    if has_final:
        fw_ref = next(it)
    o_ref = next(it)
    a_scr = next(it)

    tm = x_ref.shape[1]
    d_ff = wdn_ref.shape[0]

    def rows(ref, r0):
        return ref[0] if ref.shape[1] == 1 else ref[0, r0:r0 + sub, :]

    for r0 in range(0, tm, sub):
        x = x_ref[0, r0:r0 + sub, :]
        if has_mix:
            half = go_ref.shape[-1]
            mix = (_bdot(go_ref[0, r0:r0 + sub, :], wo_ref[0:half, :])
                   + _bdot(mo_ref[0, r0:r0 + sub, :], wo_ref[half:, :]))
            x = x + rows(g2_ref, r0) * mix
        hb = _norm_mod(x, nw_ref[...], rows(sc_ref, r0), rows(sh_ref, r0)).astype(BF16)
        for c in range(d_ff // tf):
            gate = jnp.dot(hb, wup_ref[:, c * tf:(c + 1) * tf], preferred_element_type=F32)
            up = jnp.dot(hb, wup_ref[:, d_ff + c * tf:d_ff + (c + 1) * tf], preferred_element_type=F32)
            a_scr[r0:r0 + sub, c * tf:(c + 1) * tf] = (_silu(gate) * up).astype(BF16)
        down = jnp.dot(a_scr[r0:r0 + sub, :], wdn_ref[...], preferred_element_type=F32)
        out = x + (0.5 * rows(g_ref, r0)) * down
        if has_final:
            out = out * lax.rsqrt(jnp.mean(out * out, axis=-1, keepdims=True) + EPS) * fw_ref[...]
        o_ref[0, r0:r0 + sub, :] = out


def _ffn(x, sh, sc, g, nw, wup, wdn, *, tm, sub=None, mix=None, final_w=None):
    G, T, D = x.shape
    d_ff = wdn.shape[0]
    mxu_n = 256
    tf = d_ff // 2 if (d_ff // 2) % mxu_n == 0 else d_ff
    row = pl.BlockSpec((1, tm, D), lambda g_, t: (g_, t, 0))
    ins = [x, sh, sc, g, nw.reshape(1, D), wup, wdn]
    specs = [row, _mod_spec(sh.shape[1], tm, D), _mod_spec(sc.shape[1], tm, D),
             _mod_spec(g.shape[1], tm, D), _resident((1, D)), _resident(wup.shape), _resident(wdn.shape)]
    if mix is not None:
        go, mo, g2, wo = mix
        half = go.shape[-1]
        ins += [go, mo, g2, wo]
        specs += [pl.BlockSpec((1, tm, half), lambda g_, t: (g_, t, 0)),
                  pl.BlockSpec((1, tm, half), lambda g_, t: (g_, t, 0)),
                  _mod_spec(g2.shape[1], tm, D), _resident(wo.shape)]
    if final_w is not None:
        ins.append(final_w.reshape(1, D))
        specs.append(_resident((1, D)))
    return pl.pallas_call(
        functools.partial(_ffn_kernel, tf=tf, sub=sub or tm, has_mix=mix is not None,
                          has_final=final_w is not None),
        grid=(G, T // tm),
        in_specs=specs,
        out_specs=row,
        out_shape=jax.ShapeDtypeStruct((G, T, D), F32),
        scratch_shapes=[pltpu.VMEM((tm, d_ff), BF16)],
        compiler_params=pltpu.CompilerParams(dimension_semantics=("parallel", "parallel"),
                                             vmem_limit_bytes=VMEM_LIMIT),
        name="ffn_mix" if mix is not None else "ffn",
    )(*ins)


def _win_kernel(h_ref, sh_ref, sc_ref, nw_ref, wa_ref, wkv_ref, wab_ref,
                qkv_ref, z_ref, mq_ref, kt_ref, vt_ref, ab_ref, *, n_qkv, n_z, n_m):
    hn = _norm_mod(h_ref[0], nw_ref[...], sc_ref[0], sh_ref[0])
    hb = hn.astype(BF16)
    h_lo = (hn - hb.astype(F32)).astype(BF16)
    d = functools.partial(jnp.dot, preferred_element_type=F32)
    qkv_ref[0] = d(hb, wa_ref[:, 0:n_qkv])
    z_ref[0] = d(hb, wa_ref[:, n_qkv:n_qkv + n_z])
    mq_ref[0] = d(hb, wa_ref[:, n_qkv + n_z:n_qkv + n_z + n_m]) * (MOBA_HD ** -0.5)
    nt = functools.partial(lax.dot_general, dimension_numbers=(((1,), (1,)), ((), ())),
                           preferred_element_type=F32)
    kt_ref[0] = nt(wkv_ref[0:n_m, :], hb)
    vt_ref[0] = nt(wkv_ref[n_m:, :], hb)
    r_hi = d(hb, wab_ref[...])
    r_lo = d(h_lo, wab_ref[...])
    ab_ref[0] = (r_hi[:, :LANES] + r_lo[:, :LANES]) + (r_hi[:, LANES:] + r_lo[:, LANES:])


def _win(h, sh, sc, nw, wa, wkv_t, wab, *, tm, n_qkv, n_z, n_m):
    G, T, D = h.shape
    row = lambda n: pl.BlockSpec((1, tm, n), lambda g_, t: (g_, t, 0))
    col = pl.BlockSpec((1, n_m, tm), lambda g_, t: (g_, 0, t))
    return pl.pallas_call(
        functools.partial(_win_kernel, n_qkv=n_qkv, n_z=n_z, n_m=n_m),
        grid=(G, T // tm),
        in_specs=[row(D), _mod_spec(sh.shape[1], tm, D), _mod_spec(sc.shape[1], tm, D),
                  _resident((1, D)), _resident(wa.shape), _resident(wkv_t.shape),
                  _resident(wab.shape)],
        out_specs=[row(n_qkv), row(n_z), row(n_m), col, col, row(LANES)],
        out_shape=[jax.ShapeDtypeStruct((G, T, n_qkv), F32), jax.ShapeDtypeStruct((G, T, n_z), F32),
                   jax.ShapeDtypeStruct((G, T, n_m), F32), jax.ShapeDtypeStruct((G, n_m, T), F32),
                   jax.ShapeDtypeStruct((G, n_m, T), F32), jax.ShapeDtypeStruct((G, T, LANES), F32)],
        compiler_params=pltpu.CompilerParams(dimension_semantics=("parallel", "parallel"),
                                             vmem_limit_bytes=VMEM_LIMIT),
        name="w_in",
    )(h, sh, sc, nw.reshape(1, D), wa, wkv_t, wab)


TRI_BASE = 8


def _bmm(a, b):
    return lax.dot_general(a.astype(BF16), b.astype(BF16), (((2,), (1,)), ((0,), (0,))),
                           preferred_element_type=F32)


def _bmm_nt(a, b):
    return lax.dot_general(a.astype(BF16), b.astype(BF16), (((2,), (2,)), ((0,), (0,))),
                           preferred_element_type=F32)


def _bmm_tn(a, b):
    return lax.dot_general(a.astype(BF16), b.astype(BF16), (((1,), (1,)), ((0,), (0,))),
                           preferred_element_type=F32)


def _unit_lower_inverse(m, eye, ri, ci, c):
    size = TRI_BASE
    a = jnp.where(ri // size == ci // size, m, 0.0)
    t = eye - a
    p = a
    e = 2
    while e < size:
        p = _bmm(p, p)
        t = t + _bmm(t, p)
        e *= 2
    h, n, _ = m.shape
    while size < c:
        pairs = n // (2 * size)
        split = lambda x: x.reshape(h, pairs, 2, size, n)
        t_up, t_low = split(t)[:, :, 0], split(t)[:, :, 1].reshape(h, n // 2, n)
        pair_ok = (ri // (2 * size) == ci // (2 * size)) & (ri // size != ci // size)
        f_low = split(jnp.where(pair_ok, m, 0.0))[:, :, 1].reshape(h, n // 2, n)
        ft = _bmm(f_low, t).reshape(h, pairs, size, n)
        ft_full = jnp.stack([jnp.zeros_like(ft), ft], axis=2).reshape(h, n, n)
        t_low = t_low - _bmm(t_low, ft_full)
        t = jnp.stack([t_up, t_low.reshape(h, pairs, size, n)], axis=2).reshape(h, n, n)
        size *= 2
    return t


def _gdn_kernel(x_ref, c0_ref, cw_ref, ab_ref, z_ref, s0_ref, gp_ref, nw_ref, go_ref, so_ref, nc_ref,
                xbuf, s_scr, *, ct, c, t_valid, n_t):
    t = pl.program_id(1)
    H, dk = GDN_HEADS, GDN_DK
    halo = GDN_CONV - 1
    base = 8

    n_col = xbuf.shape[0]
    col = lambda j: slice(j * dk, (j + 1) * dk)

    @pl.when(t == 0)
    def _():
        for j in range(n_col):
            xbuf[j, base - halo:base, :] = c0_ref[0, :, col(j)]
        s_scr[...] = s0_ref[0]

    @pl.when(t > 0)
    def _():
        for j in range(n_col):
            xbuf[j, base - halo:base, :] = xbuf[j, ct + base - halo:ct + base, :]

    conv = []
    for j in range(n_col):
        xbuf[j, base:base + ct, :] = x_ref[0, :, col(j)]
        acc = cw_ref[0:1, col(j)] * xbuf[j, base - halo:base - halo + ct, :]
        for i in range(1, GDN_CONV):
            acc = acc + cw_ref[i:i + 1, col(j)] * xbuf[j, base - halo + i:base - halo + i + ct, :]
        conv.append(_silu(acc))
    q, k, v = jnp.stack(conv[0:H]), jnp.stack(conv[H:2 * H]), jnp.stack(conv[2 * H:3 * H])
    q = q * lax.rsqrt(jnp.sum(q * q, axis=-1, keepdims=True) + EPS) * (dk ** -0.5)
    k = k * lax.rsqrt(jnp.sum(k * k, axis=-1, keepdims=True) + EPS)

    ab = ab_ref[0]
    rowi = lax.broadcasted_iota(jnp.int32, (ct, LANES), 0)
    xg = ab + gp_ref[1:2, :]
    softplus = jnp.maximum(xg, 0.0) + jnp.log(1.0 + jnp.exp(-jnp.abs(xg)))
    g_all = -jnp.exp(gp_ref[0:1, :]) * softplus
    b_all = _sigmoid(ab)
    if t_valid % ct != 0:
        valid = (rowi + t * ct) < t_valid
        g_all = jnp.where(valid, g_all, 0.0)
        b_all = jnp.where(valid, b_all, 0.0)
    rin = rowi % c
    gc_all = g_all
    s = 1
    while s < c:
        gc_all = gc_all + jnp.where(rin >= s, pltpu.roll(gc_all, s, axis=0), 0.0)
        s *= 2
    gct = jnp.transpose(gc_all)
    gcol = jnp.stack([gc_all[:, h:h + 1] for h in range(H)])
    grow = jnp.stack([gct[h:h + 1, :] for h in range(H)])
    beta = jnp.stack([b_all[:, H + h:H + h + 1] for h in range(H)])

    ri = lax.broadcasted_iota(jnp.int32, (ct, ct), 0)
    ci = lax.broadcasted_iota(jnp.int32, (ct, ct), 1)
    same = ri // c == ci // c
    eye = jnp.where(ri == ci, 1.0, 0.0)
    decay = jnp.exp(jnp.where(same & (ri >= ci), gcol - grow, -jnp.inf))
    kb = k * beta
    m = jnp.where(same & (ri > ci), _bmm_nt(kb, k) * decay, 0.0)
    tm1 = _unit_lower_inverse(m, eye, ri, ci, c) - eye
    egc = jnp.exp(gcol)
    rhs = jnp.concatenate([v * beta, kb * egc], axis=-1)
    uw = rhs + _bmm(tm1, rhs)
    u, w = uw[..., :dk], uw[..., dk:]
    attn = _bmm_nt(q, k) * decay
    qd = q * egc

    S = s_scr[...]
    o_state, v_new = [], []
    for cc in range(ct // c):
        r0 = cc * c
        if n_t == 1 and r0 >= t_valid:
            o_state.append(jnp.zeros((H, c, dk), F32))
            v_new.append(jnp.zeros((H, c, dk), F32))
            continue
        gl = gcol[:, r0 + c - 1:r0 + c, :]
        kd = k[:, r0:r0 + c] * jnp.exp(gl - gcol[:, r0:r0 + c])
        xs = _bmm(jnp.concatenate([w[:, r0:r0 + c], qd[:, r0:r0 + c]], axis=1), S)
        vn = u[:, r0:r0 + c] - xs[:, :c]
        o_state.append(xs[:, c:])
        v_new.append(vn)
        S = S * jnp.exp(gl) + _bmm_tn(kd, vn)
    s_scr[...] = S
    o = jnp.concatenate(o_state, axis=1) + _bmm(attn, jnp.concatenate(v_new, axis=1))
    o = o * lax.rsqrt(jnp.mean(o * o, axis=-1, keepdims=True) + EPS) * nw_ref[...]
    z = z_ref[0]
    for h in range(H):
        go_ref[0, :, h * dk:(h + 1) * dk] = o[h] * _silu(z[:, h * dk:(h + 1) * dk])

    @pl.when(t == n_t - 1)
    def _():
        so_ref[0] = S
        e0 = base - halo + (t_valid - (n_t - 1) * ct)
        for j in range(n_col):
            nc_ref[0, :, col(j)] = xbuf[j, e0:e0 + halo, :]


def _gdn(qkv_pre, ab, z, conv0, s0, conv_w, gp, gnw, *, ct, t_valid):
    G, T, W3 = qkv_pre.shape
    H, dk = GDN_HEADS, GDN_DK
    n_t = T // ct
    halo = GDN_CONV - 1
    tile = lambda n: pl.BlockSpec((1, ct, n), lambda b, t: (b, t, 0))
    per_seq = lambda shape: pl.BlockSpec((1,) + shape, lambda b, t: (b,) + (0,) * len(shape))
    whole = lambda shape: pl.BlockSpec(shape, lambda b, t: (0,) * len(shape))
    return pl.pallas_call(
        functools.partial(_gdn_kernel, ct=ct, c=GDN_CHUNK, t_valid=t_valid, n_t=n_t),
        grid=(G, n_t),
        in_specs=[tile(W3), per_seq((halo, W3)), whole((GDN_CONV, W3)), tile(LANES), tile(H * dk),
                  per_seq((H, dk, dk)), whole((8, LANES)), whole((1, dk))],
        out_specs=[tile(H * dk), per_seq((H, dk, dk)), per_seq((halo, W3))],
        out_shape=[jax.ShapeDtypeStruct((G, T, H * dk), F32), jax.ShapeDtypeStruct((G, H, dk, dk), F32),
                   jax.ShapeDtypeStruct((G, halo, W3), F32)],
        scratch_shapes=[pltpu.VMEM((W3 // dk, ct + 8, dk), F32), pltpu.VMEM((H, dk, dk), F32)],
        compiler_params=pltpu.CompilerParams(dimension_semantics=("parallel", "arbitrary"),
                                             vmem_limit_bytes=VMEM_LIMIT),
        name="gdn",
    )(qkv_pre, conv0, conv_w, ab, z, s0, gp, gnw)


def _alibi_slope(head_f):
    return jnp.exp2(-8.0 * (head_f + 1.0) / MOBA_HEADS)


def _moba_prompt_kernel(q_ref, kt_ref, vt_ref, o_ref, ka_ref, kb_ref, vb_ref, qa_ref, *, nb, blk):
    p = pl.program_id(1)
    hd = MOBA_HD
    T = nb * blk
    nr = 16
    nt = functools.partial(lax.dot_general, dimension_numbers=(((1,), (1,)), ((), ())),
                           preferred_element_type=F32)

    kt = kt_ref[0]
    rowk = lax.broadcasted_iota(jnp.int32, (2 * hd, T), 0)
    colk = lax.broadcasted_iota(jnp.int32, (2 * hd, T), 1)
    blk_of_col = colk // blk

    def key_rows(r):
        pos = jnp.where(r == nr, blk_of_col, jnp.where(r == nr + 1, colk % blk, 0)).astype(F32)
        return jnp.where(r == blk_of_col, 1.0, pos)

    ka_ref[...] = jnp.where(rowk < hd, kt, key_rows(rowk - hd)).astype(BF16)
    kb_ref[...] = jnp.where(rowk >= hd, kt, key_rows(rowk)).astype(BF16)
    vb_ref[...] = vt_ref[0].astype(BF16)
    rown = lax.broadcasted_iota(jnp.int32, (nr, T), 0)
    coln = lax.broadcasted_iota(jnp.int32, (nr, T), 1) // blk
    avg = jnp.where(rown == coln, 1.0 / blk, 0.0).astype(BF16)
    k1 = kt.astype(BF16)
    r1 = kt - k1.astype(F32)
    k2 = r1.astype(BF16)
    k3 = (r1 - k2.astype(F32)).astype(BF16)
    kmt = nt(avg, k1) + (nt(avg, k2) + nt(avg, k3))
    kmt_hi, kmt_lo = _split(kmt)

    q = q_ref[0]
    lane = lax.broadcasted_iota(jnp.int32, (T, LANES), 1)
    pf = jnp.full((1, 1), p, jnp.int32).astype(F32)
    e_row = lax.broadcasted_iota(jnp.int32, (nr, LANES), 0)
    e_lane = lax.broadcasted_iota(jnp.int32, (nr, LANES), 1)
    slope = []
    for hh in range(2):
        in_head = (lane >= hh * hd) & (lane < (hh + 1) * hd)
        qx = jnp.where(in_head, q, 0.0)
        slope.append(_alibi_slope(2.0 * pf + hh))
        q_hi, q_lo = _split(qx)
        gt = nt(kmt_hi, q_hi) + (nt(kmt_lo, q_hi) + nt(kmt_hi, q_lo))
        gt = jnp.where(rown < coln, gt, -jnp.inf)
        sel = jnp.where(rown == coln, 1.0, 0.0)
        for n in range(nb):
            gn = gt[n:n + 1, :]
            ahead = jnp.where((gt > gn) | ((gt == gn) & (rown < n)), 1.0, 0.0)
            cnt = jnp.sum(ahead, axis=0, keepdims=True)
            sel = jnp.where((rown == n) & (cnt < MOBA_TOPK) & (gn > -jnp.inf), 1.0, sel)
        off = hd if hh == 0 else 0
        place = jnp.where((e_lane == e_row + off) & (e_row < nb), 1.0, 0.0).astype(BF16)
        placed = lax.dot_general(sel.astype(BF16), place, (((0,), (0,)), ((), ())),
                                 preferred_element_type=F32)
        bias = jnp.where(lane == off + nr, slope[hh] * blk,
                         jnp.where(lane == off + nr + 1, slope[hh], jnp.where(placed > 0.5, 0.0, NEG)))
        qa_ref[hh] = jnp.where(in_head, q, bias).astype(BF16)

    lane_blk = lax.broadcasted_iota(jnp.int32, (blk, LANES), 1)
    ri = lax.broadcasted_iota(jnp.int32, (blk, blk), 0)
    ci = lax.broadcasted_iota(jnp.int32, (blk, blk), 1)
    k_refs = (ka_ref, kb_ref)
    for j in range(nb):
        lo, hi = j * blk, (j + 1) * blk
        outs = []
        for hh in range(2):
            qj = qa_ref[hh, lo:hi, :]
            s_own = jnp.where(ri >= ci, jnp.dot(qj, k_refs[hh][:, lo:hi], preferred_element_type=F32), NEG)
            m = jnp.max(s_own, axis=1, keepdims=True)
            if j > 0:
                s_past = jnp.dot(qj, k_refs[hh][:, :lo], preferred_element_type=F32)
                m = jnp.maximum(m, jnp.max(s_past, axis=1, keepdims=True))
            pr = jnp.exp(s_own - m)
            den = jnp.sum(pr, axis=1, keepdims=True)
            acc = nt(pr.astype(BF16), vb_ref[:, lo:hi])
            if j > 0:
                pr = jnp.exp(s_past - m)
                den = den + jnp.sum(pr, axis=1, keepdims=True)
                acc = acc + nt(pr.astype(BF16), vb_ref[:, :lo])
            outs.append(acc / den)
        o_ref[0, lo:hi, :] = jnp.where(lane_blk < hd, outs[0], outs[1])


def _moba_prompt(mq, kt, vt):
    B, T, W = mq.shape
    blk = MOBA_BLOCK
    nb = T // blk
    assert nb <= 16
    qspec = pl.BlockSpec((1, T, LANES), lambda b, p: (b, 0, p))
    kspec = pl.BlockSpec((1, LANES, T), lambda b, p: (b, p, 0))
    return pl.pallas_call(
        functools.partial(_moba_prompt_kernel, nb=nb, blk=blk),
        grid=(B, W // LANES),
        in_specs=[qspec, kspec, kspec],
        out_specs=qspec,
        out_shape=jax.ShapeDtypeStruct((B, T, W), F32),
        scratch_shapes=[pltpu.VMEM((LANES, T), BF16)] * 3 + [pltpu.VMEM((2, T, LANES), BF16)],
        compiler_params=pltpu.CompilerParams(dimension_semantics=("parallel", "parallel"),
                                             vmem_limit_bytes=VMEM_LIMIT),
        name="moba_prompt",
    )(mq, kt, vt)


def _kmean_topk_kernel(pt_ref, *refs, n_pg, n_steps, ppb, nblk):
    pages = refs[:n_pg]
    q_ref, idx_ref, km_ref = refs[n_pg:]
    i = pl.program_id(1)
    W = MOBA_HEADS * MOBA_HD
    lane1 = lax.broadcasted_iota(jnp.int32, (1, LANES), 1)

    @pl.when(i == 0)
    def _():
        km_ref[...] = jnp.zeros((W, LANES), F32)

    km = km_ref[...]
    for bb in range(n_pg // ppb):
        tot = pages[bb * ppb][...]
        for r in range(1, ppb):
            tot = tot + pages[bb * ppb + r][...]
        col = jnp.sum(tot.reshape(W, PAGE_SIZE), axis=1, keepdims=True) * (1.0 / (ppb * PAGE_SIZE))
        km = jnp.where(lane1 == i * (n_pg // ppb) + bb, col, km)
    km_ref[...] = km

    @pl.when(i == n_steps - 1)
    def _():
        rows = q_ref.shape[1]
        lanef = lax.broadcasted_iota(jnp.int32, (rows, LANES), 1).astype(F32)
        for h in range(MOBA_HEADS):
            qh = q_ref[0, :, h * MOBA_HD:(h + 1) * MOBA_HD]
            gate = _dot3(qh, km[h * MOBA_HD:(h + 1) * MOBA_HD, :])
            gate = jnp.where(lanef < nblk, gate, -jnp.inf)
            out = jnp.zeros((rows, LANES), F32)
            for r in range(MOBA_TOPK):
                mx = jnp.max(gate, axis=1, keepdims=True)
                idx = jnp.min(jnp.where(gate == mx, lanef, 1e9), axis=1, keepdims=True)
                out = jnp.where(lanef == r, idx, out)
                gate = jnp.where(lanef == idx, -jnp.inf, gate)
            idx_ref[0, h] = out.astype(jnp.int32)


def _kmean_topk(page_table, cache_kt, mq_pad, *, n_pg=32):
    DB, n_pages = page_table.shape
    ppb = MOBA_BLOCK // PAGE_SIZE
    nblk = n_pages // ppb
    n_pg = min(n_pg, n_pages)
    assert n_pages % n_pg == 0 and n_pg % ppb == 0 and nblk <= LANES
    n_steps = n_pages // n_pg
    rows = mq_pad.shape[1]
    W = MOBA_HEADS * MOBA_HD

    def page_spec(r):
        return pl.BlockSpec((None, MOBA_HEADS, MOBA_HD, PAGE_SIZE),
                            lambda b, i, pt: (pt[b * n_pages + i * n_pg + r], 0, 0, 0))

    grid_spec = pltpu.PrefetchScalarGridSpec(
        num_scalar_prefetch=1,
        grid=(DB, n_steps),
        in_specs=[page_spec(r) for r in range(n_pg)] + [pl.BlockSpec((1, rows, W), lambda b, i, pt: (b, 0, 0))],
        out_specs=pl.BlockSpec((1, MOBA_HEADS, rows, LANES), lambda b, i, pt: (b, 0, 0, 0)),
        scratch_shapes=[pltpu.VMEM((W, LANES), F32)],
    )
    return pl.pallas_call(
        functools.partial(_kmean_topk_kernel, n_pg=n_pg, n_steps=n_steps, ppb=ppb, nblk=nblk),
        grid_spec=grid_spec,
        out_shape=jax.ShapeDtypeStruct((DB, MOBA_HEADS, rows, LANES), jnp.int32),
        compiler_params=pltpu.CompilerParams(dimension_semantics=("parallel", "arbitrary"),
                                             vmem_limit_bytes=VMEM_LIMIT),
        name="kmean_topk",
    )(page_table.reshape(-1), *([cache_kt] * n_pg), mq_pad)


def _decode_attn_kernel(phys_ref, idx_ref, qt_ref, kown_ref, vown_ref, ck_hbm, cv_hbm, o_ref,
                        kbuf, vbuf, sem, *, ds, past, n_slab):
    b = pl.program_id(0)
    nb = pl.num_programs(0)
    hd = MOBA_HD
    ppb = MOBA_BLOCK // PAGE_SIZE

    def copies(bi, slot, u):
        slab = phys_ref[bi * n_slab + u]
        return (pltpu.make_async_copy(ck_hbm.at[slab], kbuf.at[slot, u], sem.at[slot, 0]),
                pltpu.make_async_copy(cv_hbm.at[slab], vbuf.at[slot, u], sem.at[slot, 1]))

    def start_all(bi, slot):
        def body(u, carry):
            ck, cv = copies(bi, slot, u)
            ck.start()
            cv.start()
            return carry
        lax.fori_loop(0, n_slab, body, 0, unroll=8)

    def wait_all(bi, slot):
        def body(u, carry):
            ck, cv = copies(bi, slot, u)
            ck.wait()
            cv.wait()
            return carry
        lax.fori_loop(0, n_slab, body, 0, unroll=8)

    slot = b % 2

    @pl.when(b == 0)
    def _():
        start_all(b, slot)

    @pl.when(b + 1 < nb)
    def _():
        start_all(b + 1, 1 - slot)

    wait_all(b, slot)

    lane = lax.broadcasted_iota(jnp.int32, (1, LANES), 1)
    lanef = lane.astype(F32)

    def head_body(h, carry):
        r0 = pl.multiple_of(h * hd, hd)
        slope = _alibi_slope(jnp.full((1, 1), h, jnp.int32).astype(F32))
        kown = kown_ref[0, pl.ds(r0, hd), :]
        vown = vown_ref[0, pl.ds(r0, hd), :]
        qt = qt_ref[0, pl.ds(r0, hd), :]
        otile = jnp.zeros((hd, LANES), F32)
        for t in range(ds):
            qb = jnp.broadcast_to(qt[:, t:t + 1], (hd, LANES))
            s_own = jnp.sum(qb * kown, axis=0, keepdims=True)
            s_own = jnp.where(lane <= t, s_own - slope * (t - lanef), NEG)
            scores = [s_own]
            for s in range(MOBA_TOPK):
                blk_id = idx_ref[(b * MOBA_HEADS + h) * (ds * MOBA_TOPK) + t * MOBA_TOPK + s]
                for r in range(ppb):
                    u = ((h * ds + t) * MOBA_TOPK + s) * ppb + r
                    sc = jnp.sum(qb * kbuf[slot, u], axis=0, keepdims=True)
                    start = (blk_id * MOBA_BLOCK + r * PAGE_SIZE).astype(F32)
                    scores.append(sc - slope * ((past + t) - start - lanef))
            mx = scores[0]
            for sc in scores[1:]:
                mx = jnp.maximum(mx, sc)
            mx = jnp.max(mx, axis=1, keepdims=True)
            pr = jnp.exp(scores[0] - mx)
            den = pr
            acc = pr * vown
            i = 1
            for s in range(MOBA_TOPK):
                for r in range(ppb):
                    u = ((h * ds + t) * MOBA_TOPK + s) * ppb + r
                    pr = jnp.exp(scores[i] - mx)
                    den = den + pr
                    acc = acc + pr * vbuf[slot, u]
                    i += 1
            o_col = jnp.sum(acc, axis=1, keepdims=True) / jnp.sum(den, axis=1, keepdims=True)
            otile = jnp.where(lane == t, o_col, otile)
        o_ref[0, pl.ds(r0, hd), :] = otile
        return carry

    lax.fori_loop(0, MOBA_HEADS, head_body, 0)


def _decode_attn(phys, idx, qt, kown_t, vown_t, cache_kt, cache_vt, *, ds, past):
    DB, W, _ = qt.shape
    n_slab = MOBA_HEADS * ds * MOBA_TOPK * (MOBA_BLOCK // PAGE_SIZE)
    tile = pl.BlockSpec((1, W, LANES), lambda b, ph, ix: (b, 0, 0))
    grid_spec = pltpu.PrefetchScalarGridSpec(
        num_scalar_prefetch=2,
        grid=(DB,),
        in_specs=[tile, tile, tile, pl.BlockSpec(memory_space=pl.ANY), pl.BlockSpec(memory_space=pl.ANY)],
        out_specs=tile,
        scratch_shapes=[pltpu.VMEM((2, n_slab, MOBA_HD, PAGE_SIZE), F32),
                        pltpu.VMEM((2, n_slab, MOBA_HD, PAGE_SIZE), F32),
                        pltpu.SemaphoreType.DMA((2, 2))],
    )
    return pl.pallas_call(
        functools.partial(_decode_attn_kernel, ds=ds, past=past, n_slab=n_slab),
        grid_spec=grid_spec,
        out_shape=jax.ShapeDtypeStruct((DB, W, LANES), F32),
        compiler_params=pltpu.CompilerParams(dimension_semantics=("arbitrary",),
                                             vmem_limit_bytes=VMEM_LIMIT),
        name="decode_attn",
    )(phys, idx, qt, kown_t, vown_t, cache_kt, cache_vt)


def _tokens_on_lanes(a, db, ds):
    w = a.shape[0]
    a = a.reshape(w, db, ds).transpose(1, 0, 2)
    return jnp.pad(a, ((0, 0), (0, 0), (0, LANES - ds)))


def _layer_weights(w_in, conv_w, a_log, dt_bias, ffn_up, ffn_down, w_out):
    hq = GDN_HEADS * GDN_DK
    n_qkv = 3 * hq
    o_a = n_qkv
    o_z = o_a + 2 * GDN_HEADS
    o_m = o_z + hq
    n_m = MOBA_HEADS * MOBA_HD
    wa = jnp.concatenate([w_in[:, :n_qkv], w_in[:, o_z:o_m + n_m]], axis=1).astype(BF16)
    wkv_t = w_in[:, o_m + n_m:].T.astype(BF16)
    wab = jnp.pad(w_in[:, o_a:o_z], ((0, 0), (0, LANES - 2 * GDN_HEADS)))
    wab_hi = wab.astype(BF16)
    wab_lo = (wab - wab_hi.astype(F32)).astype(BF16)
    gp = jnp.zeros((8, LANES), F32).at[0, :GDN_HEADS].set(a_log).at[1, :GDN_HEADS].set(dt_bias)
    return dict(wa=wa, wkv_t=wkv_t, wab=jnp.concatenate([wab_hi, wab_lo], axis=1), gp=gp, conv_w=conv_w,
                up=[ffn_up[i].astype(BF16) for i in range(2)],
                down=[ffn_down[i].astype(BF16) for i in range(2)],
                wo=w_out.astype(BF16), n_qkv=n_qkv, n_z=hq, n_m=n_m)


def _trunk(x, mods, lw, norm_w, gdn_norm_w, conv0, s0, attend, *, tm, ct, t_valid, final_w):
    sh1, sc1, g1, sh2, sc2, g2, sh3, sc3, g3 = mods
    h = _ffn(x, sh1, sc1, g1, norm_w[0], lw["up"][0], lw["down"][0], tm=tm)
    qkv_pre, z, mq, kt, vt, ab = _win(h, sh2, sc2, norm_w[1], lw["wa"], lw["wkv_t"], lw["wab"],
                                      tm=tm, n_qkv=lw["n_qkv"], n_z=lw["n_z"], n_m=lw["n_m"])
    go, s_new, conv_new = _gdn(*attend["gdn_in"](qkv_pre, ab, z), conv0, s0, lw["conv_w"], lw["gp"],
                               gdn_norm_w.reshape(1, -1), ct=ct, t_valid=t_valid)
    go = attend["gdn_out"](go)
    mo = attend["moba"](mq, kt, vt)
    h = _ffn(h, sh3, sc3, g3, norm_w[2], lw["up"][1], lw["down"][1], tm=tm, mix=(go, mo, g2, lw["wo"]),
             final_w=final_w)
    return h, s_new, conv_new, kt, vt


def kernel(x_prompt, x_sample, c_prompt, c_sample, cache_k, cache_v, page_table, state_gdn, state_conv,
           w_ada, b_ada, norm_w, ffn_up, ffn_down, w_in, conv_w, a_log, dt_bias, gdn_norm_w, w_out,
           final_norm_w):
    B, T, D = x_prompt.shape
    DB, DS, _ = x_sample.shape
    depth = w_ada.shape[0]
    past = page_table.shape[1] * PAGE_SIZE
    ppb = MOBA_BLOCK // PAGE_SIZE
    assert past % MOBA_BLOCK == 0 and past // MOBA_BLOCK >= MOBA_TOPK and DS <= PAGE_SIZE
    assert T % MOBA_BLOCK == 0 and (T - 1) // MOBA_BLOCK >= MOBA_TOPK
    NS = DB * DS
    hq = GDN_HEADS * GDN_DK
    n_m = MOBA_HEADS * MOBA_HD
    ct_s = GDN_CHUNK

    hp = x_prompt
    hs = x_sample.reshape(1, NS, D)
    outs = [[] for _ in range(8)]
    for l in range(depth):
        lw = _layer_weights(w_in[l], conv_w[l], a_log[l], dt_bias[l], ffn_up[l], ffn_down[l], w_out[l])
        mod = _ada_mod(jnp.concatenate([c_prompt, c_sample], axis=0), w_ada[l], b_ada[l])
        mod_p = mod[:B].reshape(B, N_MOD, 1, D)
        mod_s = jnp.repeat(mod[B:].reshape(DB, N_MOD, D), DS, axis=0).reshape(1, NS, N_MOD, D)
        mods_p = [mod_p[:, i] for i in range(N_MOD)]
        mods_s = [mod_s[:, :, i] for i in range(N_MOD)]
        last = l == depth - 1
        fw = final_norm_w if last else None

        attend_p = dict(gdn_in=lambda qkv, ab, z: (qkv, ab, z), gdn_out=lambda go: go, moba=_moba_prompt)
        hp, s_p, c_p, kt_p, vt_p = _trunk(
            hp, mods_p, lw, norm_w[l], gdn_norm_w[l], jnp.zeros((B, GDN_CONV - 1, 3 * hq), F32),
            jnp.zeros((B, GDN_HEADS, GDN_DK, GDN_DK), F32), attend_p, tm=512, ct=256, t_valid=T, final_w=fw)

        cache_kt = jnp.transpose(cache_k[l], (0, 2, 3, 1))
        cache_vt = jnp.transpose(cache_v[l], (0, 2, 3, 1))
        page_tbl = page_table

        def pad_seq(a):
            return jnp.pad(a.reshape(DB, DS, a.shape[-1]), ((0, 0), (0, ct_s - DS), (0, 0)))

        def moba_s(mq, kt, vt, cache_kt=cache_kt, cache_vt=cache_vt, page_tbl=page_tbl):
            mq_pad = jnp.pad(mq.reshape(DB, DS, n_m), ((0, 0), (0, 16 - DS), (0, 0)))
            idx = _kmean_topk(page_tbl, cache_kt, mq_pad)[:, :, :DS, :MOBA_TOPK]
            lpage = idx[..., None] * ppb + jnp.arange(ppb, dtype=jnp.int32)
            phys = jnp.take_along_axis(page_tbl, lpage.reshape(DB, -1), axis=1).reshape(lpage.shape)
            slab = phys * MOBA_HEADS + jnp.arange(MOBA_HEADS, dtype=jnp.int32)[None, :, None, None, None]
            qt = _tokens_on_lanes(mq[0].T, DB, DS)
            slabs = lambda c: c.reshape((-1,) + c.shape[2:])
            ot = _decode_attn(slab.reshape(-1), idx.reshape(-1), qt, _tokens_on_lanes(kt[0], DB, DS),
                              _tokens_on_lanes(vt[0], DB, DS), slabs(cache_kt), slabs(cache_vt),
                              ds=DS, past=past)
            return ot[:, :, :DS].transpose(0, 2, 1).reshape(1, NS, n_m)

        attend_s = dict(gdn_in=lambda qkv, ab, z: (pad_seq(qkv), pad_seq(ab), pad_seq(z)),
                        gdn_out=lambda go: go[:, :DS].reshape(1, NS, hq), moba=moba_s)
        hs, s_s, c_s, kt_s, vt_s = _trunk(
            hs, mods_s, lw, norm_w[l], gdn_norm_w[l], state_conv[l], state_gdn[l], attend_s,
            tm=NS, ct=ct_s, t_valid=DS, final_w=fw)

        def heads_last(a_t, b, t):
            return a_t.reshape(b, MOBA_HEADS, MOBA_HD, t).transpose(0, 3, 1, 2)

        new = [heads_last(kt_p, B, T), heads_last(vt_p, B, T), s_p, c_p,
               heads_last(kt_s.reshape(n_m, DB, DS).transpose(1, 0, 2), DB, DS),
               heads_last(vt_s.reshape(n_m, DB, DS).transpose(1, 0, 2), DB, DS), s_s, c_s]
        for o, v in zip(outs, new):
            o.append(v)
    y_prompt = hp
    y_sample = hs.reshape(DB, DS, D)
    return (y_prompt, y_sample) + tuple(jnp.stack(o) for o in outs)
```

```python
import functools

import jax
import jax.numpy as jnp
from jax import lax
from jax.experimental import pallas as pl
from jax.experimental.pallas import tpu as pltpu

F32 = jnp.float32
BF16 = jnp.bfloat16
EPS = 1e-6
NEG = -1e30
LOG2E = 1.4426950408889634

GDN_HEADS = 4
GDN_DK = 128
GDN_CONV = 4
GDN_CHUNK = 64
MOBA_HEADS = 8
MOBA_HD = 64
MOBA_BLOCK = 256
MOBA_TOPK = 3
PAGE_SIZE = 128
N_MOD = 9
LANES = 128
VMEM_LIMIT = 56 * 1024 * 1024


def _sigmoid(x):
    return 1.0 / (1.0 + jnp.exp(-x))


def _silu(x):
    return x * _sigmoid(x)


def _bdot(a, b):
    return jnp.dot(a.astype(BF16), b.astype(BF16), preferred_element_type=F32)


def _bdot_nt(a, b):
    return lax.dot_general(a.astype(BF16), b.astype(BF16), (((1,), (1,)), ((), ())),
                           preferred_element_type=F32)


def _split(x):
    hi = x.astype(BF16)
    lo = (x - hi.astype(F32)).astype(BF16)
    return hi, lo


def _dot3(a, b):
    a_hi, a_lo = _split(a)
    b_hi, b_lo = _split(b)
    d = functools.partial(jnp.dot, preferred_element_type=F32)
    return d(a_hi, b_hi) + (d(a_lo, b_hi) + d(a_hi, b_lo))


def _norm_mod(x, nw, sc, sh):
    y = x * lax.rsqrt(jnp.mean(x * x, axis=-1, keepdims=True) + EPS)
    return (y * nw) * (1.0 + sc) + sh


def _mod_spec(rows, tm, d):
    if rows == 1:
        return pl.BlockSpec((1, 1, d), lambda g, t: (g, 0, 0))
    return pl.BlockSpec((1, tm, d), lambda g, t: (g, t, 0))


def _resident(shape):
    nd = len(shape)
    return pl.BlockSpec(shape, lambda *_: (0,) * nd, pipeline_mode=pl.Buffered(1))


def _ada_kernel(c_ref, w_ref, b_ref, o_ref):
    o_ref[...] = _dot3(_silu(c_ref[...]), w_ref[...]) + b_ref[...]


def _ada_mod(c_all, w, b):
    n, d = c_all.shape
    dout = w.shape[1]
    tn = d
    return pl.pallas_call(
        _ada_kernel,
        grid=(dout // tn,),
        in_specs=[pl.BlockSpec((n, d), lambda i: (0, 0)),
                  pl.BlockSpec((d, tn), lambda i: (0, i)),
                  pl.BlockSpec((1, tn), lambda i: (0, i))],
        out_specs=pl.BlockSpec((n, tn), lambda i: (0, i)),
        out_shape=jax.ShapeDtypeStruct((n, dout), F32),
        compiler_params=pltpu.CompilerParams(dimension_semantics=("arbitrary",),
                                             vmem_limit_bytes=VMEM_LIMIT),
        name="ada_mod",
    )(c_all, w, b.reshape(1, dout))


def _ffn_kernel(*refs, tf, sub, has_mix, has_final):
    it = iter(refs)
    x_ref, sh_ref, sc_ref, g_ref, nw_ref, wup_ref, wdn_ref = [next(it) for _ in range(7)]
    if has_mix:
        go_ref, mo_ref, g2_ref, wo_ref = [next(it) for _ in range(4)]
    if has_final:
        fw_ref = next(it)
    o_ref = next(it)
    a_scr = next(it)

    tm = x_ref.shape[1]
    d_ff = wdn_ref.shape[0]

    def rows(ref, r0):
        return ref[0] if ref.shape[1] == 1 else ref[0, r0:r0 + sub, :]

    for r0 in range(0, tm, sub):
        x = x_ref[0, r0:r0 + sub, :]
        if has_mix:
            half = go_ref.shape[-1]
            mix = (_bdot(go_ref[0, r0:r0 + sub, :], wo_ref[0:half, :])
                   + _bdot(mo_ref[0, r0:r0 + sub, :], wo_ref[half:, :]))
            x = x + rows(g2_ref, r0) * mix
        hb = _norm_mod(x, nw_ref[...], rows(sc_ref, r0), rows(sh_ref, r0)).astype(BF16)
        for c in range(d_ff // tf):
            gate = jnp.dot(hb, wup_ref[:, c * tf:(c + 1) * tf], preferred_element_type=F32)
            up = jnp.dot(hb, wup_ref[:, d_ff + c * tf:d_ff + (c + 1) * tf], preferred_element_type=F32)
            a_scr[r0:r0 + sub, c * tf:(c + 1) * tf] = (_silu(gate) * up).astype(BF16)
        down = jnp.dot(a_scr[r0:r0 + sub, :], wdn_ref[...], preferred_element_type=F32)
        out = x + (0.5 * rows(g_ref, r0)) * down
        if has_final:
            out = out * lax.rsqrt(jnp.mean(out * out, axis=-1, keepdims=True) + EPS) * fw_ref[...]
        o_ref[0, r0:r0 + sub, :] = out


def _ffn(x, sh, sc, g, nw, wup, wdn, *, tm, sub=None, mix=None, final_w=None):
    G, T, D = x.shape
    d_ff = wdn.shape[0]
    mxu_n = 256
    tf = d_ff // 2 if (d_ff // 2) % mxu_n == 0 else d_ff
    row = pl.BlockSpec((1, tm, D), lambda g_, t: (g_, t, 0))
    ins = [x, sh, sc, g, nw.reshape(1, D), wup, wdn]
    specs = [row, _mod_spec(sh.shape[1], tm, D), _mod_spec(sc.shape[1], tm, D),
             _mod_spec(g.shape[1], tm, D), _resident((1, D)), _resident(wup.shape), _resident(wdn.shape)]
    if mix is not None:
        go, mo, g2, wo = mix
        half = go.shape[-1]
        ins += [go, mo, g2, wo]
        specs += [pl.BlockSpec((1, tm, half), lambda g_, t: (g_, t, 0)),
                  pl.BlockSpec((1, tm, half), lambda g_, t: (g_, t, 0)),
                  _mod_spec(g2.shape[1], tm, D), _resident(wo.shape)]
    if final_w is not None:
        ins.append(final_w.reshape(1, D))
        specs.append(_resident((1, D)))
    return pl.pallas_call(
        functools.partial(_ffn_kernel, tf=tf, sub=sub or tm, has_mix=mix is not None,
                          has_final=final_w is not None),
        grid=(G, T // tm),
        in_specs=specs,
        out_specs=row,
        out_shape=jax.ShapeDtypeStruct((G, T, D), F32),
        scratch_shapes=[pltpu.VMEM((tm, d_ff), BF16)],
        compiler_params=pltpu.CompilerParams(dimension_semantics=("parallel", "parallel"),
                                             vmem_limit_bytes=VMEM_LIMIT),
        name="ffn_mix" if mix is not None else "ffn",
    )(*ins)


def _win_kernel(h_ref, sh_ref, sc_ref, nw_ref, wa_ref, wkv_ref, wab_ref,
                qkv_ref, z_ref, mq_ref, kt_ref, vt_ref, ab_ref, *, n_qkv, n_z, n_m):
    hn = _norm_mod(h_ref[0], nw_ref[...], sc_ref[0], sh_ref[0])
    hb = hn.astype(BF16)
    h_lo = (hn - hb.astype(F32)).astype(BF16)
    d = functools.partial(jnp.dot, preferred_element_type=F32)
    qkv_ref[0] = d(hb, wa_ref[:, 0:n_qkv])
    z_ref[0] = d(hb, wa_ref[:, n_qkv:n_qkv + n_z])
    mq_ref[0] = d(hb, wa_ref[:, n_qkv + n_z:n_qkv + n_z + n_m]) * (MOBA_HD ** -0.5)
    nt = functools.partial(lax.dot_general, dimension_numbers=(((1,), (1,)), ((), ())),
                           preferred_element_type=F32)
    kt_ref[0] = nt(wkv_ref[0:n_m, :], hb)
    vt_ref[0] = nt(wkv_ref[n_m:, :], hb)
    r_hi = d(hb, wab_ref[...])
    r_lo = d(h_lo, wab_ref[...])
    ab_ref[0] = (r_hi[:, :LANES] + r_lo[:, :LANES]) + (r_hi[:, LANES:] + r_lo[:, LANES:])


def _win(h, sh, sc, nw, wa, wkv_t, wab, *, tm, n_qkv, n_z, n_m):
    G, T, D = h.shape
    row = lambda n: pl.BlockSpec((1, tm, n), lambda g_, t: (g_, t, 0))
    col = pl.BlockSpec((1, n_m, tm), lambda g_, t: (g_, 0, t))
    return pl.pallas_call(
        functools.partial(_win_kernel, n_qkv=n_qkv, n_z=n_z, n_m=n_m),
        grid=(G, T // tm),
        in_specs=[row(D), _mod_spec(sh.shape[1], tm, D), _mod_spec(sc.shape[1], tm, D),
                  _resident((1, D)), _resident(wa.shape), _resident(wkv_t.shape),
                  _resident(wab.shape)],
        out_specs=[row(n_qkv), row(n_z), row(n_m), col, col, row(LANES)],
        out_shape=[jax.ShapeDtypeStruct((G, T, n_qkv), F32), jax.ShapeDtypeStruct((G, T, n_z), F32),
                   jax.ShapeDtypeStruct((G, T, n_m), F32), jax.ShapeDtypeStruct((G, n_m, T), F32),
                   jax.ShapeDtypeStruct((G, n_m, T), F32), jax.ShapeDtypeStruct((G, T, LANES), F32)],
        compiler_params=pltpu.CompilerParams(dimension_semantics=("parallel", "parallel"),
                                             vmem_limit_bytes=VMEM_LIMIT),
        name="w_in",
    )(h, sh, sc, nw.reshape(1, D), wa, wkv_t, wab)


TRI_BASE = 8


def _bmm(a, b):
    return lax.dot_general(a.astype(BF16), b.astype(BF16), (((2,), (1,)), ((0,), (0,))),
                           preferred_element_type=F32)


def _bmm_nt(a, b):
    return lax.dot_general(a.astype(BF16), b.astype(BF16), (((2,), (2,)), ((0,), (0,))),
                           preferred_element_type=F32)


def _bmm_tn(a, b):
    return lax.dot_general(a.astype(BF16), b.astype(BF16), (((1,), (1,)), ((0,), (0,))),
                           preferred_element_type=F32)


def _unit_lower_inverse(m, eye, ri, ci, c):
    size = TRI_BASE
    a = jnp.where(ri // size == ci // size, m, 0.0)
    t = eye - a
    p = a
    e = 2
    while e < size:
        p = _bmm(p, p)
        t = t + _bmm(t, p)
        e *= 2
    h, n, _ = m.shape
    while size < c:
        pairs = n // (2 * size)
        split = lambda x: x.reshape(h, pairs, 2, size, n)
        t_up, t_low = split(t)[:, :, 0], split(t)[:, :, 1].reshape(h, n // 2, n)
        pair_ok = (ri // (2 * size) == ci // (2 * size)) & (ri // size != ci // size)
        f_low = split(jnp.where(pair_ok, m, 0.0))[:, :, 1].reshape(h, n // 2, n)
        ft = _bmm(f_low, t).reshape(h, pairs, size, n)
        ft_full = jnp.stack([jnp.zeros_like(ft), ft], axis=2).reshape(h, n, n)
        t_low = t_low - _bmm(t_low, ft_full)
        t = jnp.stack([t_up, t_low.reshape(h, pairs, size, n)], axis=2).reshape(h, n, n)
        size *= 2
    return t


def _gdn_kernel(x_ref, c0_ref, cw_ref, ab_ref, z_ref, s0_ref, gp_ref, nw_ref, go_ref, so_ref, nc_ref,
                xbuf, s_scr, *, ct, c, t_valid, n_t):
    t = pl.program_id(1)
    H, dk = GDN_HEADS, GDN_DK
    halo = GDN_CONV - 1
    base = 8

    n_col = xbuf.shape[0]
    col = lambda j: slice(j * dk, (j + 1) * dk)

    @pl.when(t == 0)
    def _():
        for j in range(n_col):
            xbuf[j, base - halo:base, :] = c0_ref[0, :, col(j)]
        s_scr[...] = s0_ref[0]

    @pl.when(t > 0)
    def _():
        for j in range(n_col):
            xbuf[j, base - halo:base, :] = xbuf[j, ct + base - halo:ct + base, :]

    conv = []
    for j in range(n_col):
        xbuf[j, base:base + ct, :] = x_ref[0, :, col(j)]
        acc = cw_ref[0:1, col(j)] * xbuf[j, base - halo:base - halo + ct, :]
        for i in range(1, GDN_CONV):
            acc = acc + cw_ref[i:i + 1, col(j)] * xbuf[j, base - halo + i:base - halo + i + ct, :]
        conv.append(_silu(acc))
    q, k, v = jnp.stack(conv[0:H]), jnp.stack(conv[H:2 * H]), jnp.stack(conv[2 * H:3 * H])
    q = q * lax.rsqrt(jnp.sum(q * q, axis=-1, keepdims=True) + EPS) * (dk ** -0.5)
    k = k * lax.rsqrt(jnp.sum(k * k, axis=-1, keepdims=True) + EPS)

    ab = ab_ref[0]
    rowi = lax.broadcasted_iota(jnp.int32, (ct, LANES), 0)
    xg = ab + gp_ref[1:2, :]
    softplus = jnp.maximum(xg, 0.0) + jnp.log(1.0 + jnp.exp(-jnp.abs(xg)))
    g_all = -jnp.exp(gp_ref[0:1, :]) * softplus
    b_all = _sigmoid(ab)
    if t_valid % ct != 0:
        valid = (rowi + t * ct) < t_valid
        g_all = jnp.where(valid, g_all, 0.0)
        b_all = jnp.where(valid, b_all, 0.0)
    rin = rowi % c
    gc_all = g_all
    s = 1
    while s < c:
        gc_all = gc_all + jnp.where(rin >= s, pltpu.roll(gc_all, s, axis=0), 0.0)
        s *= 2
    gct = jnp.transpose(gc_all)
    gcol = jnp.stack([gc_all[:, h:h + 1] for h in range(H)])
    grow = jnp.stack([gct[h:h + 1, :] for h in range(H)])
    beta = jnp.stack([b_all[:, H + h:H + h + 1] for h in range(H)])

    ri = lax.broadcasted_iota(jnp.int32, (ct, ct), 0)
    ci = lax.broadcasted_iota(jnp.int32, (ct, ct), 1)
    same = ri // c == ci // c
    eye = jnp.where(ri == ci, 1.0, 0.0)
    decay = jnp.exp(jnp.where(same & (ri >= ci), gcol - grow, -jnp.inf))
    kb = k * beta
    m = jnp.where(same & (ri > ci), _bmm_nt(kb, k) * decay, 0.0)
    tm1 = _unit_lower_inverse(m, eye, ri, ci, c) - eye
    egc = jnp.exp(gcol)
    rhs = jnp.concatenate([v * beta, kb * egc], axis=-1)
    uw = rhs + _bmm(tm1, rhs)
    u, w = uw[..., :dk], uw[..., dk:]
    attn = _bmm_nt(q, k) * decay
    qd = q * egc

    S = s_scr[...]
    o_state, v_new = [], []
    for cc in range(ct // c):
        r0 = cc * c
        if n_t == 1 and r0 >= t_valid:
            o_state.append(jnp.zeros((H, c, dk), F32))
            v_new.append(jnp.zeros((H, c, dk), F32))
            continue
        gl = gcol[:, r0 + c - 1:r0 + c, :]
        kd = k[:, r0:r0 + c] * jnp.exp(gl - gcol[:, r0:r0 + c])
        xs = _bmm(jnp.concatenate([w[:, r0:r0 + c], qd[:, r0:r0 + c]], axis=1), S)
        vn = u[:, r0:r0 + c] - xs[:, :c]
        o_state.append(xs[:, c:])
        v_new.append(vn)
        S = S * jnp.exp(gl) + _bmm_tn(kd, vn)
    s_scr[...] = S
    o = jnp.concatenate(o_state, axis=1) + _bmm(attn, jnp.concatenate(v_new, axis=1))
    o = o * lax.rsqrt(jnp.mean(o * o, axis=-1, keepdims=True) + EPS) * nw_ref[...]
    z = z_ref[0]
    for h in range(H):
        go_ref[0, :, h * dk:(h + 1) * dk] = o[h] * _silu(z[:, h * dk:(h + 1) * dk])

    @pl.when(t == n_t - 1)
    def _():
        so_ref[0] = S
        e0 = base - halo + (t_valid - (n_t - 1) * ct)
        for j in range(n_col):
            nc_ref[0, :, col(j)] = xbuf[j, e0:e0 + halo, :]


def _gdn(qkv_pre, ab, z, conv0, s0, conv_w, gp, gnw, *, ct, t_valid):
    G, T, W3 = qkv_pre.shape
    H, dk = GDN_HEADS, GDN_DK
    n_t = T // ct
    halo = GDN_CONV - 1
    tile = lambda n: pl.BlockSpec((1, ct, n), lambda b, t: (b, t, 0))
    per_seq = lambda shape: pl.BlockSpec((1,) + shape, lambda b, t: (b,) + (0,) * len(shape))
    whole = lambda shape: pl.BlockSpec(shape, lambda b, t: (0,) * len(shape))
    return pl.pallas_call(
        functools.partial(_gdn_kernel, ct=ct, c=GDN_CHUNK, t_valid=t_valid, n_t=n_t),
        grid=(G, n_t),
        in_specs=[tile(W3), per_seq((halo, W3)), whole((GDN_CONV, W3)), tile(LANES), tile(H * dk),
                  per_seq((H, dk, dk)), whole((8, LANES)), whole((1, dk))],
        out_specs=[tile(H * dk), per_seq((H, dk, dk)), per_seq((halo, W3))],
        out_shape=[jax.ShapeDtypeStruct((G, T, H * dk), F32), jax.ShapeDtypeStruct((G, H, dk, dk), F32),
                   jax.ShapeDtypeStruct((G, halo, W3), F32)],
        scratch_shapes=[pltpu.VMEM((W3 // dk, ct + 8, dk), F32), pltpu.VMEM((H, dk, dk), F32)],
        compiler_params=pltpu.CompilerParams(dimension_semantics=("parallel", "arbitrary"),
                                             vmem_limit_bytes=VMEM_LIMIT),
        name="gdn",
    )(qkv_pre, conv0, conv_w, ab, z, s0, gp, gnw)


def _alibi_slope(head_f):
    return jnp.exp2(-8.0 * (head_f + 1.0) / MOBA_HEADS)


def _moba_prompt_kernel(q_ref, kt_ref, vt_ref, o_ref, ka_ref, kb_ref, vb_ref, qa_ref, *, nb, blk):
    p = pl.program_id(1)
    hd = MOBA_HD
    T = nb * blk
    nr = 16
    nt = functools.partial(lax.dot_general, dimension_numbers=(((1,), (1,)), ((), ())),
                           preferred_element_type=F32)

    kt = kt_ref[0]
    rowk = lax.broadcasted_iota(jnp.int32, (2 * hd, T), 0)
    colk = lax.broadcasted_iota(jnp.int32, (2 * hd, T), 1)
    blk_of_col = colk // blk

    def key_rows(r):
        pos = jnp.where((r == nr) | (r == nr + 1), blk_of_col,
                        jnp.where((r == nr + 2) | (r == nr + 3), colk % blk, 0)).astype(F32)
        return jnp.where(r == blk_of_col, 1.0, pos)

    ka_ref[...] = jnp.where(rowk < hd, kt, key_rows(rowk - hd)).astype(BF16)
    kb_ref[...] = jnp.where(rowk >= hd, kt, key_rows(rowk)).astype(BF16)
    vb_ref[...] = vt_ref[0].astype(BF16)
    rown = lax.broadcasted_iota(jnp.int32, (nr, T), 0)
    coln = lax.broadcasted_iota(jnp.int32, (nr, T), 1) // blk
    avg = jnp.where(rown == coln, 1.0 / blk, 0.0).astype(BF16)
    k1 = kt.astype(BF16)
    r1 = kt - k1.astype(F32)
    k2 = r1.astype(BF16)
    k3 = (r1 - k2.astype(F32)).astype(BF16)
    kmt = nt(avg, k1) + (nt(avg, k2) + nt(avg, k3))
    kmt_hi, kmt_lo = _split(kmt)

    q = q_ref[0]
    lane = lax.broadcasted_iota(jnp.int32, (T, LANES), 1)
    pf = jnp.full((1, 1), p, jnp.int32).astype(F32)
    e_row = lax.broadcasted_iota(jnp.int32, (nr, LANES), 0)
    e_lane = lax.broadcasted_iota(jnp.int32, (nr, LANES), 1)
    nrank = 8 if nb <= 8 else nr
    rrow = lax.broadcasted_iota(jnp.int32, (nrank, T), 0)
    rcol = lax.broadcasted_iota(jnp.int32, (nrank, T), 1) // blk
    slope = []
    for hh in range(2):
        in_head = (lane >= hh * hd) & (lane < (hh + 1) * hd)
        qx = jnp.where(in_head, q, 0.0)
        slope.append(_alibi_slope(2.0 * pf + hh))
        q_hi, q_lo = _split(qx)
        gt = nt(kmt_hi, q_hi) + (nt(kmt_lo, q_hi) + nt(kmt_hi, q_lo))
        gt = jnp.where(rown < coln, gt, -jnp.inf)[:nrank]
        sel = jnp.where(rrow == rcol, 1.0, 0.0)
        for n in range(nb):
            gn = gt[n:n + 1, :]
            ahead = jnp.where((gt > gn) | ((gt == gn) & (rrow < n)), 1.0, 0.0)
            cnt = jnp.sum(ahead, axis=0, keepdims=True)
            sel = jnp.where((rrow == n) & (cnt < MOBA_TOPK) & (gn > -jnp.inf), 1.0, sel)
        if nrank < nr:
            sel = jnp.concatenate([sel, jnp.zeros((nr - nrank, T), F32)], axis=0)
        off = hd if hh == 0 else 0
        place = jnp.where((e_lane == e_row + off) & (e_row < nb), 1.0, 0.0).astype(BF16)
        placed = lax.dot_general(sel.astype(BF16), place, (((0,), (0,)), ((), ())),
                                 preferred_element_type=F32)
        s2 = slope[hh] * LOG2E
        s2_hi = s2.astype(BF16).astype(F32)
        s2_lo = s2 - s2_hi
        bias = jnp.where(placed > 0.5, 0.0, NEG)
        for i, val in enumerate((s2_hi * blk, s2_lo * blk, s2_hi, s2_lo)):
            bias = jnp.where(lane == off + nr + i, val, bias)
        qa_ref[hh] = jnp.where(in_head, q * LOG2E, bias).astype(BF16)

    lane_blk = lax.broadcasted_iota(jnp.int32, (blk, LANES), 1)
    ri = lax.broadcasted_iota(jnp.int32, (blk, blk), 0)
    ci = lax.broadcasted_iota(jnp.int32, (blk, blk), 1)
    k_refs = (ka_ref, kb_ref)
    def scores(j, hh):
        lo, hi = j * blk, (j + 1) * blk
        qj = qa_ref[hh, lo:hi, :]
        s_own = jnp.dot(qj, k_refs[hh][:, lo:hi], preferred_element_type=F32)
        s_past = jnp.dot(qj, k_refs[hh][:, :lo], preferred_element_type=F32) if j > 0 else None
        return s_own, s_past

    def softmax(s_own, s_past):
        s_own = jnp.where(ri >= ci, s_own, NEG)
        m = jnp.max(s_own, axis=1, keepdims=True)
        if s_past is not None:
            m = jnp.maximum(m, jnp.max(s_past, axis=1, keepdims=True))
        p_own = jnp.exp2(s_own - m)
        den = jnp.sum(p_own, axis=1, keepdims=True)
        p_past = None
        if s_past is not None:
            p_past = jnp.exp2(s_past - m)
            den = den + jnp.sum(p_past, axis=1, keepdims=True)
            p_past = p_past.astype(BF16)
        return p_own.astype(BF16), p_past, den

    def values(j, p_own, p_past, den):
        lo, hi = j * blk, (j + 1) * blk
        acc = nt(p_own, vb_ref[:, lo:hi])
        if p_past is not None:
            acc = acc + nt(p_past, vb_ref[:, :lo])
        return acc / den

    sc = [scores(0, 0), scores(0, 1)]
    for j in range(nb):
        outs = []
        for hh in range(2):
            sc_next = scores(j + 1, hh) if j + 1 < nb else None
            outs.append(values(j, *softmax(*sc[hh])))
            sc[hh] = sc_next
        o_ref[0, j * blk:(j + 1) * blk, :] = jnp.where(lane_blk < hd, outs[0], outs[1])


def _moba_prompt(mq, kt, vt):
    B, T, W = mq.shape
    blk = MOBA_BLOCK
    nb = T // blk
    assert nb <= 16
    qspec = pl.BlockSpec((1, T, LANES), lambda b, p: (b, 0, p))
    kspec = pl.BlockSpec((1, LANES, T), lambda b, p: (b, p, 0))
    return pl.pallas_call(
        functools.partial(_moba_prompt_kernel, nb=nb, blk=blk),
        grid=(B, W // LANES),
        in_specs=[qspec, kspec, kspec],
        out_specs=qspec,
        out_shape=jax.ShapeDtypeStruct((B, T, W), F32),
        scratch_shapes=[pltpu.VMEM((LANES, T), BF16)] * 3 + [pltpu.VMEM((2, T, LANES), BF16)],
        compiler_params=pltpu.CompilerParams(dimension_semantics=("parallel", "parallel"),
                                             vmem_limit_bytes=VMEM_LIMIT),
        name="moba_prompt",
    )(mq, kt, vt)


def _kmean_topk_kernel(pt_ref, *refs, n_pg, n_steps, ppb, nblk):
    pages = refs[:n_pg]
    q_ref, idx_ref, km_ref = refs[n_pg:]
    i = pl.program_id(1)
    W = MOBA_HEADS * MOBA_HD
    lane1 = lax.broadcasted_iota(jnp.int32, (1, LANES), 1)

    @pl.when(i == 0)
    def _():
        km_ref[...] = jnp.zeros((W, LANES), F32)

    km = km_ref[...]
    for bb in range(n_pg // ppb):
        tot = pages[bb * ppb][...]
        for r in range(1, ppb):
            tot = tot + pages[bb * ppb + r][...]
        col = jnp.sum(tot.reshape(W, PAGE_SIZE), axis=1, keepdims=True) * (1.0 / (ppb * PAGE_SIZE))
        km = jnp.where(lane1 == i * (n_pg // ppb) + bb, col, km)
    km_ref[...] = km

    @pl.when(i == n_steps - 1)
    def _():
        rows = q_ref.shape[1]
        lanef = lax.broadcasted_iota(jnp.int32, (rows, LANES), 1).astype(F32)
        for h in range(MOBA_HEADS):
            qh = q_ref[0, :, h * MOBA_HD:(h + 1) * MOBA_HD]
            gate = _dot3(qh, km[h * MOBA_HD:(h + 1) * MOBA_HD, :])
            gate = jnp.where(lanef < nblk, gate, -jnp.inf)
            out = jnp.zeros((rows, LANES), F32)
            for r in range(MOBA_TOPK):
                mx = jnp.max(gate, axis=1, keepdims=True)
                idx = jnp.min(jnp.where(gate == mx, lanef, 1e9), axis=1, keepdims=True)
                out = jnp.where(lanef == r, idx, out)
                gate = jnp.where(lanef == idx, -jnp.inf, gate)
            idx_ref[0, h] = out.astype(jnp.int32)


def _kmean_topk(page_table, cache_kt, mq_pad, *, n_pg=32):
    DB, n_pages = page_table.shape
    ppb = MOBA_BLOCK // PAGE_SIZE
    nblk = n_pages // ppb
    n_pg = min(n_pg, n_pages)
    assert n_pages % n_pg == 0 and n_pg % ppb == 0 and nblk <= LANES
    n_steps = n_pages // n_pg
    rows = mq_pad.shape[1]
    W = MOBA_HEADS * MOBA_HD

    def page_spec(r):
        return pl.BlockSpec((None, MOBA_HEADS, MOBA_HD, PAGE_SIZE),
                            lambda b, i, pt: (pt[b * n_pages + i * n_pg + r], 0, 0, 0))

    grid_spec = pltpu.PrefetchScalarGridSpec(
        num_scalar_prefetch=1,
        grid=(DB, n_steps),
        in_specs=[page_spec(r) for r in range(n_pg)] + [pl.BlockSpec((1, rows, W), lambda b, i, pt: (b, 0, 0))],
        out_specs=pl.BlockSpec((1, MOBA_HEADS, rows, LANES), lambda b, i, pt: (b, 0, 0, 0)),
        scratch_shapes=[pltpu.VMEM((W, LANES), F32)],
    )
    return pl.pallas_call(
        functools.partial(_kmean_topk_kernel, n_pg=n_pg, n_steps=n_steps, ppb=ppb, nblk=nblk),
        grid_spec=grid_spec,
        out_shape=jax.ShapeDtypeStruct((DB, MOBA_HEADS, rows, LANES), jnp.int32),
        compiler_params=pltpu.CompilerParams(dimension_semantics=("parallel", "arbitrary"),
                                             vmem_limit_bytes=VMEM_LIMIT),
        name="kmean_topk",
    )(page_table.reshape(-1), *([cache_kt] * n_pg), mq_pad)


def _decode_attn_kernel(phys_ref, idx_ref, qt_ref, kown_ref, vown_ref, ck_hbm, cv_hbm, o_ref,
                        kbuf, vbuf, sem, *, ds, past, n_slab):
    b = pl.program_id(0)
    nb = pl.num_programs(0)
    hd = MOBA_HD
    ppb = MOBA_BLOCK // PAGE_SIZE

    def copies(bi, slot, u):
        slab = phys_ref[bi * n_slab + u]
        return (pltpu.make_async_copy(ck_hbm.at[slab], kbuf.at[slot, u], sem.at[slot, 0]),
                pltpu.make_async_copy(cv_hbm.at[slab], vbuf.at[slot, u], sem.at[slot, 1]))

    def start_all(bi, slot):
        def body(u, carry):
            ck, cv = copies(bi, slot, u)
            ck.start()
            cv.start()
            return carry
        lax.fori_loop(0, n_slab, body, 0, unroll=8)

    def wait_all(bi, slot):
        def body(u, carry):
            ck, cv = copies(bi, slot, u)
            ck.wait()
            cv.wait()
            return carry
        lax.fori_loop(0, n_slab, body, 0, unroll=8)

    slot = b % 2

    @pl.when(b == 0)
    def _():
        start_all(b, slot)

    @pl.when(b + 1 < nb)
    def _():
        start_all(b + 1, 1 - slot)

    wait_all(b, slot)

    lane = lax.broadcasted_iota(jnp.int32, (1, LANES), 1)
    lanef = lane.astype(F32)

    def head_body(h, carry):
        r0 = pl.multiple_of(h * hd, hd)
        slope = _alibi_slope(jnp.full((1, 1), h, jnp.int32).astype(F32))
        kown = kown_ref[0, pl.ds(r0, hd), :]
        vown = vown_ref[0, pl.ds(r0, hd), :]
        qt = qt_ref[0, pl.ds(r0, hd), :]
        otile = jnp.zeros((hd, LANES), F32)
        for t in range(ds):
            qb = jnp.broadcast_to(qt[:, t:t + 1], (hd, LANES))
            s_own = jnp.sum(qb * kown, axis=0, keepdims=True)
            s_own = jnp.where(lane <= t, s_own - slope * (t - lanef), NEG)
            scores = [s_own]
            for s in range(MOBA_TOPK):
                blk_id = idx_ref[(b * MOBA_HEADS + h) * (ds * MOBA_TOPK) + t * MOBA_TOPK + s]
                for r in range(ppb):
                    u = ((h * ds + t) * MOBA_TOPK + s) * ppb + r
                    sc = jnp.sum(qb * kbuf[slot, u], axis=0, keepdims=True)
                    start = (blk_id * MOBA_BLOCK + r * PAGE_SIZE).astype(F32)
                    scores.append(sc - slope * ((past + t) - start - lanef))
            mx = scores[0]
            for sc in scores[1:]:
                mx = jnp.maximum(mx, sc)
            mx = jnp.max(mx, axis=1, keepdims=True)
            pr = jnp.exp(scores[0] - mx)
            den = pr
            acc = pr * vown
            i = 1
            for s in range(MOBA_TOPK):
                for r in range(ppb):
                    u = ((h * ds + t) * MOBA_TOPK + s) * ppb + r
                    pr = jnp.exp(scores[i] - mx)
                    den = den + pr
                    acc = acc + pr * vbuf[slot, u]
                    i += 1
            o_col = jnp.sum(acc, axis=1, keepdims=True) / jnp.sum(den, axis=1, keepdims=True)
            otile = jnp.where(lane == t, o_col, otile)
        o_ref[0, pl.ds(r0, hd), :] = otile
        return carry

    lax.fori_loop(0, MOBA_HEADS, head_body, 0)


def _decode_attn(phys, idx, qt, kown_t, vown_t, cache_kt, cache_vt, *, ds, past):
    DB, W, _ = qt.shape
    n_slab = MOBA_HEADS * ds * MOBA_TOPK * (MOBA_BLOCK // PAGE_SIZE)
    tile = pl.BlockSpec((1, W, LANES), lambda b, ph, ix: (b, 0, 0))
    grid_spec = pltpu.PrefetchScalarGridSpec(
        num_scalar_prefetch=2,
        grid=(DB,),
        in_specs=[tile, tile, tile, pl.BlockSpec(memory_space=pl.ANY), pl.BlockSpec(memory_space=pl.ANY)],
        out_specs=tile,
        scratch_shapes=[pltpu.VMEM((2, n_slab, MOBA_HD, PAGE_SIZE), F32),
                        pltpu.VMEM((2, n_slab, MOBA_HD, PAGE_SIZE), F32),
                        pltpu.SemaphoreType.DMA((2, 2))],
    )
    return pl.pallas_call(
        functools.partial(_decode_attn_kernel, ds=ds, past=past, n_slab=n_slab),
        grid_spec=grid_spec,
        out_shape=jax.ShapeDtypeStruct((DB, W, LANES), F32),
        compiler_params=pltpu.CompilerParams(dimension_semantics=("arbitrary",),
                                             vmem_limit_bytes=VMEM_LIMIT),
        name="decode_attn",
    )(phys, idx, qt, kown_t, vown_t, cache_kt, cache_vt)


def _tokens_on_lanes(a, db, ds):
    w = a.shape[0]
    a = a.reshape(w, db, ds).transpose(1, 0, 2)
    return jnp.pad(a, ((0, 0), (0, 0), (0, LANES - ds)))


def _layer_weights(w_in, conv_w, a_log, dt_bias, ffn_up, ffn_down, w_out):
    hq = GDN_HEADS * GDN_DK
    n_qkv = 3 * hq
    o_a = n_qkv
    o_z = o_a + 2 * GDN_HEADS
    o_m = o_z + hq
    n_m = MOBA_HEADS * MOBA_HD
    wa = jnp.concatenate([w_in[:, :n_qkv], w_in[:, o_z:o_m + n_m]], axis=1).astype(BF16)
    wkv_t = w_in[:, o_m + n_m:].T.astype(BF16)
    wab = jnp.pad(w_in[:, o_a:o_z], ((0, 0), (0, LANES - 2 * GDN_HEADS)))
    wab_hi = wab.astype(BF16)
    wab_lo = (wab - wab_hi.astype(F32)).astype(BF16)
    gp = jnp.zeros((8, LANES), F32).at[0, :GDN_HEADS].set(a_log).at[1, :GDN_HEADS].set(dt_bias)
    return dict(wa=wa, wkv_t=wkv_t, wab=jnp.concatenate([wab_hi, wab_lo], axis=1), gp=gp, conv_w=conv_w,
                up=[ffn_up[i].astype(BF16) for i in range(2)],
                down=[ffn_down[i].astype(BF16) for i in range(2)],
                wo=w_out.astype(BF16), n_qkv=n_qkv, n_z=hq, n_m=n_m)


def _trunk(x, mods, lw, norm_w, gdn_norm_w, conv0, s0, attend, *, tm, ct, t_valid, final_w):
    sh1, sc1, g1, sh2, sc2, g2, sh3, sc3, g3 = mods
    h = _ffn(x, sh1, sc1, g1, norm_w[0], lw["up"][0], lw["down"][0], tm=tm)
    qkv_pre, z, mq, kt, vt, ab = _win(h, sh2, sc2, norm_w[1], lw["wa"], lw["wkv_t"], lw["wab"],
                                      tm=tm, n_qkv=lw["n_qkv"], n_z=lw["n_z"], n_m=lw["n_m"])
    go, s_new, conv_new = _gdn(*attend["gdn_in"](qkv_pre, ab, z), conv0, s0, lw["conv_w"], lw["gp"],
                               gdn_norm_w.reshape(1, -1), ct=ct, t_valid=t_valid)
    go = attend["gdn_out"](go)
    mo = attend["moba"](mq, kt, vt)
    h = _ffn(h, sh3, sc3, g3, norm_w[2], lw["up"][1], lw["down"][1], tm=tm, mix=(go, mo, g2, lw["wo"]),
             final_w=final_w)
    return h, s_new, conv_new, kt, vt


def kernel(x_prompt, x_sample, c_prompt, c_sample, cache_k, cache_v, page_table, state_gdn, state_conv,
           w_ada, b_ada, norm_w, ffn_up, ffn_down, w_in, conv_w, a_log, dt_bias, gdn_norm_w, w_out,
           final_norm_w):
    B, T, D = x_prompt.shape
    DB, DS, _ = x_sample.shape
    depth = w_ada.shape[0]
    past = page_table.shape[1] * PAGE_SIZE
    ppb = MOBA_BLOCK // PAGE_SIZE
    assert past % MOBA_BLOCK == 0 and past // MOBA_BLOCK >= MOBA_TOPK and DS <= PAGE_SIZE
    assert T % MOBA_BLOCK == 0 and (T - 1) // MOBA_BLOCK >= MOBA_TOPK
    NS = DB * DS
    hq = GDN_HEADS * GDN_DK
    n_m = MOBA_HEADS * MOBA_HD
    ct_s = GDN_CHUNK

    hp = x_prompt
    hs = x_sample.reshape(1, NS, D)
    outs = [[] for _ in range(8)]
    for l in range(depth):
        lw = _layer_weights(w_in[l], conv_w[l], a_log[l], dt_bias[l], ffn_up[l], ffn_down[l], w_out[l])
        mod = _ada_mod(jnp.concatenate([c_prompt, c_sample], axis=0), w_ada[l], b_ada[l])
        mod_p = mod[:B].reshape(B, N_MOD, 1, D)
        mod_s = jnp.repeat(mod[B:].reshape(DB, N_MOD, D), DS, axis=0).reshape(1, NS, N_MOD, D)
        mods_p = [mod_p[:, i] for i in range(N_MOD)]
        mods_s = [mod_s[:, :, i] for i in range(N_MOD)]
        last = l == depth - 1
        fw = final_norm_w if last else None

        attend_p = dict(gdn_in=lambda qkv, ab, z: (qkv, ab, z), gdn_out=lambda go: go, moba=_moba_prompt)
        hp, s_p, c_p, kt_p, vt_p = _trunk(
            hp, mods_p, lw, norm_w[l], gdn_norm_w[l], jnp.zeros((B, GDN_CONV - 1, 3 * hq), F32),
            jnp.zeros((B, GDN_HEADS, GDN_DK, GDN_DK), F32), attend_p, tm=512, ct=256, t_valid=T, final_w=fw)

        cache_kt = jnp.transpose(cache_k[l], (0, 2, 3, 1))
        cache_vt = jnp.transpose(cache_v[l], (0, 2, 3, 1))
        page_tbl = page_table

        def pad_seq(a):
            return jnp.pad(a.reshape(DB, DS, a.shape[-1]), ((0, 0), (0, ct_s - DS), (0, 0)))

        def moba_s(mq, kt, vt, cache_kt=cache_kt, cache_vt=cache_vt, page_tbl=page_tbl):
            mq_pad = jnp.pad(mq.reshape(DB, DS, n_m), ((0, 0), (0, 16 - DS), (0, 0)))
            idx = _kmean_topk(page_tbl, cache_kt, mq_pad)[:, :, :DS, :MOBA_TOPK]
            lpage = idx[..., None] * ppb + jnp.arange(ppb, dtype=jnp.int32)
            phys = jnp.take_along_axis(page_tbl, lpage.reshape(DB, -1), axis=1).reshape(lpage.shape)
            slab = phys * MOBA_HEADS + jnp.arange(MOBA_HEADS, dtype=jnp.int32)[None, :, None, None, None]
            qt = _tokens_on_lanes(mq[0].T, DB, DS)
            slabs = lambda c: c.reshape((-1,) + c.shape[2:])
            ot = _decode_attn(slab.reshape(-1), idx.reshape(-1), qt, _tokens_on_lanes(kt[0], DB, DS),
                              _tokens_on_lanes(vt[0], DB, DS), slabs(cache_kt), slabs(cache_vt),
                              ds=DS, past=past)
            return ot[:, :, :DS].transpose(0, 2, 1).reshape(1, NS, n_m)

        attend_s = dict(gdn_in=lambda qkv, ab, z: (pad_seq(qkv), pad_seq(ab), pad_seq(z)),
                        gdn_out=lambda go: go[:, :DS].reshape(1, NS, hq), moba=moba_s)
        hs, s_s, c_s, kt_s, vt_s = _trunk(
            hs, mods_s, lw, norm_w[l], gdn_norm_w[l], state_conv[l], state_gdn[l], attend_s,
            tm=NS, ct=ct_s, t_valid=DS, final_w=fw)

        def heads_last(a_t, b, t):
            return a_t.reshape(b, MOBA_HEADS, MOBA_HD, t).transpose(0, 3, 1, 2)

        new = [heads_last(kt_p, B, T), heads_last(vt_p, B, T), s_p, c_p,
               heads_last(kt_s.reshape(n_m, DB, DS).transpose(1, 0, 2), DB, DS),
               heads_last(vt_s.reshape(n_m, DB, DS).transpose(1, 0, 2), DB, DS), s_s, c_s]
        for o, v in zip(outs, new):
            o.append(v)
    y_prompt = hp
    y_sample = hs.reshape(DB, DS, D)
    return (y_prompt, y_sample) + tuple(jnp.stack(o) for o in outs)
```

```python
import functools

import jax
import jax.numpy as jnp
from jax import lax
from jax.experimental import pallas as pl
from jax.experimental.pallas import tpu as pltpu

F32 = jnp.float32
BF16 = jnp.bfloat16
EPS = 1e-6
NEG = -1e30
LOG2E = 1.4426950408889634

GDN_HEADS = 4
GDN_DK = 128
GDN_CONV = 4
GDN_CHUNK = 64
MOBA_HEADS = 8
MOBA_HD = 64
MOBA_BLOCK = 256
MOBA_TOPK = 3
PAGE_SIZE = 128
N_MOD = 9
LANES = 128
VMEM_LIMIT = 56 * 1024 * 1024


def _sigmoid(x):
    return 1.0 / (1.0 + jnp.exp(-x))


def _silu(x):
    return x * _sigmoid(x)


def _bdot(a, b):
    return jnp.dot(a.astype(BF16), b.astype(BF16), preferred_element_type=F32)


def _bdot_nt(a, b):
    return lax.dot_general(a.astype(BF16), b.astype(BF16), (((1,), (1,)), ((), ())),
                           preferred_element_type=F32)


def _split(x):
    hi = x.astype(BF16)
    lo = (x - hi.astype(F32)).astype(BF16)
    return hi, lo


def _dot3(a, b):
    a_hi, a_lo = _split(a)
    b_hi, b_lo = _split(b)
    d = functools.partial(jnp.dot, preferred_element_type=F32)
    return d(a_hi, b_hi) + (d(a_lo, b_hi) + d(a_hi, b_lo))


def _norm_mod(x, nw, sc, sh):
    y = x * lax.rsqrt(jnp.mean(x * x, axis=-1, keepdims=True) + EPS)
    return (y * nw) * (1.0 + sc) + sh


def _mod_spec(rows, tm, d):
    if rows == 1:
        return pl.BlockSpec((1, 1, d), lambda g, t: (g, 0, 0))
    return pl.BlockSpec((1, tm, d), lambda g, t: (g, t, 0))


def _resident(shape):
    nd = len(shape)
    return pl.BlockSpec(shape, lambda *_: (0,) * nd, pipeline_mode=pl.Buffered(1))


def _ada_kernel(c_ref, w_ref, b_ref, o_ref):
    o_ref[...] = _dot3(_silu(c_ref[...]), w_ref[...]) + b_ref[...]


def _ada_mod(c_all, w, b):
    n, d = c_all.shape
    dout = w.shape[1]
    tn = d
    return pl.pallas_call(
        _ada_kernel,
        grid=(dout // tn,),
        in_specs=[pl.BlockSpec((n, d), lambda i: (0, 0)),
                  pl.BlockSpec((d, tn), lambda i: (0, i)),
                  pl.BlockSpec((1, tn), lambda i: (0, i))],
        out_specs=pl.BlockSpec((n, tn), lambda i: (0, i)),
        out_shape=jax.ShapeDtypeStruct((n, dout), F32),
        compiler_params=pltpu.CompilerParams(dimension_semantics=("arbitrary",),
                                             vmem_limit_bytes=VMEM_LIMIT),
        name="ada_mod",
    )(c_all, w, b.reshape(1, dout))


def _ffn_kernel(*refs, tf, sub, has_mix, has_final):
    it = iter(refs)
    x_ref, sh_ref, sc_ref, g_ref, nw_ref, wup_ref, wdn_ref = [next(it) for _ in range(7)]
    if has_mix:
        go_ref, mo_ref, g2_ref, wo_ref = [next(it) for _ in range(4)]
    if has_final:
        fw_ref = next(it)
    o_ref = next(it)
    a_scr = next(it)

    tm = x_ref.shape[1]
    d_ff = wdn_ref.shape[0]

    def rows(ref, r0):
        return ref[0] if ref.shape[1] == 1 else ref[0, r0:r0 + sub, :]

    for r0 in range(0, tm, sub):
        x = x_ref[0, r0:r0 + sub, :]
        if has_mix:
            half = go_ref.shape[-1]
            mix = (_bdot(go_ref[0, r0:r0 + sub, :], wo_ref[0:half, :])
                   + _bdot(mo_ref[0, r0:r0 + sub, :], wo_ref[half:, :]))
            x = x + rows(g2_ref, r0) * mix
        hb = _norm_mod(x, nw_ref[...], rows(sc_ref, r0), rows(sh_ref, r0)).astype(BF16)
        for c in range(d_ff // tf):
            gate = jnp.dot(hb, wup_ref[:, c * tf:(c + 1) * tf], preferred_element_type=F32)
            up = jnp.dot(hb, wup_ref[:, d_ff + c * tf:d_ff + (c + 1) * tf], preferred_element_type=F32)
            a_scr[r0:r0 + sub, c * tf:(c + 1) * tf] = (_silu(gate) * up).astype(BF16)
        down = jnp.dot(a_scr[r0:r0 + sub, :], wdn_ref[...], preferred_element_type=F32)
        out = x + (0.5 * rows(g_ref, r0)) * down
        if has_final:
            out = out * lax.rsqrt(jnp.mean(out * out, axis=-1, keepdims=True) + EPS) * fw_ref[...]
        o_ref[0, r0:r0 + sub, :] = out


def _ffn(x, sh, sc, g, nw, wup, wdn, *, tm, sub=None, mix=None, final_w=None):
    G, T, D = x.shape
    d_ff = wdn.shape[0]
    mxu_n = 256
    tf = d_ff // 2 if (d_ff // 2) % mxu_n == 0 else d_ff
    row = pl.BlockSpec((1, tm, D), lambda g_, t: (g_, t, 0))
    ins = [x, sh, sc, g, nw.reshape(1, D), wup, wdn]
    specs = [row, _mod_spec(sh.shape[1], tm, D), _mod_spec(sc.shape[1], tm, D),
             _mod_spec(g.shape[1], tm, D), _resident((1, D)), _resident(wup.shape), _resident(wdn.shape)]
    if mix is not None:
        go, mo, g2, wo = mix
        half = go.shape[-1]
        ins += [go, mo, g2, wo]
        specs += [pl.BlockSpec((1, tm, half), lambda g_, t: (g_, t, 0)),
                  pl.BlockSpec((1, tm, half), lambda g_, t: (g_, t, 0)),
                  _mod_spec(g2.shape[1], tm, D), _resident(wo.shape)]
    if final_w is not None:
        ins.append(final_w.reshape(1, D))
        specs.append(_resident((1, D)))
    return pl.pallas_call(
        functools.partial(_ffn_kernel, tf=tf, sub=sub or tm, has_mix=mix is not None,
                          has_final=final_w is not None),
        grid=(G, T // tm),
        in_specs=specs,
        out_specs=row,
        out_shape=jax.ShapeDtypeStruct((G, T, D), F32),
        scratch_shapes=[pltpu.VMEM((tm, d_ff), BF16)],
        compiler_params=pltpu.CompilerParams(dimension_semantics=("parallel", "parallel"),
                                             vmem_limit_bytes=VMEM_LIMIT),
        name="ffn_mix" if mix is not None else "ffn",
    )(*ins)


def _win_kernel(h_ref, sh_ref, sc_ref, nw_ref, wa_ref, wkv_ref, wab_ref,
                qkv_ref, z_ref, mq_ref, kt_ref, vt_ref, ab_ref, *, n_qkv, n_z, n_m):
    hn = _norm_mod(h_ref[0], nw_ref[...], sc_ref[0], sh_ref[0])
    hb = hn.astype(BF16)
    h_lo = (hn - hb.astype(F32)).astype(BF16)
    d = functools.partial(jnp.dot, preferred_element_type=F32)
    qkv_ref[0] = d(hb, wa_ref[:, 0:n_qkv])
    z_ref[0] = d(hb, wa_ref[:, n_qkv:n_qkv + n_z])
    mq_ref[0] = d(hb, wa_ref[:, n_qkv + n_z:n_qkv + n_z + n_m]) * (MOBA_HD ** -0.5)
    nt = functools.partial(lax.dot_general, dimension_numbers=(((1,), (1,)), ((), ())),
                           preferred_element_type=F32)
    kt_ref[0] = nt(wkv_ref[0:n_m, :], hb)
    vt_ref[0] = nt(wkv_ref[n_m:, :], hb)
    r_hi = d(hb, wab_ref[...])
    r_lo = d(h_lo, wab_ref[...])
    ab_ref[0] = (r_hi[:, :LANES] + r_lo[:, :LANES]) + (r_hi[:, LANES:] + r_lo[:, LANES:])


def _win(h, sh, sc, nw, wa, wkv_t, wab, *, tm, n_qkv, n_z, n_m):
    G, T, D = h.shape
    row = lambda n: pl.BlockSpec((1, tm, n), lambda g_, t: (g_, t, 0))
    col = pl.BlockSpec((1, n_m, tm), lambda g_, t: (g_, 0, t))
    return pl.pallas_call(
        functools.partial(_win_kernel, n_qkv=n_qkv, n_z=n_z, n_m=n_m),
        grid=(G, T // tm),
        in_specs=[row(D), _mod_spec(sh.shape[1], tm, D), _mod_spec(sc.shape[1], tm, D),
                  _resident((1, D)), _resident(wa.shape), _resident(wkv_t.shape),
                  _resident(wab.shape)],
        out_specs=[row(n_qkv), row(n_z), row(n_m), col, col, row(LANES)],
        out_shape=[jax.ShapeDtypeStruct((G, T, n_qkv), F32), jax.ShapeDtypeStruct((G, T, n_z), F32),
                   jax.ShapeDtypeStruct((G, T, n_m), F32), jax.ShapeDtypeStruct((G, n_m, T), F32),
                   jax.ShapeDtypeStruct((G, n_m, T), F32), jax.ShapeDtypeStruct((G, T, LANES), F32)],
        compiler_params=pltpu.CompilerParams(dimension_semantics=("parallel", "parallel"),
                                             vmem_limit_bytes=VMEM_LIMIT),
        name="w_in",
    )(h, sh, sc, nw.reshape(1, D), wa, wkv_t, wab)


TRI_BASE = 8


def _bmm(a, b):
    return lax.dot_general(a.astype(BF16), b.astype(BF16), (((2,), (1,)), ((0,), (0,))),
                           preferred_element_type=F32)


def _bmm_nt(a, b):
    return lax.dot_general(a.astype(BF16), b.astype(BF16), (((2,), (2,)), ((0,), (0,))),
                           preferred_element_type=F32)


def _bmm_tn(a, b):
    return lax.dot_general(a.astype(BF16), b.astype(BF16), (((1,), (1,)), ((0,), (0,))),
                           preferred_element_type=F32)


def _unit_lower_inverse(m, eye, ri, ci, c):
    size = TRI_BASE
    a = jnp.where(ri // size == ci // size, m, 0.0)
    t = eye - a
    p = a
    e = 2
    while e < size:
        p = _bmm(p, p)
        t = t + _bmm(t, p)
        e *= 2
    h, n, _ = m.shape
    while size < c:
        pairs = n // (2 * size)
        split = lambda x: x.reshape(h, pairs, 2, size, n)
        t_up, t_low = split(t)[:, :, 0], split(t)[:, :, 1].reshape(h, n // 2, n)
        pair_ok = (ri // (2 * size) == ci // (2 * size)) & (ri // size != ci // size)
        f_low = split(jnp.where(pair_ok, m, 0.0))[:, :, 1].reshape(h, n // 2, n)
        ft = _bmm(f_low, t).reshape(h, pairs, size, n)
        ft_full = jnp.stack([jnp.zeros_like(ft), ft], axis=2).reshape(h, n, n)
        t_low = t_low - _bmm(t_low, ft_full)
        t = jnp.stack([t_up, t_low.reshape(h, pairs, size, n)], axis=2).reshape(h, n, n)
        size *= 2
    return t


def _gdn_kernel(x_ref, c0_ref, cw_ref, ab_ref, z_ref, s0_ref, gp_ref, nw_ref, go_ref, so_ref, nc_ref,
                xbuf, s_scr, *, ct, c, t_valid, n_t):
    t = pl.program_id(1)
    H, dk = GDN_HEADS, GDN_DK
    halo = GDN_CONV - 1
    base = 8

    nseq = x_ref.shape[0]
    n_col = xbuf.shape[0] // nseq
    col = lambda j: slice(j * dk, (j + 1) * dk)

    @pl.when(t == 0)
    def _():
        for sq in range(nseq):
            for j in range(n_col):
                xbuf[sq * n_col + j, base - halo:base, :] = c0_ref[sq, :, col(j)]
        s_scr[...] = s0_ref[...].reshape(nseq * H, dk, dk)

    @pl.when(t > 0)
    def _():
        for j in range(nseq * n_col):
            xbuf[j, base - halo:base, :] = xbuf[j, ct + base - halo:ct + base, :]

    conv = []
    for sq in range(nseq):
        for j in range(n_col):
            xj = xbuf.at[sq * n_col + j]
            xj[base:base + ct, :] = x_ref[sq, :, col(j)]
            acc = cw_ref[0:1, col(j)] * xj[base - halo:base - halo + ct, :]
            for i in range(1, GDN_CONV):
                acc = acc + cw_ref[i:i + 1, col(j)] * xj[base - halo + i:base - halo + i + ct, :]
            conv.append(_silu(acc))

    def heads(off):
        return jnp.stack([conv[sq * n_col + off + h] for sq in range(nseq) for h in range(H)])

    q, k, v = heads(0), heads(H), heads(2 * H)
    q = q * lax.rsqrt(jnp.sum(q * q, axis=-1, keepdims=True) + EPS) * (dk ** -0.5)
    k = k * lax.rsqrt(jnp.sum(k * k, axis=-1, keepdims=True) + EPS)

    rowi = lax.broadcasted_iota(jnp.int32, (ct, LANES), 0)
    rin = rowi % c
    gcol, grow, beta = [], [], []
    for sq in range(nseq):
        ab = ab_ref[sq]
        xg = ab + gp_ref[1:2, :]
        softplus = jnp.maximum(xg, 0.0) + jnp.log(1.0 + jnp.exp(-jnp.abs(xg)))
        g_all = -jnp.exp(gp_ref[0:1, :]) * softplus
        b_all = _sigmoid(ab)
        if t_valid % ct != 0:
            valid = (rowi + t * ct) < t_valid
            g_all = jnp.where(valid, g_all, 0.0)
            b_all = jnp.where(valid, b_all, 0.0)
        gc_all = g_all
        s = 1
        while s < c:
            gc_all = gc_all + jnp.where(rin >= s, pltpu.roll(gc_all, s, axis=0), 0.0)
            s *= 2
        gct = jnp.transpose(gc_all)
        gcol += [gc_all[:, h:h + 1] for h in range(H)]
        grow += [gct[h:h + 1, :] for h in range(H)]
        beta += [b_all[:, H + h:H + h + 1] for h in range(H)]
    gcol, grow, beta = jnp.stack(gcol), jnp.stack(grow), jnp.stack(beta)

    ri = lax.broadcasted_iota(jnp.int32, (ct, ct), 0)
    ci = lax.broadcasted_iota(jnp.int32, (ct, ct), 1)
    same = ri // c == ci // c
    eye = jnp.where(ri == ci, 1.0, 0.0)
    decay = jnp.exp(jnp.where(same & (ri >= ci), gcol - grow, -jnp.inf))
    kb = k * beta
    m = jnp.where(same & (ri > ci), _bmm_nt(kb, k) * decay, 0.0)
    tm1 = _unit_lower_inverse(m, eye, ri, ci, c) - eye
    egc = jnp.exp(gcol)
    rhs = jnp.concatenate([v * beta, kb * egc], axis=-1)
    uw = rhs + _bmm(tm1, rhs)
    u, w = uw[..., :dk], uw[..., dk:]
    attn = _bmm_nt(q, k) * decay
    qd = q * egc

    S = s_scr[...]
    o_state, v_new = [], []
    for cc in range(ct // c):
        r0 = cc * c
        if n_t == 1 and r0 >= t_valid:
            o_state.append(jnp.zeros((nseq * H, c, dk), F32))
            v_new.append(jnp.zeros((nseq * H, c, dk), F32))
            continue
        gl = gcol[:, r0 + c - 1:r0 + c, :]
        kd = k[:, r0:r0 + c] * jnp.exp(gl - gcol[:, r0:r0 + c])
        xs = _bmm(jnp.concatenate([w[:, r0:r0 + c], qd[:, r0:r0 + c]], axis=1), S)
        vn = u[:, r0:r0 + c] - xs[:, :c]
        o_state.append(xs[:, c:])
        v_new.append(vn)
        S = S * jnp.exp(gl) + _bmm_tn(kd, vn)
    s_scr[...] = S
    o = jnp.concatenate(o_state, axis=1) + _bmm(attn, jnp.concatenate(v_new, axis=1))
    o = o * lax.rsqrt(jnp.mean(o * o, axis=-1, keepdims=True) + EPS) * nw_ref[...]
    for sq in range(nseq):
        for h in range(H):
            go_ref[sq, :, col(h)] = o[sq * H + h] * _silu(z_ref[sq, :, col(h)])

    @pl.when(t == n_t - 1)
    def _():
        so_ref[...] = S.reshape(nseq, H, dk, dk)
        e0 = base - halo + (t_valid - (n_t - 1) * ct)
        for sq in range(nseq):
            for j in range(n_col):
                nc_ref[sq, :, col(j)] = xbuf[sq * n_col + j, e0:e0 + halo, :]


def _gdn(qkv_pre, ab, z, conv0, s0, conv_w, gp, gnw, *, ct, t_valid, nseq=2):
    G, T, W3 = qkv_pre.shape
    H, dk = GDN_HEADS, GDN_DK
    n_t = T // ct
    halo = GDN_CONV - 1
    nseq = nseq if G % nseq == 0 else 1
    tile = lambda n: pl.BlockSpec((nseq, ct, n), lambda b, t: (b, t, 0))
    per_seq = lambda shape: pl.BlockSpec((nseq,) + shape, lambda b, t: (b,) + (0,) * len(shape))
    whole = lambda shape: pl.BlockSpec(shape, lambda b, t: (0,) * len(shape))
    return pl.pallas_call(
        functools.partial(_gdn_kernel, ct=ct, c=GDN_CHUNK, t_valid=t_valid, n_t=n_t),
        grid=(G // nseq, n_t),
        in_specs=[tile(W3), per_seq((halo, W3)), whole((GDN_CONV, W3)), tile(LANES), tile(H * dk),
                  per_seq((H, dk, dk)), whole((8, LANES)), whole((1, dk))],
        out_specs=[tile(H * dk), per_seq((H, dk, dk)), per_seq((halo, W3))],
        out_shape=[jax.ShapeDtypeStruct((G, T, H * dk), F32), jax.ShapeDtypeStruct((G, H, dk, dk), F32),
                   jax.ShapeDtypeStruct((G, halo, W3), F32)],
        scratch_shapes=[pltpu.VMEM((nseq * W3 // dk, ct + 8, dk), F32), pltpu.VMEM((nseq * H, dk, dk), F32)],
        compiler_params=pltpu.CompilerParams(dimension_semantics=("parallel", "arbitrary"),
                                             vmem_limit_bytes=VMEM_LIMIT),
        name="gdn",
    )(qkv_pre, conv0, conv_w, ab, z, s0, gp, gnw)


def _alibi_slope(head_f):
    return jnp.exp2(-8.0 * (head_f + 1.0) / MOBA_HEADS)


def _moba_prompt_kernel(q_ref, kt_ref, vt_ref, o_ref, ka_ref, kb_ref, vb_ref, qa_ref, *, nb, blk):
    p = pl.program_id(1)
    hd = MOBA_HD
    T = nb * blk
    nr = 16
    nt = functools.partial(lax.dot_general, dimension_numbers=(((1,), (1,)), ((), ())),
                           preferred_element_type=F32)

    kt = kt_ref[0]
    rowk = lax.broadcasted_iota(jnp.int32, (2 * hd, T), 0)
    colk = lax.broadcasted_iota(jnp.int32, (2 * hd, T), 1)
    blk_of_col = colk // blk

    def key_rows(r):
        pos = jnp.where((r == nr) | (r == nr + 1), blk_of_col,
                        jnp.where((r == nr + 2) | (r == nr + 3), colk % blk, 0)).astype(F32)
        return jnp.where(r == blk_of_col, 1.0, pos)

    ka_ref[...] = jnp.where(rowk < hd, kt, key_rows(rowk - hd)).astype(BF16)
    kb_ref[...] = jnp.where(rowk >= hd, kt, key_rows(rowk)).astype(BF16)
    vb_ref[...] = vt_ref[0].astype(BF16)
    rown = lax.broadcasted_iota(jnp.int32, (nr, T), 0)
    coln = lax.broadcasted_iota(jnp.int32, (nr, T), 1) // blk
    avg = jnp.where(rown == coln, 1.0 / blk, 0.0).astype(BF16)
    k1 = kt.astype(BF16)
    r1 = kt - k1.astype(F32)
    k2 = r1.astype(BF16)
    k3 = (r1 - k2.astype(F32)).astype(BF16)
    kmt = nt(avg, k1) + (nt(avg, k2) + nt(avg, k3))
    kmt_hi, kmt_lo = _split(kmt)

    q = q_ref[0]
    lane = lax.broadcasted_iota(jnp.int32, (T, LANES), 1)
    pf = jnp.full((1, 1), p, jnp.int32).astype(F32)
    e_row = lax.broadcasted_iota(jnp.int32, (nr, LANES), 0)
    e_lane = lax.broadcasted_iota(jnp.int32, (nr, LANES), 1)
    nrank = 8 if nb <= 8 else nr
    rrow = lax.broadcasted_iota(jnp.int32, (nrank, T), 0)
    rcol = lax.broadcasted_iota(jnp.int32, (nrank, T), 1) // blk
    slope = []
    for hh in range(2):
        in_head = (lane >= hh * hd) & (lane < (hh + 1) * hd)
        qx = jnp.where(in_head, q, 0.0)
        slope.append(_alibi_slope(2.0 * pf + hh))
        q_hi, q_lo = _split(qx)
        gt = nt(kmt_hi, q_hi) + (nt(kmt_lo, q_hi) + nt(kmt_hi, q_lo))
        gt = jnp.where(rown < coln, gt, -jnp.inf)[:nrank]
        sel = jnp.where(rrow == rcol, 1.0, 0.0)
        for n in range(nb):
            gn = gt[n:n + 1, :]
            ahead = jnp.where((gt > gn) | ((gt == gn) & (rrow < n)), 1.0, 0.0)
            cnt = jnp.sum(ahead, axis=0, keepdims=True)
            sel = jnp.where((rrow == n) & (cnt < MOBA_TOPK) & (gn > -jnp.inf), 1.0, sel)
        if nrank < nr:
            sel = jnp.concatenate([sel, jnp.zeros((nr - nrank, T), F32)], axis=0)
        off = hd if hh == 0 else 0
        place = jnp.where((e_lane == e_row + off) & (e_row < nb), 1.0, 0.0).astype(BF16)
        placed = lax.dot_general(sel.astype(BF16), place, (((0,), (0,)), ((), ())),
                                 preferred_element_type=F32)
        s2 = slope[hh] * LOG2E
        s2_hi = s2.astype(BF16).astype(F32)
        s2_lo = s2 - s2_hi
        bias = jnp.where(placed > 0.5, 0.0, NEG)
        for i, val in enumerate((s2_hi * blk, s2_lo * blk, s2_hi, s2_lo)):
            bias = jnp.where(lane == off + nr + i, val, bias)
        qa_ref[hh] = jnp.where(in_head, q * LOG2E, bias).astype(BF16)

    lane_blk = lax.broadcasted_iota(jnp.int32, (blk, LANES), 1)
    ri = lax.broadcasted_iota(jnp.int32, (blk, blk), 0)
    ci = lax.broadcasted_iota(jnp.int32, (blk, blk), 1)
    k_refs = (ka_ref, kb_ref)
    def scores(j, hh):
        lo, hi = j * blk, (j + 1) * blk
        qj = qa_ref[hh, lo:hi, :]
        s_own = jnp.dot(qj, k_refs[hh][:, lo:hi], preferred_element_type=F32)
        s_past = jnp.dot(qj, k_refs[hh][:, :lo], preferred_element_type=F32) if j > 0 else None
        return s_own, s_past

    def softmax(s_own, s_past):
        s_own = jnp.where(ri >= ci, s_own, NEG)
        m = jnp.max(s_own, axis=1, keepdims=True)
        if s_past is not None:
            m = jnp.maximum(m, jnp.max(s_past, axis=1, keepdims=True))
        p_own = jnp.exp2(s_own - m)
        den = jnp.sum(p_own, axis=1, keepdims=True)
        p_past = None
        if s_past is not None:
            p_past = jnp.exp2(s_past - m)
            den = den + jnp.sum(p_past, axis=1, keepdims=True)
            p_past = p_past.astype(BF16)
        return p_own.astype(BF16), p_past, den

    def values(j, p_own, p_past, den):
        lo, hi = j * blk, (j + 1) * blk
        acc = nt(p_own, vb_ref[:, lo:hi])
        if p_past is not None:
            acc = acc + nt(p_past, vb_ref[:, :lo])
        return acc / den

    sc = [scores(0, 0), scores(0, 1)]
    for j in range(nb):
        outs = []
        for hh in range(2):
            sc_next = scores(j + 1, hh) if j + 1 < nb else None
            outs.append(values(j, *softmax(*sc[hh])))
            sc[hh] = sc_next
        o_ref[0, j * blk:(j + 1) * blk, :] = jnp.where(lane_blk < hd, outs[0], outs[1])


def _moba_prompt(mq, kt, vt):
    B, T, W = mq.shape
    blk = MOBA_BLOCK
    nb = T // blk
    assert nb <= 16
    qspec = pl.BlockSpec((1, T, LANES), lambda b, p: (b, 0, p))
    kspec = pl.BlockSpec((1, LANES, T), lambda b, p: (b, p, 0))
    return pl.pallas_call(
        functools.partial(_moba_prompt_kernel, nb=nb, blk=blk),
        grid=(B, W // LANES),
        in_specs=[qspec, kspec, kspec],
        out_specs=qspec,
        out_shape=jax.ShapeDtypeStruct((B, T, W), F32),
        scratch_shapes=[pltpu.VMEM((LANES, T), BF16)] * 3 + [pltpu.VMEM((2, T, LANES), BF16)],
        compiler_params=pltpu.CompilerParams(dimension_semantics=("parallel", "parallel"),
                                             vmem_limit_bytes=VMEM_LIMIT),
        name="moba_prompt",
    )(mq, kt, vt)


def _kmean_topk_kernel(pt_ref, *refs, n_pg, n_steps, ppb, nblk):
    pages = refs[:n_pg]
    q_ref, idx_ref, km_ref = refs[n_pg:]
    i = pl.program_id(1)
    W = MOBA_HEADS * MOBA_HD
    lane1 = lax.broadcasted_iota(jnp.int32, (1, LANES), 1)

    @pl.when(i == 0)
    def _():
        km_ref[...] = jnp.zeros((W, LANES), F32)

    km = km_ref[...]
    for bb in range(n_pg // ppb):
        tot = pages[bb * ppb][...]
        for r in range(1, ppb):
            tot = tot + pages[bb * ppb + r][...]
        col = jnp.sum(tot.reshape(W, PAGE_SIZE), axis=1, keepdims=True) * (1.0 / (ppb * PAGE_SIZE))
        km = jnp.where(lane1 == i * (n_pg // ppb) + bb, col, km)
    km_ref[...] = km

    @pl.when(i == n_steps - 1)
    def _():
        rows = q_ref.shape[1]
        lanef = lax.broadcasted_iota(jnp.int32, (rows, LANES), 1).astype(F32)
        for h in range(MOBA_HEADS):
            qh = q_ref[0, :, h * MOBA_HD:(h + 1) * MOBA_HD]
            gate = _dot3(qh, km[h * MOBA_HD:(h + 1) * MOBA_HD, :])
            gate = jnp.where(lanef < nblk, gate, -jnp.inf)
            out = jnp.zeros((rows, LANES), F32)
            for r in range(MOBA_TOPK):
                mx = jnp.max(gate, axis=1, keepdims=True)
                idx = jnp.min(jnp.where(gate == mx, lanef, 1e9), axis=1, keepdims=True)
                out = jnp.where(lanef == r, idx, out)
                gate = jnp.where(lanef == idx, -jnp.inf, gate)
            idx_ref[0, h] = out.astype(jnp.int32)


def _kmean_topk(page_table, cache_kt, mq_pad, *, n_pg=32):
    DB, n_pages = page_table.shape
    ppb = MOBA_BLOCK // PAGE_SIZE
    nblk = n_pages // ppb
    n_pg = min(n_pg, n_pages)
    assert n_pages % n_pg == 0 and n_pg % ppb == 0 and nblk <= LANES
    n_steps = n_pages // n_pg
    rows = mq_pad.shape[1]
    W = MOBA_HEADS * MOBA_HD

    def page_spec(r):
        return pl.BlockSpec((None, MOBA_HEADS, MOBA_HD, PAGE_SIZE),
                            lambda b, i, pt: (pt[b * n_pages + i * n_pg + r], 0, 0, 0))

    grid_spec = pltpu.PrefetchScalarGridSpec(
        num_scalar_prefetch=1,
        grid=(DB, n_steps),
        in_specs=[page_spec(r) for r in range(n_pg)] + [pl.BlockSpec((1, rows, W), lambda b, i, pt: (b, 0, 0))],
        out_specs=pl.BlockSpec((1, MOBA_HEADS, rows, LANES), lambda b, i, pt: (b, 0, 0, 0)),
        scratch_shapes=[pltpu.VMEM((W, LANES), F32)],
    )
    return pl.pallas_call(
        functools.partial(_kmean_topk_kernel, n_pg=n_pg, n_steps=n_steps, ppb=ppb, nblk=nblk),
        grid_spec=grid_spec,
        out_shape=jax.ShapeDtypeStruct((DB, MOBA_HEADS, rows, LANES), jnp.int32),
        compiler_params=pltpu.CompilerParams(dimension_semantics=("parallel", "arbitrary"),
                                             vmem_limit_bytes=VMEM_LIMIT),
        name="kmean_topk",
    )(page_table.reshape(-1), *([cache_kt] * n_pg), mq_pad)


def _decode_attn_kernel(phys_ref, idx_ref, qt_ref, kown_ref, vown_ref, ck_hbm, cv_hbm, o_ref,
                        kbuf, vbuf, sem, *, ds, past, n_slab):
    b = pl.program_id(0)
    nb = pl.num_programs(0)
    hd = MOBA_HD
    ppb = MOBA_BLOCK // PAGE_SIZE

    def copies(bi, slot, u):
        slab = phys_ref[bi * n_slab + u]
        return (pltpu.make_async_copy(ck_hbm.at[slab], kbuf.at[slot, u], sem.at[slot, 0]),
                pltpu.make_async_copy(cv_hbm.at[slab], vbuf.at[slot, u], sem.at[slot, 1]))

    def start_all(bi, slot):
        def body(u, carry):
            ck, cv = copies(bi, slot, u)
            ck.start()
            cv.start()
            return carry
        lax.fori_loop(0, n_slab, body, 0, unroll=8)

    def wait_all(bi, slot):
        def body(u, carry):
            ck, cv = copies(bi, slot, u)
            ck.wait()
            cv.wait()
            return carry
        lax.fori_loop(0, n_slab, body, 0, unroll=8)

    slot = b % 2

    @pl.when(b == 0)
    def _():
        start_all(b, slot)

    @pl.when(b + 1 < nb)
    def _():
        start_all(b + 1, 1 - slot)

    wait_all(b, slot)

    lane = lax.broadcasted_iota(jnp.int32, (1, LANES), 1)
    lanef = lane.astype(F32)

    def head_body(h, carry):
        r0 = pl.multiple_of(h * hd, hd)
        slope = _alibi_slope(jnp.full((1, 1), h, jnp.int32).astype(F32))
        kown = kown_ref[0, pl.ds(r0, hd), :]
        vown = vown_ref[0, pl.ds(r0, hd), :]
        qt = qt_ref[0, pl.ds(r0, hd), :]
        otile = jnp.zeros((hd, LANES), F32)
        for t in range(ds):
            qb = jnp.broadcast_to(qt[:, t:t + 1], (hd, LANES))
            s_own = jnp.sum(qb * kown, axis=0, keepdims=True)
            s_own = jnp.where(lane <= t, s_own - slope * (t - lanef), NEG)
            scores = [s_own]
            for s in range(MOBA_TOPK):
                blk_id = idx_ref[(b * MOBA_HEADS + h) * (ds * MOBA_TOPK) + t * MOBA_TOPK + s]
                for r in range(ppb):
                    u = ((h * ds + t) * MOBA_TOPK + s) * ppb + r
                    sc = jnp.sum(qb * kbuf[slot, u], axis=0, keepdims=True)
                    start = (blk_id * MOBA_BLOCK + r * PAGE_SIZE).astype(F32)
                    scores.append(sc - slope * ((past + t) - start - lanef))
            mx = scores[0]
            for sc in scores[1:]:
                mx = jnp.maximum(mx, sc)
            mx = jnp.max(mx, axis=1, keepdims=True)
            pr = jnp.exp(scores[0] - mx)
            den = pr
            acc = pr * vown
            i = 1
            for s in range(MOBA_TOPK):
                for r in range(ppb):
                    u = ((h * ds + t) * MOBA_TOPK + s) * ppb + r
                    pr = jnp.exp(scores[i] - mx)
                    den = den + pr
                    acc = acc + pr * vbuf[slot, u]
                    i += 1
            o_col = jnp.sum(acc, axis=1, keepdims=True) / jnp.sum(den, axis=1, keepdims=True)
            otile = jnp.where(lane == t, o_col, otile)
        o_ref[0, pl.ds(r0, hd), :] = otile
        return carry

    lax.fori_loop(0, MOBA_HEADS, head_body, 0)


def _decode_attn(phys, idx, qt, kown_t, vown_t, cache_kt, cache_vt, *, ds, past):
    DB, W, _ = qt.shape
    n_slab = MOBA_HEADS * ds * MOBA_TOPK * (MOBA_BLOCK // PAGE_SIZE)
    tile = pl.BlockSpec((1, W, LANES), lambda b, ph, ix: (b, 0, 0))
    grid_spec = pltpu.PrefetchScalarGridSpec(
        num_scalar_prefetch=2,
        grid=(DB,),
        in_specs=[tile, tile, tile, pl.BlockSpec(memory_space=pl.ANY), pl.BlockSpec(memory_space=pl.ANY)],
        out_specs=tile,
        scratch_shapes=[pltpu.VMEM((2, n_slab, MOBA_HD, PAGE_SIZE), F32),
                        pltpu.VMEM((2, n_slab, MOBA_HD, PAGE_SIZE), F32),
                        pltpu.SemaphoreType.DMA((2, 2))],
    )
    return pl.pallas_call(
        functools.partial(_decode_attn_kernel, ds=ds, past=past, n_slab=n_slab),
        grid_spec=grid_spec,
        out_shape=jax.ShapeDtypeStruct((DB, W, LANES), F32),
        compiler_params=pltpu.CompilerParams(dimension_semantics=("arbitrary",),
                                             vmem_limit_bytes=VMEM_LIMIT),
        name="decode_attn",
    )(phys, idx, qt, kown_t, vown_t, cache_kt, cache_vt)


def _tokens_on_lanes(a, db, ds):
    w = a.shape[0]
    a = a.reshape(w, db, ds).transpose(1, 0, 2)
    return jnp.pad(a, ((0, 0), (0, 0), (0, LANES - ds)))


def _layer_weights(w_in, conv_w, a_log, dt_bias, ffn_up, ffn_down, w_out):
    hq = GDN_HEADS * GDN_DK
    n_qkv = 3 * hq
    o_a = n_qkv
    o_z = o_a + 2 * GDN_HEADS
    o_m = o_z + hq
    n_m = MOBA_HEADS * MOBA_HD
    wa = jnp.concatenate([w_in[:, :n_qkv], w_in[:, o_z:o_m + n_m]], axis=1).astype(BF16)
    wkv_t = w_in[:, o_m + n_m:].T.astype(BF16)
    wab = jnp.pad(w_in[:, o_a:o_z], ((0, 0), (0, LANES - 2 * GDN_HEADS)))
    wab_hi = wab.astype(BF16)
    wab_lo = (wab - wab_hi.astype(F32)).astype(BF16)
    gp = jnp.zeros((8, LANES), F32).at[0, :GDN_HEADS].set(a_log).at[1, :GDN_HEADS].set(dt_bias)
    return dict(wa=wa, wkv_t=wkv_t, wab=jnp.concatenate([wab_hi, wab_lo], axis=1), gp=gp, conv_w=conv_w,
                up=[ffn_up[i].astype(BF16) for i in range(2)],
                down=[ffn_down[i].astype(BF16) for i in range(2)],
                wo=w_out.astype(BF16), n_qkv=n_qkv, n_z=hq, n_m=n_m)


def _trunk(x, mods, lw, norm_w, gdn_norm_w, conv0, s0, attend, *, tm, ct, t_valid, final_w):
    sh1, sc1, g1, sh2, sc2, g2, sh3, sc3, g3 = mods
    h = _ffn(x, sh1, sc1, g1, norm_w[0], lw["up"][0], lw["down"][0], tm=tm)
    qkv_pre, z, mq, kt, vt, ab = _win(h, sh2, sc2, norm_w[1], lw["wa"], lw["wkv_t"], lw["wab"],
                                      tm=tm, n_qkv=lw["n_qkv"], n_z=lw["n_z"], n_m=lw["n_m"])
    go, s_new, conv_new = _gdn(*attend["gdn_in"](qkv_pre, ab, z), conv0, s0, lw["conv_w"], lw["gp"],
                               gdn_norm_w.reshape(1, -1), ct=ct, t_valid=t_valid)
    go = attend["gdn_out"](go)
    mo = attend["moba"](mq, kt, vt)
    h = _ffn(h, sh3, sc3, g3, norm_w[2], lw["up"][1], lw["down"][1], tm=tm, mix=(go, mo, g2, lw["wo"]),
             final_w=final_w)
    return h, s_new, conv_new, kt, vt


def kernel(x_prompt, x_sample, c_prompt, c_sample, cache_k, cache_v, page_table, state_gdn, state_conv,
           w_ada, b_ada, norm_w, ffn_up, ffn_down, w_in, conv_w, a_log, dt_bias, gdn_norm_w, w_out,
           final_norm_w):
    B, T, D = x_prompt.shape
    DB, DS, _ = x_sample.shape
    depth = w_ada.shape[0]
    past = page_table.shape[1] * PAGE_SIZE
    ppb = MOBA_BLOCK // PAGE_SIZE
    assert past % MOBA_BLOCK == 0 and past // MOBA_BLOCK >= MOBA_TOPK and DS <= PAGE_SIZE
    assert T % MOBA_BLOCK == 0 and (T - 1) // MOBA_BLOCK >= MOBA_TOPK
    NS = DB * DS
    hq = GDN_HEADS * GDN_DK
    n_m = MOBA_HEADS * MOBA_HD
    ct_s = GDN_CHUNK

    hp = x_prompt
    hs = x_sample.reshape(1, NS, D)
    outs = [[] for _ in range(8)]
    for l in range(depth):
        lw = _layer_weights(w_in[l], conv_w[l], a_log[l], dt_bias[l], ffn_up[l], ffn_down[l], w_out[l])
        mod = _ada_mod(jnp.concatenate([c_prompt, c_sample], axis=0), w_ada[l], b_ada[l])
        mod_p = mod[:B].reshape(B, N_MOD, 1, D)
        mod_s = jnp.repeat(mod[B:].reshape(DB, N_MOD, D), DS, axis=0).reshape(1, NS, N_MOD, D)
        mods_p = [mod_p[:, i] for i in range(N_MOD)]
        mods_s = [mod_s[:, :, i] for i in range(N_MOD)]
        last = l == depth - 1
        fw = final_norm_w if last else None

        attend_p = dict(gdn_in=lambda qkv, ab, z: (qkv, ab, z), gdn_out=lambda go: go, moba=_moba_prompt)
        hp, s_p, c_p, kt_p, vt_p = _trunk(
            hp, mods_p, lw, norm_w[l], gdn_norm_w[l], jnp.zeros((B, GDN_CONV - 1, 3 * hq), F32),
            jnp.zeros((B, GDN_HEADS, GDN_DK, GDN_DK), F32), attend_p, tm=512, ct=256, t_valid=T, final_w=fw)

        cache_kt = jnp.transpose(cache_k[l], (0, 2, 3, 1))
        cache_vt = jnp.transpose(cache_v[l], (0, 2, 3, 1))
        page_tbl = page_table

        def pad_seq(a):
            return jnp.pad(a.reshape(DB, DS, a.shape[-1]), ((0, 0), (0, ct_s - DS), (0, 0)))

        def moba_s(mq, kt, vt, cache_kt=cache_kt, cache_vt=cache_vt, page_tbl=page_tbl):
            mq_pad = jnp.pad(mq.reshape(DB, DS, n_m), ((0, 0), (0, 16 - DS), (0, 0)))
            idx = _kmean_topk(page_tbl, cache_kt, mq_pad)[:, :, :DS, :MOBA_TOPK]
            lpage = idx[..., None] * ppb + jnp.arange(ppb, dtype=jnp.int32)
            phys = jnp.take_along_axis(page_tbl, lpage.reshape(DB, -1), axis=1).reshape(lpage.shape)
            slab = phys * MOBA_HEADS + jnp.arange(MOBA_HEADS, dtype=jnp.int32)[None, :, None, None, None]
            qt = _tokens_on_lanes(mq[0].T, DB, DS)
            slabs = lambda c: c.reshape((-1,) + c.shape[2:])
            ot = _decode_attn(slab.reshape(-1), idx.reshape(-1), qt, _tokens_on_lanes(kt[0], DB, DS),
                              _tokens_on_lanes(vt[0], DB, DS), slabs(cache_kt), slabs(cache_vt),
                              ds=DS, past=past)
            return ot[:, :, :DS].transpose(0, 2, 1).reshape(1, NS, n_m)

        attend_s = dict(gdn_in=lambda qkv, ab, z: (pad_seq(qkv), pad_seq(ab), pad_seq(z)),
                        gdn_out=lambda go: go[:, :DS].reshape(1, NS, hq), moba=moba_s)
        hs, s_s, c_s, kt_s, vt_s = _trunk(
            hs, mods_s, lw, norm_w[l], gdn_norm_w[l], state_conv[l], state_gdn[l], attend_s,
            tm=NS, ct=ct_s, t_valid=DS, final_w=fw)

        def heads_last(a_t, b, t):
            return a_t.reshape(b, MOBA_HEADS, MOBA_HD, t).transpose(0, 3, 1, 2)

        new = [heads_last(kt_p, B, T), heads_last(vt_p, B, T), s_p, c_p,
               heads_last(kt_s.reshape(n_m, DB, DS).transpose(1, 0, 2), DB, DS),
               heads_last(vt_s.reshape(n_m, DB, DS).transpose(1, 0, 2), DB, DS), s_s, c_s]
        for o, v in zip(outs, new):
            o.append(v)
    y_prompt = hp
    y_sample = hs.reshape(DB, DS, D)
    return (y_prompt, y_sample) + tuple(jnp.stack(o) for o in outs)
```

```python
import functools

import jax
import jax.numpy as jnp
from jax import lax
from jax.experimental import pallas as pl
from jax.experimental.pallas import tpu as pltpu

F32 = jnp.float32
BF16 = jnp.bfloat16
EPS = 1e-6
NEG = -1e30
LOG2E = 1.4426950408889634

GDN_HEADS = 4
GDN_DK = 128
GDN_CONV = 4
GDN_CHUNK = 64
MOBA_HEADS = 8
MOBA_HD = 64
MOBA_BLOCK = 256
MOBA_TOPK = 3
PAGE_SIZE = 128
N_MOD = 9
LANES = 128
VMEM_LIMIT = 56 * 1024 * 1024


def _sigmoid(x):
    return 1.0 / (1.0 + jnp.exp(-x))


def _silu(x):
    return x * _sigmoid(x)


def _bdot(a, b):
    return jnp.dot(a.astype(BF16), b.astype(BF16), preferred_element_type=F32)


def _bdot_nt(a, b):
    return lax.dot_general(a.astype(BF16), b.astype(BF16), (((1,), (1,)), ((), ())),
                           preferred_element_type=F32)


def _split(x):
    hi = x.astype(BF16)
    lo = (x - hi.astype(F32)).astype(BF16)
    return hi, lo


def _dot3(a, b):
    a_hi, a_lo = _split(a)
    b_hi, b_lo = _split(b)
    d = functools.partial(jnp.dot, preferred_element_type=F32)
    return d(a_hi, b_hi) + (d(a_lo, b_hi) + d(a_hi, b_lo))


def _norm_mod(x, nw, sc, sh):
    y = x * lax.rsqrt(jnp.mean(x * x, axis=-1, keepdims=True) + EPS)
    return (y * nw) * (1.0 + sc) + sh


def _mod_spec(rows, tm, d):
    if rows == 1:
        return pl.BlockSpec((1, 1, d), lambda g, t: (g, 0, 0))
    return pl.BlockSpec((1, tm, d), lambda g, t: (g, t, 0))


def _resident(shape):
    nd = len(shape)
    return pl.BlockSpec(shape, lambda *_: (0,) * nd, pipeline_mode=pl.Buffered(1))


def _ada_kernel(c_ref, w_ref, b_ref, o_ref):
    o_ref[...] = _dot3(_silu(c_ref[...]), w_ref[...]) + b_ref[...]


def _ada_mod(c_all, w, b):
    n, d = c_all.shape
    dout = w.shape[1]
    tn = d
    return pl.pallas_call(
        _ada_kernel,
        grid=(dout // tn,),
        in_specs=[pl.BlockSpec((n, d), lambda i: (0, 0)),
                  pl.BlockSpec((d, tn), lambda i: (0, i)),
                  pl.BlockSpec((1, tn), lambda i: (0, i))],
        out_specs=pl.BlockSpec((n, tn), lambda i: (0, i)),
        out_shape=jax.ShapeDtypeStruct((n, dout), F32),
        compiler_params=pltpu.CompilerParams(dimension_semantics=("arbitrary",),
                                             vmem_limit_bytes=VMEM_LIMIT),
        name="ada_mod",
    )(c_all, w, b.reshape(1, dout))


def _ffn_kernel(*refs, tf, sub, has_mix, has_final):
    it = iter(refs)
    x_ref, sh_ref, sc_ref, g_ref, nw_ref, wup_ref, wdn_ref = [next(it) for _ in range(7)]
    if has_mix:
        go_ref, mo_ref, g2_ref, wo_ref = [next(it) for _ in range(4)]
    if has_final:
        fw_ref = next(it)
    o_ref = next(it)
    a_scr = next(it)

    tm = x_ref.shape[1]
    d_ff = wdn_ref.shape[0]

    def rows(ref, r0):
        return ref[0] if ref.shape[1] == 1 else ref[0, r0:r0 + sub, :]

    for r0 in range(0, tm, sub):
        x = x_ref[0, r0:r0 + sub, :]
        if has_mix:
            half = go_ref.shape[-1]
            mix = (_bdot(go_ref[0, r0:r0 + sub, :], wo_ref[0:half, :])
                   + _bdot(mo_ref[0, r0:r0 + sub, :], wo_ref[half:, :]))
            x = x + rows(g2_ref, r0) * mix
        hb = _norm_mod(x, nw_ref[...], rows(sc_ref, r0), rows(sh_ref, r0)).astype(BF16)
        for c in range(d_ff // tf):
            gate = jnp.dot(hb, wup_ref[:, c * tf:(c + 1) * tf], preferred_element_type=F32)
            up = jnp.dot(hb, wup_ref[:, d_ff + c * tf:d_ff + (c + 1) * tf], preferred_element_type=F32)
            a_scr[r0:r0 + sub, c * tf:(c + 1) * tf] = (_silu(gate) * up).astype(BF16)
        down = jnp.dot(a_scr[r0:r0 + sub, :], wdn_ref[...], preferred_element_type=F32)
        out = x + (0.5 * rows(g_ref, r0)) * down
        if has_final:
            out = out * lax.rsqrt(jnp.mean(out * out, axis=-1, keepdims=True) + EPS) * fw_ref[...]
        o_ref[0, r0:r0 + sub, :] = out


def _ffn(x, sh, sc, g, nw, wup, wdn, *, tm, sub=None, mix=None, final_w=None):
    G, T, D = x.shape
    d_ff = wdn.shape[0]
    mxu_n = 256
    tf = d_ff // 2 if (d_ff // 2) % mxu_n == 0 else d_ff
    row = pl.BlockSpec((1, tm, D), lambda g_, t: (g_, t, 0))
    ins = [x, sh, sc, g, nw.reshape(1, D), wup, wdn]
    specs = [row, _mod_spec(sh.shape[1], tm, D), _mod_spec(sc.shape[1], tm, D),
             _mod_spec(g.shape[1], tm, D), _resident((1, D)), _resident(wup.shape), _resident(wdn.shape)]
    if mix is not None:
        go, mo, g2, wo = mix
        half = go.shape[-1]
        ins += [go, mo, g2, wo]
        specs += [pl.BlockSpec((1, tm, half), lambda g_, t: (g_, t, 0)),
                  pl.BlockSpec((1, tm, half), lambda g_, t: (g_, t, 0)),
                  _mod_spec(g2.shape[1], tm, D), _resident(wo.shape)]
    if final_w is not None:
        ins.append(final_w.reshape(1, D))
        specs.append(_resident((1, D)))
    return pl.pallas_call(
        functools.partial(_ffn_kernel, tf=tf, sub=sub or tm, has_mix=mix is not None,
                          has_final=final_w is not None),
        grid=(G, T // tm),
        in_specs=specs,
        out_specs=row,
        out_shape=jax.ShapeDtypeStruct((G, T, D), F32),
        scratch_shapes=[pltpu.VMEM((tm, d_ff), BF16)],
        compiler_params=pltpu.CompilerParams(dimension_semantics=("parallel", "parallel"),
                                             vmem_limit_bytes=VMEM_LIMIT),
        name="ffn_mix" if mix is not None else "ffn",
    )(*ins)


def _win_kernel(h_ref, sh_ref, sc_ref, nw_ref, wa_ref, wkv_ref, wab_ref,
                qkv_ref, z_ref, mq_ref, kt_ref, vt_ref, ab_ref, *, n_qkv, n_z, n_m):
    hn = _norm_mod(h_ref[0], nw_ref[...], sc_ref[0], sh_ref[0])
    hb = hn.astype(BF16)
    h_lo = (hn - hb.astype(F32)).astype(BF16)
    d = functools.partial(jnp.dot, preferred_element_type=F32)
    qkv_ref[0] = d(hb, wa_ref[:, 0:n_qkv])
    z_ref[0] = d(hb, wa_ref[:, n_qkv:n_qkv + n_z])
    mq_ref[0] = d(hb, wa_ref[:, n_qkv + n_z:n_qkv + n_z + n_m]) * (MOBA_HD ** -0.5)
    nt = functools.partial(lax.dot_general, dimension_numbers=(((1,), (1,)), ((), ())),
                           preferred_element_type=F32)
    kt_ref[0] = nt(wkv_ref[0:n_m, :], hb)
    vt_ref[0] = nt(wkv_ref[n_m:, :], hb)
    r_hi = d(hb, wab_ref[...])
    r_lo = d(h_lo, wab_ref[...])
    ab_ref[0] = (r_hi[:, :LANES] + r_lo[:, :LANES]) + (r_hi[:, LANES:] + r_lo[:, LANES:])


def _win(h, sh, sc, nw, wa, wkv_t, wab, *, tm, n_qkv, n_z, n_m):
    G, T, D = h.shape
    row = lambda n: pl.BlockSpec((1, tm, n), lambda g_, t: (g_, t, 0))
    col = pl.BlockSpec((1, n_m, tm), lambda g_, t: (g_, 0, t))
    return pl.pallas_call(
        functools.partial(_win_kernel, n_qkv=n_qkv, n_z=n_z, n_m=n_m),
        grid=(G, T // tm),
        in_specs=[row(D), _mod_spec(sh.shape[1], tm, D), _mod_spec(sc.shape[1], tm, D),
                  _resident((1, D)), _resident(wa.shape), _resident(wkv_t.shape),
                  _resident(wab.shape)],
        out_specs=[row(n_qkv), row(n_z), row(n_m), col, col, row(LANES)],
        out_shape=[jax.ShapeDtypeStruct((G, T, n_qkv), F32), jax.ShapeDtypeStruct((G, T, n_z), F32),
                   jax.ShapeDtypeStruct((G, T, n_m), F32), jax.ShapeDtypeStruct((G, n_m, T), F32),
                   jax.ShapeDtypeStruct((G, n_m, T), F32), jax.ShapeDtypeStruct((G, T, LANES), F32)],
        compiler_params=pltpu.CompilerParams(dimension_semantics=("parallel", "parallel"),
                                             vmem_limit_bytes=VMEM_LIMIT),
        name="w_in",
    )(h, sh, sc, nw.reshape(1, D), wa, wkv_t, wab)


TRI_BASE = 8


def _bmm(a, b):
    return lax.dot_general(a.astype(BF16), b.astype(BF16), (((2,), (1,)), ((0,), (0,))),
                           preferred_element_type=F32)


def _bmm_nt(a, b):
    return lax.dot_general(a.astype(BF16), b.astype(BF16), (((2,), (2,)), ((0,), (0,))),
                           preferred_element_type=F32)


def _bmm_tn(a, b):
    return lax.dot_general(a.astype(BF16), b.astype(BF16), (((1,), (1,)), ((0,), (0,))),
                           preferred_element_type=F32)


def _unit_lower_inverse(m, eye, ri, ci, c):
    size = TRI_BASE
    a = jnp.where(ri // size == ci // size, m, 0.0)
    t = eye - a
    p = a
    e = 2
    while e < size:
        p = _bmm(p, p)
        t = t + _bmm(t, p)
        e *= 2
    h, n, _ = m.shape
    while size < c:
        pairs = n // (2 * size)
        split = lambda x: x.reshape(h, pairs, 2, size, n)
        t_up, t_low = split(t)[:, :, 0], split(t)[:, :, 1].reshape(h, n // 2, n)
        pair_ok = (ri // (2 * size) == ci // (2 * size)) & (ri // size != ci // size)
        f_low = split(jnp.where(pair_ok, m, 0.0))[:, :, 1].reshape(h, n // 2, n)
        ft = _bmm(f_low, t).reshape(h, pairs, size, n)
        ft_full = jnp.stack([jnp.zeros_like(ft), ft], axis=2).reshape(h, n, n)
        t_low = t_low - _bmm(t_low, ft_full)
        t = jnp.stack([t_up, t_low.reshape(h, pairs, size, n)], axis=2).reshape(h, n, n)
        size *= 2
    return t


def _gdn_kernel(x_ref, c0_ref, cw_ref, ab_ref, z_ref, s0_ref, gp_ref, nw_ref, go_ref, so_ref, nc_ref,
                xbuf, s_scr, *, ct, c, t_valid, n_t):
    t = pl.program_id(1)
    H, dk = GDN_HEADS, GDN_DK
    halo = GDN_CONV - 1
    base = 8

    nseq = x_ref.shape[0]
    n_col = xbuf.shape[0] // nseq
    col = lambda j: slice(j * dk, (j + 1) * dk)

    @pl.when(t == 0)
    def _():
        for sq in range(nseq):
            for j in range(n_col):
                xbuf[sq * n_col + j, base - halo:base, :] = c0_ref[sq, :, col(j)]
        s_scr[...] = s0_ref[...].reshape(nseq * H, dk, dk)

    @pl.when(t > 0)
    def _():
        for j in range(nseq * n_col):
            xbuf[j, base - halo:base, :] = xbuf[j, ct + base - halo:ct + base, :]

    conv = []
    for sq in range(nseq):
        for j in range(n_col):
            xj = xbuf.at[sq * n_col + j]
            xj[base:base + ct, :] = x_ref[sq, :, col(j)]
            acc = cw_ref[0:1, col(j)] * xj[base - halo:base - halo + ct, :]
            for i in range(1, GDN_CONV):
                acc = acc + cw_ref[i:i + 1, col(j)] * xj[base - halo + i:base - halo + i + ct, :]
            conv.append(_silu(acc))

    def heads(off):
        return jnp.stack([conv[sq * n_col + off + h] for sq in range(nseq) for h in range(H)])

    q, k, v = heads(0), heads(H), heads(2 * H)
    q = q * lax.rsqrt(jnp.sum(q * q, axis=-1, keepdims=True) + EPS) * (dk ** -0.5)
    k = k * lax.rsqrt(jnp.sum(k * k, axis=-1, keepdims=True) + EPS)

    rowi = lax.broadcasted_iota(jnp.int32, (ct, LANES), 0)
    rin = rowi % c
    gcol, grow, beta = [], [], []
    for sq in range(nseq):
        ab = ab_ref[sq]
        xg = ab + gp_ref[1:2, :]
        softplus = jnp.maximum(xg, 0.0) + jnp.log(1.0 + jnp.exp(-jnp.abs(xg)))
        g_all = -jnp.exp(gp_ref[0:1, :]) * softplus
        b_all = _sigmoid(ab)
        if t_valid % ct != 0:
            valid = (rowi + t * ct) < t_valid
            g_all = jnp.where(valid, g_all, 0.0)
            b_all = jnp.where(valid, b_all, 0.0)
        gc_all = g_all
        s = 1
        while s < c:
            gc_all = gc_all + jnp.where(rin >= s, pltpu.roll(gc_all, s, axis=0), 0.0)
            s *= 2
        gct = jnp.transpose(gc_all)
        gcol += [gc_all[:, h:h + 1] for h in range(H)]
        grow += [gct[h:h + 1, :] for h in range(H)]
        beta += [b_all[:, H + h:H + h + 1] for h in range(H)]
    gcol, grow, beta = jnp.stack(gcol), jnp.stack(grow), jnp.stack(beta)

    ri = lax.broadcasted_iota(jnp.int32, (ct, ct), 0)
    ci = lax.broadcasted_iota(jnp.int32, (ct, ct), 1)
    same = ri // c == ci // c
    eye = jnp.where(ri == ci, 1.0, 0.0)
    decay = jnp.exp(jnp.where(same & (ri >= ci), gcol - grow, -jnp.inf))
    kb = k * beta
    m = jnp.where(same & (ri > ci), _bmm_nt(kb, k) * decay, 0.0)
    tm1 = _unit_lower_inverse(m, eye, ri, ci, c) - eye
    egc = jnp.exp(gcol)
    rhs = jnp.concatenate([v * beta, kb * egc], axis=-1)
    uw = rhs + _bmm(tm1, rhs)
    u, w = uw[..., :dk], uw[..., dk:]
    attn = _bmm_nt(q, k) * decay
    qd = q * egc

    S = s_scr[...]
    o_state, v_new = [], []
    for cc in range(ct // c):
        r0 = cc * c
        if n_t == 1 and r0 >= t_valid:
            o_state.append(jnp.zeros((nseq * H, c, dk), F32))
            v_new.append(jnp.zeros((nseq * H, c, dk), F32))
            continue
        gl = gcol[:, r0 + c - 1:r0 + c, :]
        kd = k[:, r0:r0 + c] * jnp.exp(gl - gcol[:, r0:r0 + c])
        xs = _bmm(jnp.concatenate([w[:, r0:r0 + c], qd[:, r0:r0 + c]], axis=1), S)
        vn = u[:, r0:r0 + c] - xs[:, :c]
        o_state.append(xs[:, c:])
        v_new.append(vn)
        S = S * jnp.exp(gl) + _bmm_tn(kd, vn)
    s_scr[...] = S
    o = jnp.concatenate(o_state, axis=1) + _bmm(attn, jnp.concatenate(v_new, axis=1))
    o = o * lax.rsqrt(jnp.mean(o * o, axis=-1, keepdims=True) + EPS) * nw_ref[...]
    for sq in range(nseq):
        for h in range(H):
            go_ref[sq, :, col(h)] = o[sq * H + h] * _silu(z_ref[sq, :, col(h)])

    @pl.when(t == n_t - 1)
    def _():
        so_ref[...] = S.reshape(nseq, H, dk, dk)
        e0 = base - halo + (t_valid - (n_t - 1) * ct)
        for sq in range(nseq):
            for j in range(n_col):
                nc_ref[sq, :, col(j)] = xbuf[sq * n_col + j, e0:e0 + halo, :]


def _gdn(qkv_pre, ab, z, conv0, s0, conv_w, gp, gnw, *, ct, t_valid, nseq=2):
    G, T, W3 = qkv_pre.shape
    H, dk = GDN_HEADS, GDN_DK
    n_t = T // ct
    halo = GDN_CONV - 1
    while G % nseq:
        nseq -= 1
    tile = lambda n: pl.BlockSpec((nseq, ct, n), lambda b, t: (b, t, 0))
    per_seq = lambda shape: pl.BlockSpec((nseq,) + shape, lambda b, t: (b,) + (0,) * len(shape))
    whole = lambda shape: pl.BlockSpec(shape, lambda b, t: (0,) * len(shape))
    return pl.pallas_call(
        functools.partial(_gdn_kernel, ct=ct, c=GDN_CHUNK, t_valid=t_valid, n_t=n_t),
        grid=(G // nseq, n_t),
        in_specs=[tile(W3), per_seq((halo, W3)), whole((GDN_CONV, W3)), tile(LANES), tile(H * dk),
                  per_seq((H, dk, dk)), whole((8, LANES)), whole((1, dk))],
        out_specs=[tile(H * dk), per_seq((H, dk, dk)), per_seq((halo, W3))],
        out_shape=[jax.ShapeDtypeStruct((G, T, H * dk), F32), jax.ShapeDtypeStruct((G, H, dk, dk), F32),
                   jax.ShapeDtypeStruct((G, halo, W3), F32)],
        scratch_shapes=[pltpu.VMEM((nseq * W3 // dk, ct + 8, dk), F32), pltpu.VMEM((nseq * H, dk, dk), F32)],
        compiler_params=pltpu.CompilerParams(dimension_semantics=("parallel", "arbitrary"),
                                             vmem_limit_bytes=VMEM_LIMIT),
        name="gdn",
    )(qkv_pre, conv0, conv_w, ab, z, s0, gp, gnw)


def _alibi_slope(head_f):
    return jnp.exp2(-8.0 * (head_f + 1.0) / MOBA_HEADS)


def _moba_prompt_kernel(q_ref, kt_ref, vt_ref, o_ref, ka_ref, kb_ref, vb_ref, qa_ref, *, nb, blk):
    p = pl.program_id(1)
    hd = MOBA_HD
    T = nb * blk
    nr = 16
    nt = functools.partial(lax.dot_general, dimension_numbers=(((1,), (1,)), ((), ())),
                           preferred_element_type=F32)

    kt = kt_ref[0]
    rowk = lax.broadcasted_iota(jnp.int32, (2 * hd, T), 0)
    colk = lax.broadcasted_iota(jnp.int32, (2 * hd, T), 1)
    blk_of_col = colk // blk

    def key_rows(r):
        pos = jnp.where((r == nr) | (r == nr + 1), blk_of_col,
                        jnp.where((r == nr + 2) | (r == nr + 3), colk % blk, 0)).astype(F32)
        return jnp.where(r == blk_of_col, 1.0, pos)

    ka_ref[...] = jnp.where(rowk < hd, kt, key_rows(rowk - hd)).astype(BF16)
    kb_ref[...] = jnp.where(rowk >= hd, kt, key_rows(rowk)).astype(BF16)
    vb_ref[...] = vt_ref[0].astype(BF16)
    rown = lax.broadcasted_iota(jnp.int32, (nr, T), 0)
    coln = lax.broadcasted_iota(jnp.int32, (nr, T), 1) // blk
    avg = jnp.where(rown == coln, 1.0 / blk, 0.0).astype(BF16)
    k1 = kt.astype(BF16)
    r1 = kt - k1.astype(F32)
    k2 = r1.astype(BF16)
    k3 = (r1 - k2.astype(F32)).astype(BF16)
    kmt = nt(avg, k1) + (nt(avg, k2) + nt(avg, k3))
    kmt_hi, kmt_lo = _split(kmt)

    q = q_ref[0]
    lane = lax.broadcasted_iota(jnp.int32, (T, LANES), 1)
    pf = jnp.full((1, 1), p, jnp.int32).astype(F32)
    e_row = lax.broadcasted_iota(jnp.int32, (nr, LANES), 0)
    e_lane = lax.broadcasted_iota(jnp.int32, (nr, LANES), 1)
    nrank = 8 if nb <= 8 else nr
    rrow = lax.broadcasted_iota(jnp.int32, (nrank, T), 0)
    rcol = lax.broadcasted_iota(jnp.int32, (nrank, T), 1) // blk
    slope = []
    for hh in range(2):
        in_head = (lane >= hh * hd) & (lane < (hh + 1) * hd)
        qx = jnp.where(in_head, q, 0.0)
        slope.append(_alibi_slope(2.0 * pf + hh))
        q_hi, q_lo = _split(qx)
        gt = nt(kmt_hi, q_hi) + (nt(kmt_lo, q_hi) + nt(kmt_hi, q_lo))
        gt = jnp.where(rown < coln, gt, -jnp.inf)[:nrank]
        sel = jnp.where(rrow == rcol, 1.0, 0.0)
        for n in range(nb):
            gn = gt[n:n + 1, :]
            ahead = jnp.where((gt > gn) | ((gt == gn) & (rrow < n)), 1.0, 0.0)
            cnt = jnp.sum(ahead, axis=0, keepdims=True)
            sel = jnp.where((rrow == n) & (cnt < MOBA_TOPK) & (gn > -jnp.inf), 1.0, sel)
        if nrank < nr:
            sel = jnp.concatenate([sel, jnp.zeros((nr - nrank, T), F32)], axis=0)
        off = hd if hh == 0 else 0
        place = jnp.where((e_lane == e_row + off) & (e_row < nb), 1.0, 0.0).astype(BF16)
        placed = lax.dot_general(sel.astype(BF16), place, (((0,), (0,)), ((), ())),
                                 preferred_element_type=F32)
        s2 = slope[hh] * LOG2E
        s2_hi = s2.astype(BF16).astype(F32)
        s2_lo = s2 - s2_hi
        bias = jnp.where(placed > 0.5, 0.0, NEG)
        for i, val in enumerate((s2_hi * blk, s2_lo * blk, s2_hi, s2_lo)):
            bias = jnp.where(lane == off + nr + i, val, bias)
        qa_ref[hh] = jnp.where(in_head, q * LOG2E, bias).astype(BF16)

    lane_blk = lax.broadcasted_iota(jnp.int32, (blk, LANES), 1)
    ri = lax.broadcasted_iota(jnp.int32, (blk, blk), 0)
    ci = lax.broadcasted_iota(jnp.int32, (blk, blk), 1)
    k_refs = (ka_ref, kb_ref)
    def scores(j, hh):
        lo, hi = j * blk, (j + 1) * blk
        qj = qa_ref[hh, lo:hi, :]
        s_own = jnp.dot(qj, k_refs[hh][:, lo:hi], preferred_element_type=F32)
        s_past = jnp.dot(qj, k_refs[hh][:, :lo], preferred_element_type=F32) if j > 0 else None
        return s_own, s_past

    def softmax(s_own, s_past):
        s_own = jnp.where(ri >= ci, s_own, NEG)
        m = jnp.max(s_own, axis=1, keepdims=True)
        if s_past is not None:
            m = jnp.maximum(m, jnp.max(s_past, axis=1, keepdims=True))
        p_own = jnp.exp2(s_own - m)
        den = jnp.sum(p_own, axis=1, keepdims=True)
        p_past = None
        if s_past is not None:
            p_past = jnp.exp2(s_past - m)
            den = den + jnp.sum(p_past, axis=1, keepdims=True)
            p_past = p_past.astype(BF16)
        return p_own.astype(BF16), p_past, den

    def values(j, p_own, p_past, den):
        lo, hi = j * blk, (j + 1) * blk
        acc = nt(p_own, vb_ref[:, lo:hi])
        if p_past is not None:
            acc = acc + nt(p_past, vb_ref[:, :lo])
        return acc / den

    sc = [scores(0, 0), scores(0, 1)]
    for j in range(nb):
        outs = []
        for hh in range(2):
            sc_next = scores(j + 1, hh) if j + 1 < nb else None
            outs.append(values(j, *softmax(*sc[hh])))
            sc[hh] = sc_next
        o_ref[0, j * blk:(j + 1) * blk, :] = jnp.where(lane_blk < hd, outs[0], outs[1])


def _moba_prompt(mq, kt, vt):
    B, T, W = mq.shape
    blk = MOBA_BLOCK
    nb = T // blk
    assert nb <= 16
    qspec = pl.BlockSpec((1, T, LANES), lambda b, p: (b, 0, p))
    kspec = pl.BlockSpec((1, LANES, T), lambda b, p: (b, p, 0))
    return pl.pallas_call(
        functools.partial(_moba_prompt_kernel, nb=nb, blk=blk),
        grid=(B, W // LANES),
        in_specs=[qspec, kspec, kspec],
        out_specs=qspec,
        out_shape=jax.ShapeDtypeStruct((B, T, W), F32),
        scratch_shapes=[pltpu.VMEM((LANES, T), BF16)] * 3 + [pltpu.VMEM((2, T, LANES), BF16)],
        compiler_params=pltpu.CompilerParams(dimension_semantics=("parallel", "parallel"),
                                             vmem_limit_bytes=VMEM_LIMIT),
        name="moba_prompt",
    )(mq, kt, vt)


def _kmean_topk_kernel(pt_ref, *refs, n_pg, n_steps, ppb, nblk):
    pages = refs[:n_pg]
    q_ref, idx_ref, km_ref = refs[n_pg:]
    i = pl.program_id(1)
    W = MOBA_HEADS * MOBA_HD
    lane1 = lax.broadcasted_iota(jnp.int32, (1, LANES), 1)

    @pl.when(i == 0)
    def _():
        km_ref[...] = jnp.zeros((W, LANES), F32)

    km = km_ref[...]
    for bb in range(n_pg // ppb):
        tot = pages[bb * ppb][...]
        for r in range(1, ppb):
            tot = tot + pages[bb * ppb + r][...]
        col = jnp.sum(tot.reshape(W, PAGE_SIZE), axis=1, keepdims=True) * (1.0 / (ppb * PAGE_SIZE))
        km = jnp.where(lane1 == i * (n_pg // ppb) + bb, col, km)
    km_ref[...] = km

    @pl.when(i == n_steps - 1)
    def _():
        rows = q_ref.shape[1]
        lanef = lax.broadcasted_iota(jnp.int32, (rows, LANES), 1).astype(F32)
        for h in range(MOBA_HEADS):
            qh = q_ref[0, :, h * MOBA_HD:(h + 1) * MOBA_HD]
            gate = _dot3(qh, km[h * MOBA_HD:(h + 1) * MOBA_HD, :])
            gate = jnp.where(lanef < nblk, gate, -jnp.inf)
            out = jnp.zeros((rows, LANES), F32)
            for r in range(MOBA_TOPK):
                mx = jnp.max(gate, axis=1, keepdims=True)
                idx = jnp.min(jnp.where(gate == mx, lanef, 1e9), axis=1, keepdims=True)
                out = jnp.where(lanef == r, idx, out)
                gate = jnp.where(lanef == idx, -jnp.inf, gate)
            idx_ref[0, h] = out.astype(jnp.int32)


def _kmean_topk(page_table, cache_kt, mq_pad, *, n_pg=32):
    DB, n_pages = page_table.shape
    ppb = MOBA_BLOCK // PAGE_SIZE
    nblk = n_pages // ppb
    n_pg = min(n_pg, n_pages)
    assert n_pages % n_pg == 0 and n_pg % ppb == 0 and nblk <= LANES
    n_steps = n_pages // n_pg
    rows = mq_pad.shape[1]
    W = MOBA_HEADS * MOBA_HD

    def page_spec(r):
        return pl.BlockSpec((None, MOBA_HEADS, MOBA_HD, PAGE_SIZE),
                            lambda b, i, pt: (pt[b * n_pages + i * n_pg + r], 0, 0, 0))

    grid_spec = pltpu.PrefetchScalarGridSpec(
        num_scalar_prefetch=1,
        grid=(DB, n_steps),
        in_specs=[page_spec(r) for r in range(n_pg)] + [pl.BlockSpec((1, rows, W), lambda b, i, pt: (b, 0, 0))],
        out_specs=pl.BlockSpec((1, MOBA_HEADS, rows, LANES), lambda b, i, pt: (b, 0, 0, 0)),
        scratch_shapes=[pltpu.VMEM((W, LANES), F32)],
    )
    return pl.pallas_call(
        functools.partial(_kmean_topk_kernel, n_pg=n_pg, n_steps=n_steps, ppb=ppb, nblk=nblk),
        grid_spec=grid_spec,
        out_shape=jax.ShapeDtypeStruct((DB, MOBA_HEADS, rows, LANES), jnp.int32),
        compiler_params=pltpu.CompilerParams(dimension_semantics=("parallel", "arbitrary"),
                                             vmem_limit_bytes=VMEM_LIMIT),
        name="kmean_topk",
    )(page_table.reshape(-1), *([cache_kt] * n_pg), mq_pad)


def _decode_attn_kernel(phys_ref, idx_ref, qt_ref, kown_ref, vown_ref, ck_hbm, cv_hbm, o_ref,
                        kbuf, vbuf, sem, *, ds, past, n_slab):
    b = pl.program_id(0)
    nb = pl.num_programs(0)
    hd = MOBA_HD
    ppb = MOBA_BLOCK // PAGE_SIZE

    def copies(bi, slot, u):
        slab = phys_ref[bi * n_slab + u]
        return (pltpu.make_async_copy(ck_hbm.at[slab], kbuf.at[slot, u], sem.at[slot, 0]),
                pltpu.make_async_copy(cv_hbm.at[slab], vbuf.at[slot, u], sem.at[slot, 1]))

    def start_all(bi, slot):
        def body(u, carry):
            ck, cv = copies(bi, slot, u)
            ck.start()
            cv.start()
            return carry
        lax.fori_loop(0, n_slab, body, 0, unroll=8)

    def wait_all(bi, slot):
        def body(u, carry):
            ck, cv = copies(bi, slot, u)
            ck.wait()
            cv.wait()
            return carry
        lax.fori_loop(0, n_slab, body, 0, unroll=8)

    slot = b % 2

    @pl.when(b == 0)
    def _():
        start_all(b, slot)

    @pl.when(b + 1 < nb)
    def _():
        start_all(b + 1, 1 - slot)

    wait_all(b, slot)

    lane = lax.broadcasted_iota(jnp.int32, (1, LANES), 1)
    lanef = lane.astype(F32)

    def head_body(h, carry):
        r0 = pl.multiple_of(h * hd, hd)
        slope = _alibi_slope(jnp.full((1, 1), h, jnp.int32).astype(F32))
        kown = kown_ref[0, pl.ds(r0, hd), :]
        vown = vown_ref[0, pl.ds(r0, hd), :]
        qt = qt_ref[0, pl.ds(r0, hd), :]
        otile = jnp.zeros((hd, LANES), F32)
        for t in range(ds):
            qb = jnp.broadcast_to(qt[:, t:t + 1], (hd, LANES))
            s_own = jnp.sum(qb * kown, axis=0, keepdims=True)
            s_own = jnp.where(lane <= t, s_own - slope * (t - lanef), NEG)
            scores = [s_own]
            for s in range(MOBA_TOPK):
                blk_id = idx_ref[(b * MOBA_HEADS + h) * (ds * MOBA_TOPK) + t * MOBA_TOPK + s]
                for r in range(ppb):
                    u = ((h * ds + t) * MOBA_TOPK + s) * ppb + r
                    sc = jnp.sum(qb * kbuf[slot, u], axis=0, keepdims=True)
                    start = (blk_id * MOBA_BLOCK + r * PAGE_SIZE).astype(F32)
                    scores.append(sc - slope * ((past + t) - start - lanef))
            mx = scores[0]
            for sc in scores[1:]:
                mx = jnp.maximum(mx, sc)
            mx = jnp.max(mx, axis=1, keepdims=True)
            pr = jnp.exp(scores[0] - mx)
            den = pr
            acc = pr * vown
            i = 1
            for s in range(MOBA_TOPK):
                for r in range(ppb):
                    u = ((h * ds + t) * MOBA_TOPK + s) * ppb + r
                    pr = jnp.exp(scores[i] - mx)
                    den = den + pr
                    acc = acc + pr * vbuf[slot, u]
                    i += 1
            o_col = jnp.sum(acc, axis=1, keepdims=True) / jnp.sum(den, axis=1, keepdims=True)
            otile = jnp.where(lane == t, o_col, otile)
        o_ref[0, pl.ds(r0, hd), :] = otile
        return carry

    lax.fori_loop(0, MOBA_HEADS, head_body, 0)


def _decode_attn(phys, idx, qt, kown_t, vown_t, cache_kt, cache_vt, *, ds, past):
    DB, W, _ = qt.shape
    n_slab = MOBA_HEADS * ds * MOBA_TOPK * (MOBA_BLOCK // PAGE_SIZE)
    tile = pl.BlockSpec((1, W, LANES), lambda b, ph, ix: (b, 0, 0))
    grid_spec = pltpu.PrefetchScalarGridSpec(
        num_scalar_prefetch=2,
        grid=(DB,),
        in_specs=[tile, tile, tile, pl.BlockSpec(memory_space=pl.ANY), pl.BlockSpec(memory_space=pl.ANY)],
        out_specs=tile,
        scratch_shapes=[pltpu.VMEM((2, n_slab, MOBA_HD, PAGE_SIZE), F32),
                        pltpu.VMEM((2, n_slab, MOBA_HD, PAGE_SIZE), F32),
                        pltpu.SemaphoreType.DMA((2, 2))],
    )
    return pl.pallas_call(
        functools.partial(_decode_attn_kernel, ds=ds, past=past, n_slab=n_slab),
        grid_spec=grid_spec,
        out_shape=jax.ShapeDtypeStruct((DB, W, LANES), F32),
        compiler_params=pltpu.CompilerParams(dimension_semantics=("arbitrary",),
                                             vmem_limit_bytes=VMEM_LIMIT),
        name="decode_attn",
    )(phys, idx, qt, kown_t, vown_t, cache_kt, cache_vt)


def _tokens_on_lanes(a, db, ds):
    w = a.shape[0]
    a = a.reshape(w, db, ds).transpose(1, 0, 2)
    return jnp.pad(a, ((0, 0), (0, 0), (0, LANES - ds)))


def _layer_weights(w_in, conv_w, a_log, dt_bias, ffn_up, ffn_down, w_out):
    hq = GDN_HEADS * GDN_DK
    n_qkv = 3 * hq
    o_a = n_qkv
    o_z = o_a + 2 * GDN_HEADS
    o_m = o_z + hq
    n_m = MOBA_HEADS * MOBA_HD
    wa = jnp.concatenate([w_in[:, :n_qkv], w_in[:, o_z:o_m + n_m]], axis=1).astype(BF16)
    wkv_t = w_in[:, o_m + n_m:].T.astype(BF16)
    wab = jnp.pad(w_in[:, o_a:o_z], ((0, 0), (0, LANES - 2 * GDN_HEADS)))
    wab_hi = wab.astype(BF16)
    wab_lo = (wab - wab_hi.astype(F32)).astype(BF16)
    gp = jnp.zeros((8, LANES), F32).at[0, :GDN_HEADS].set(a_log).at[1, :GDN_HEADS].set(dt_bias)
    return dict(wa=wa, wkv_t=wkv_t, wab=jnp.concatenate([wab_hi, wab_lo], axis=1), gp=gp, conv_w=conv_w,
                up=[ffn_up[i].astype(BF16) for i in range(2)],
                down=[ffn_down[i].astype(BF16) for i in range(2)],
                wo=w_out.astype(BF16), n_qkv=n_qkv, n_z=hq, n_m=n_m)


def _trunk(x, mods, lw, norm_w, gdn_norm_w, conv0, s0, attend, *, tm, ct, nseq, t_valid, final_w):
    sh1, sc1, g1, sh2, sc2, g2, sh3, sc3, g3 = mods
    h = _ffn(x, sh1, sc1, g1, norm_w[0], lw["up"][0], lw["down"][0], tm=tm)
    qkv_pre, z, mq, kt, vt, ab = _win(h, sh2, sc2, norm_w[1], lw["wa"], lw["wkv_t"], lw["wab"],
                                      tm=tm, n_qkv=lw["n_qkv"], n_z=lw["n_z"], n_m=lw["n_m"])
    go, s_new, conv_new = _gdn(*attend["gdn_in"](qkv_pre, ab, z), conv0, s0, lw["conv_w"], lw["gp"],
                               gdn_norm_w.reshape(1, -1), ct=ct, t_valid=t_valid, nseq=nseq)
    go = attend["gdn_out"](go)
    mo = attend["moba"](mq, kt, vt)
    h = _ffn(h, sh3, sc3, g3, norm_w[2], lw["up"][1], lw["down"][1], tm=tm, mix=(go, mo, g2, lw["wo"]),
             final_w=final_w)
    return h, s_new, conv_new, kt, vt


def kernel(x_prompt, x_sample, c_prompt, c_sample, cache_k, cache_v, page_table, state_gdn, state_conv,
           w_ada, b_ada, norm_w, ffn_up, ffn_down, w_in, conv_w, a_log, dt_bias, gdn_norm_w, w_out,
           final_norm_w):
    B, T, D = x_prompt.shape
    DB, DS, _ = x_sample.shape
    depth = w_ada.shape[0]
    past = page_table.shape[1] * PAGE_SIZE
    ppb = MOBA_BLOCK // PAGE_SIZE
    assert past % MOBA_BLOCK == 0 and past // MOBA_BLOCK >= MOBA_TOPK and DS <= PAGE_SIZE
    assert T % MOBA_BLOCK == 0 and (T - 1) // MOBA_BLOCK >= MOBA_TOPK
    NS = DB * DS
    hq = GDN_HEADS * GDN_DK
    n_m = MOBA_HEADS * MOBA_HD
    ct_s = GDN_CHUNK

    hp = x_prompt
    hs = x_sample.reshape(1, NS, D)
    outs = [[] for _ in range(8)]
    for l in range(depth):
        lw = _layer_weights(w_in[l], conv_w[l], a_log[l], dt_bias[l], ffn_up[l], ffn_down[l], w_out[l])
        mod = _ada_mod(jnp.concatenate([c_prompt, c_sample], axis=0), w_ada[l], b_ada[l])
        mod_p = mod[:B].reshape(B, N_MOD, 1, D)
        mod_s = jnp.repeat(mod[B:].reshape(DB, N_MOD, D), DS, axis=0).reshape(1, NS, N_MOD, D)
        mods_p = [mod_p[:, i] for i in range(N_MOD)]
        mods_s = [mod_s[:, :, i] for i in range(N_MOD)]
        last = l == depth - 1
        fw = final_norm_w if last else None

        attend_p = dict(gdn_in=lambda qkv, ab, z: (qkv, ab, z), gdn_out=lambda go: go, moba=_moba_prompt)
        hp, s_p, c_p, kt_p, vt_p = _trunk(
            hp, mods_p, lw, norm_w[l], gdn_norm_w[l], jnp.zeros((B, GDN_CONV - 1, 3 * hq), F32),
            jnp.zeros((B, GDN_HEADS, GDN_DK, GDN_DK), F32), attend_p, tm=512, ct=128, nseq=4, t_valid=T,
            final_w=fw)

        cache_kt = jnp.transpose(cache_k[l], (0, 2, 3, 1))
        cache_vt = jnp.transpose(cache_v[l], (0, 2, 3, 1))
        page_tbl = page_table

        def pad_seq(a):
            return jnp.pad(a.reshape(DB, DS, a.shape[-1]), ((0, 0), (0, ct_s - DS), (0, 0)))

        def moba_s(mq, kt, vt, cache_kt=cache_kt, cache_vt=cache_vt, page_tbl=page_tbl):
            mq_pad = jnp.pad(mq.reshape(DB, DS, n_m), ((0, 0), (0, 16 - DS), (0, 0)))
            idx = _kmean_topk(page_tbl, cache_kt, mq_pad)[:, :, :DS, :MOBA_TOPK]
            lpage = idx[..., None] * ppb + jnp.arange(ppb, dtype=jnp.int32)
            phys = jnp.take_along_axis(page_tbl, lpage.reshape(DB, -1), axis=1).reshape(lpage.shape)
            slab = phys * MOBA_HEADS + jnp.arange(MOBA_HEADS, dtype=jnp.int32)[None, :, None, None, None]
            qt = _tokens_on_lanes(mq[0].T, DB, DS)
            slabs = lambda c: c.reshape((-1,) + c.shape[2:])
            ot = _decode_attn(slab.reshape(-1), idx.reshape(-1), qt, _tokens_on_lanes(kt[0], DB, DS),
                              _tokens_on_lanes(vt[0], DB, DS), slabs(cache_kt), slabs(cache_vt),
                              ds=DS, past=past)
            return ot[:, :, :DS].transpose(0, 2, 1).reshape(1, NS, n_m)

        attend_s = dict(gdn_in=lambda qkv, ab, z: (pad_seq(qkv), pad_seq(ab), pad_seq(z)),
                        gdn_out=lambda go: go[:, :DS].reshape(1, NS, hq), moba=moba_s)
        hs, s_s, c_s, kt_s, vt_s = _trunk(
            hs, mods_s, lw, norm_w[l], gdn_norm_w[l], state_conv[l], state_gdn[l], attend_s,
            tm=NS, ct=ct_s, nseq=8, t_valid=DS, final_w=fw)

        def heads_last(a_t, b, t):
            return a_t.reshape(b, MOBA_HEADS, MOBA_HD, t).transpose(0, 3, 1, 2)

        new = [heads_last(kt_p, B, T), heads_last(vt_p, B, T), s_p, c_p,
               heads_last(kt_s.reshape(n_m, DB, DS).transpose(1, 0, 2), DB, DS),
               heads_last(vt_s.reshape(n_m, DB, DS).transpose(1, 0, 2), DB, DS), s_s, c_s]
        for o, v in zip(outs, new):
            o.append(v)
    y_prompt = hp
    y_sample = hs.reshape(DB, DS, D)
    return (y_prompt, y_sample) + tuple(jnp.stack(o) for o in outs)
```

```python
import functools

import jax
import jax.numpy as jnp
from jax import lax
from jax.experimental import pallas as pl
from jax.experimental.pallas import tpu as pltpu

F32 = jnp.float32
BF16 = jnp.bfloat16
EPS = 1e-6
NEG = -1e30
LOG2E = 1.4426950408889634

GDN_HEADS = 4
GDN_DK = 128
GDN_CONV = 4
GDN_CHUNK = 64
MOBA_HEADS = 8
MOBA_HD = 64
MOBA_BLOCK = 256
MOBA_TOPK = 3
PAGE_SIZE = 128
N_MOD = 9
LANES = 128
VMEM_LIMIT = 56 * 1024 * 1024


def _sigmoid(x):
    return 1.0 / (1.0 + jnp.exp(-x))


def _silu(x):
    return x * _sigmoid(x)


def _bdot(a, b):
    return jnp.dot(a.astype(BF16), b.astype(BF16), preferred_element_type=F32)


def _bdot_nt(a, b):
    return lax.dot_general(a.astype(BF16), b.astype(BF16), (((1,), (1,)), ((), ())),
                           preferred_element_type=F32)


def _split(x):
    hi = x.astype(BF16)
    lo = (x - hi.astype(F32)).astype(BF16)
    return hi, lo


def _dot3(a, b):
    a_hi, a_lo = _split(a)
    b_hi, b_lo = _split(b)
    d = functools.partial(jnp.dot, preferred_element_type=F32)
    return d(a_hi, b_hi) + (d(a_lo, b_hi) + d(a_hi, b_lo))


def _norm_mod(x, nw, sc, sh):
    y = x * lax.rsqrt(jnp.mean(x * x, axis=-1, keepdims=True) + EPS)
    return (y * nw) * (1.0 + sc) + sh


def _mod_spec(rows, tm, d):
    if rows == 1:
        return pl.BlockSpec((1, 1, d), lambda g, t: (g, 0, 0))
    return pl.BlockSpec((1, tm, d), lambda g, t: (g, t, 0))


def _resident(shape):
    nd = len(shape)
    return pl.BlockSpec(shape, lambda *_: (0,) * nd, pipeline_mode=pl.Buffered(1))


def _ada_kernel(c_ref, w_ref, b_ref, o_ref):
    o_ref[...] = _dot3(_silu(c_ref[...]), w_ref[...]) + b_ref[...]


def _ada_mod(c_all, w, b):
    n, d = c_all.shape
    dout = w.shape[1]
    tn = d
    return pl.pallas_call(
        _ada_kernel,
        grid=(dout // tn,),
        in_specs=[pl.BlockSpec((n, d), lambda i: (0, 0)),
                  pl.BlockSpec((d, tn), lambda i: (0, i)),
                  pl.BlockSpec((1, tn), lambda i: (0, i))],
        out_specs=pl.BlockSpec((n, tn), lambda i: (0, i)),
        out_shape=jax.ShapeDtypeStruct((n, dout), F32),
        compiler_params=pltpu.CompilerParams(dimension_semantics=("arbitrary",),
                                             vmem_limit_bytes=VMEM_LIMIT),
        name="ada_mod",
    )(c_all, w, b.reshape(1, dout))


def _ffn_kernel(*refs, tf, sub, has_mix, has_final):
    it = iter(refs)
    x_ref, sh_ref, sc_ref, g_ref, nw_ref, wup_ref, wdn_ref = [next(it) for _ in range(7)]
    if has_mix:
        go_ref, mo_ref, g2_ref, wo_ref = [next(it) for _ in range(4)]
    if has_final:
        fw_ref = next(it)
    o_ref = next(it)
    a_scr = next(it)

    tm = x_ref.shape[1]
    d_ff = wdn_ref.shape[0]

    def rows(ref, r0):
        return ref[0] if ref.shape[1] == 1 else ref[0, r0:r0 + sub, :]

    for r0 in range(0, tm, sub):
        x = x_ref[0, r0:r0 + sub, :]
        if has_mix:
            half = go_ref.shape[-1]
            mix = (_bdot(go_ref[0, r0:r0 + sub, :], wo_ref[0:half, :])
                   + _bdot(mo_ref[0, r0:r0 + sub, :], wo_ref[half:, :]))
            x = x + rows(g2_ref, r0) * mix
        hb = _norm_mod(x, nw_ref[...], rows(sc_ref, r0), rows(sh_ref, r0)).astype(BF16)
        for c in range(d_ff // tf):
            gate = jnp.dot(hb, wup_ref[:, c * tf:(c + 1) * tf], preferred_element_type=F32)
            up = jnp.dot(hb, wup_ref[:, d_ff + c * tf:d_ff + (c + 1) * tf], preferred_element_type=F32)
            a_scr[r0:r0 + sub, c * tf:(c + 1) * tf] = (_silu(gate) * up).astype(BF16)
        down = jnp.dot(a_scr[r0:r0 + sub, :], wdn_ref[...], preferred_element_type=F32)
        out = x + (0.5 * rows(g_ref, r0)) * down
        if has_final:
            out = out * lax.rsqrt(jnp.mean(out * out, axis=-1, keepdims=True) + EPS) * fw_ref[...]
        o_ref[0, r0:r0 + sub, :] = out


def _ffn(x, sh, sc, g, nw, wup, wdn, *, tm, sub=None, mix=None, final_w=None):
    G, T, D = x.shape
    d_ff = wdn.shape[0]
    mxu_n = 256
    tf = d_ff // 2 if (d_ff // 2) % mxu_n == 0 else d_ff
    row = pl.BlockSpec((1, tm, D), lambda g_, t: (g_, t, 0))
    ins = [x, sh, sc, g, nw.reshape(1, D), wup, wdn]
    specs = [row, _mod_spec(sh.shape[1], tm, D), _mod_spec(sc.shape[1], tm, D),
             _mod_spec(g.shape[1], tm, D), _resident((1, D)), _resident(wup.shape), _resident(wdn.shape)]
    if mix is not None:
        go, mo, g2, wo = mix
        half = go.shape[-1]
        ins += [go, mo, g2, wo]
        specs += [pl.BlockSpec((1, tm, half), lambda g_, t: (g_, t, 0)),
                  pl.BlockSpec((1, tm, half), lambda g_, t: (g_, t, 0)),
                  _mod_spec(g2.shape[1], tm, D), _resident(wo.shape)]
    if final_w is not None:
        ins.append(final_w.reshape(1, D))
        specs.append(_resident((1, D)))
    return pl.pallas_call(
        functools.partial(_ffn_kernel, tf=tf, sub=sub or tm, has_mix=mix is not None,
                          has_final=final_w is not None),
        grid=(G, T // tm),
        in_specs=specs,
        out_specs=row,
        out_shape=jax.ShapeDtypeStruct((G, T, D), F32),
        scratch_shapes=[pltpu.VMEM((tm, d_ff), BF16)],
        compiler_params=pltpu.CompilerParams(dimension_semantics=("parallel", "parallel"),
                                             vmem_limit_bytes=VMEM_LIMIT),
        name="ffn_mix" if mix is not None else "ffn",
    )(*ins)


def _win_kernel(h_ref, sh_ref, sc_ref, nw_ref, wa_ref, wkv_ref, wab_ref,
                qkv_ref, z_ref, mq_ref, kt_ref, vt_ref, ab_ref, *, n_qkv, n_z, n_m):
    hn = _norm_mod(h_ref[0], nw_ref[...], sc_ref[0], sh_ref[0])
    hb = hn.astype(BF16)
    h_lo = (hn - hb.astype(F32)).astype(BF16)
    d = functools.partial(jnp.dot, preferred_element_type=F32)
    qkv_ref[0] = d(hb, wa_ref[:, 0:n_qkv])
    z_ref[0] = d(hb, wa_ref[:, n_qkv:n_qkv + n_z])
    mq_ref[0] = d(hb, wa_ref[:, n_qkv + n_z:n_qkv + n_z + n_m]) * (MOBA_HD ** -0.5)
    nt = functools.partial(lax.dot_general, dimension_numbers=(((1,), (1,)), ((), ())),
                           preferred_element_type=F32)
    kt_ref[0] = nt(wkv_ref[0:n_m, :], hb)
    vt_ref[0] = nt(wkv_ref[n_m:, :], hb)
    r_hi = d(hb, wab_ref[...])
    r_lo = d(h_lo, wab_ref[...])
    ab_ref[0] = (r_hi[:, :LANES] + r_lo[:, :LANES]) + (r_hi[:, LANES:] + r_lo[:, LANES:])


def _win(h, sh, sc, nw, wa, wkv_t, wab, *, tm, n_qkv, n_z, n_m):
    G, T, D = h.shape
    row = lambda n: pl.BlockSpec((1, tm, n), lambda g_, t: (g_, t, 0))
    col = pl.BlockSpec((1, n_m, tm), lambda g_, t: (g_, 0, t))
    return pl.pallas_call(
        functools.partial(_win_kernel, n_qkv=n_qkv, n_z=n_z, n_m=n_m),
        grid=(G, T // tm),
        in_specs=[row(D), _mod_spec(sh.shape[1], tm, D), _mod_spec(sc.shape[1], tm, D),
                  _resident((1, D)), _resident(wa.shape), _resident(wkv_t.shape),
                  _resident(wab.shape)],
        out_specs=[row(n_qkv), row(n_z), row(n_m), col, col, row(LANES)],
        out_shape=[jax.ShapeDtypeStruct((G, T, n_qkv), F32), jax.ShapeDtypeStruct((G, T, n_z), F32),
                   jax.ShapeDtypeStruct((G, T, n_m), F32), jax.ShapeDtypeStruct((G, n_m, T), F32),
                   jax.ShapeDtypeStruct((G, n_m, T), F32), jax.ShapeDtypeStruct((G, T, LANES), F32)],
        compiler_params=pltpu.CompilerParams(dimension_semantics=("parallel", "parallel"),
                                             vmem_limit_bytes=VMEM_LIMIT),
        name="w_in",
    )(h, sh, sc, nw.reshape(1, D), wa, wkv_t, wab)


TRI_BASE = 8


def _bmm(a, b):
    return lax.dot_general(a.astype(BF16), b.astype(BF16), (((2,), (1,)), ((0,), (0,))),
                           preferred_element_type=F32)


def _bmm_nt(a, b):
    return lax.dot_general(a.astype(BF16), b.astype(BF16), (((2,), (2,)), ((0,), (0,))),
                           preferred_element_type=F32)


def _bmm_tn(a, b):
    return lax.dot_general(a.astype(BF16), b.astype(BF16), (((1,), (1,)), ((0,), (0,))),
                           preferred_element_type=F32)


def _unit_lower_inverse(m, eye, ri, ci, c):
    size = TRI_BASE
    a = jnp.where(ri // size == ci // size, m, 0.0)
    t = eye - a
    p = a
    e = 2
    while e < size:
        p = _bmm(p, p)
        t = t + _bmm(t, p)
        e *= 2
    h, n, _ = m.shape
    while size < c:
        pairs = n // (2 * size)
        split = lambda x: x.reshape(h, pairs, 2, size, n)
        t_up, t_low = split(t)[:, :, 0], split(t)[:, :, 1].reshape(h, n // 2, n)
        pair_ok = (ri // (2 * size) == ci // (2 * size)) & (ri // size != ci // size)
        f_low = split(jnp.where(pair_ok, m, 0.0))[:, :, 1].reshape(h, n // 2, n)
        ft = _bmm(f_low, t).reshape(h, pairs, size, n)
        ft_full = jnp.stack([jnp.zeros_like(ft), ft], axis=2).reshape(h, n, n)
        t_low = t_low - _bmm(t_low, ft_full)
        t = jnp.stack([t_up, t_low.reshape(h, pairs, size, n)], axis=2).reshape(h, n, n)
        size *= 2
    return t


def _gdn_kernel(x_ref, c0_ref, cw_ref, ab_ref, z_ref, s0_ref, gp_ref, nw_ref, go_ref, so_ref, nc_ref,
                xbuf, s_scr, *, ct, c, t_valid, n_t):
    t = pl.program_id(1)
    H, dk = GDN_HEADS, GDN_DK
    halo = GDN_CONV - 1
    base = 8

    nseq = x_ref.shape[0]
    n_col = xbuf.shape[0] // nseq
    col = lambda j: slice(j * dk, (j + 1) * dk)

    @pl.when(t == 0)
    def _():
        for sq in range(nseq):
            for j in range(n_col):
                xbuf[sq * n_col + j, base - halo:base, :] = c0_ref[sq, :, col(j)]
        s_scr[...] = s0_ref[...].reshape(nseq * H, dk, dk)

    @pl.when(t > 0)
    def _():
        for j in range(nseq * n_col):
            xbuf[j, base - halo:base, :] = xbuf[j, ct + base - halo:ct + base, :]

    conv = []
    for sq in range(nseq):
        for j in range(n_col):
            xj = xbuf.at[sq * n_col + j]
            xj[base:base + ct, :] = x_ref[sq, :, col(j)]
            acc = cw_ref[0:1, col(j)] * xj[base - halo:base - halo + ct, :]
            for i in range(1, GDN_CONV):
                acc = acc + cw_ref[i:i + 1, col(j)] * xj[base - halo + i:base - halo + i + ct, :]
            conv.append(_silu(acc))

    def heads(off):
        return jnp.stack([conv[sq * n_col + off + h] for sq in range(nseq) for h in range(H)])

    q, k, v = heads(0), heads(H), heads(2 * H)
    q = q * lax.rsqrt(jnp.sum(q * q, axis=-1, keepdims=True) + EPS) * (dk ** -0.5)
    k = k * lax.rsqrt(jnp.sum(k * k, axis=-1, keepdims=True) + EPS)

    rowi = lax.broadcasted_iota(jnp.int32, (ct, LANES), 0)
    rin = rowi % c
    gcol, grow, beta = [], [], []
    for sq in range(nseq):
        ab = ab_ref[sq]
        xg = ab + gp_ref[1:2, :]
        softplus = jnp.maximum(xg, 0.0) + jnp.log(1.0 + jnp.exp(-jnp.abs(xg)))
        g_all = -jnp.exp(gp_ref[0:1, :]) * softplus
        b_all = _sigmoid(ab)
        if t_valid % ct != 0:
            valid = (rowi + t * ct) < t_valid
            g_all = jnp.where(valid, g_all, 0.0)
            b_all = jnp.where(valid, b_all, 0.0)
        gc_all = g_all
        s = 1
        while s < c:
            gc_all = gc_all + jnp.where(rin >= s, pltpu.roll(gc_all, s, axis=0), 0.0)
            s *= 2
        gct = jnp.transpose(gc_all)
        gcol += [gc_all[:, h:h + 1] for h in range(H)]
        grow += [gct[h:h + 1, :] for h in range(H)]
        beta += [b_all[:, H + h:H + h + 1] for h in range(H)]
    gcol, grow, beta = jnp.stack(gcol), jnp.stack(grow), jnp.stack(beta)

    ri = lax.broadcasted_iota(jnp.int32, (ct, ct), 0)
    ci = lax.broadcasted_iota(jnp.int32, (ct, ct), 1)
    same = ri // c == ci // c
    eye = jnp.where(ri == ci, 1.0, 0.0)
    decay = jnp.exp(jnp.where(same & (ri >= ci), gcol - grow, -jnp.inf))
    kb = k * beta
    m = jnp.where(same & (ri > ci), _bmm_nt(kb, k) * decay, 0.0)
    tm1 = _unit_lower_inverse(m, eye, ri, ci, c) - eye
    egc = jnp.exp(gcol)
    rhs = jnp.concatenate([v * beta, kb * egc], axis=-1)
    uw = rhs + _bmm(tm1, rhs)
    u, w = uw[..., :dk], uw[..., dk:]
    attn = _bmm_nt(q, k) * decay
    qd = q * egc

    S = s_scr[...]
    o_state, v_new = [], []
    for cc in range(ct // c):
        r0 = cc * c
        if n_t == 1 and r0 >= t_valid:
            o_state.append(jnp.zeros((nseq * H, c, dk), F32))
            v_new.append(jnp.zeros((nseq * H, c, dk), F32))
            continue
        gl = gcol[:, r0 + c - 1:r0 + c, :]
        kd = k[:, r0:r0 + c] * jnp.exp(gl - gcol[:, r0:r0 + c])
        xs = _bmm(jnp.concatenate([w[:, r0:r0 + c], qd[:, r0:r0 + c]], axis=1), S)
        vn = u[:, r0:r0 + c] - xs[:, :c]
        o_state.append(xs[:, c:])
        v_new.append(vn)
        S = S * jnp.exp(gl) + _bmm_tn(kd, vn)
    s_scr[...] = S
    o = jnp.concatenate(o_state, axis=1) + _bmm(attn, jnp.concatenate(v_new, axis=1))
    o = o * lax.rsqrt(jnp.mean(o * o, axis=-1, keepdims=True) + EPS) * nw_ref[...]
    for sq in range(nseq):
        for h in range(H):
            go_ref[sq, :, col(h)] = o[sq * H + h] * _silu(z_ref[sq, :, col(h)])

    @pl.when(t == n_t - 1)
    def _():
        so_ref[...] = S.reshape(nseq, H, dk, dk)
        e0 = base - halo + (t_valid - (n_t - 1) * ct)
        for sq in range(nseq):
            for j in range(n_col):
                nc_ref[sq, :, col(j)] = xbuf[sq * n_col + j, e0:e0 + halo, :]


def _gdn(qkv_pre, ab, z, conv0, s0, conv_w, gp, gnw, *, ct, t_valid, nseq=2):
    G, T, W3 = qkv_pre.shape
    H, dk = GDN_HEADS, GDN_DK
    n_t = T // ct
    halo = GDN_CONV - 1
    while G % nseq:
        nseq -= 1
    tile = lambda n: pl.BlockSpec((nseq, ct, n), lambda b, t: (b, t, 0))
    per_seq = lambda shape: pl.BlockSpec((nseq,) + shape, lambda b, t: (b,) + (0,) * len(shape))
    whole = lambda shape: pl.BlockSpec(shape, lambda b, t: (0,) * len(shape))
    return pl.pallas_call(
        functools.partial(_gdn_kernel, ct=ct, c=GDN_CHUNK, t_valid=t_valid, n_t=n_t),
        grid=(G // nseq, n_t),
        in_specs=[tile(W3), per_seq((halo, W3)), whole((GDN_CONV, W3)), tile(LANES), tile(H * dk),
                  per_seq((H, dk, dk)), whole((8, LANES)), whole((1, dk))],
        out_specs=[tile(H * dk), per_seq((H, dk, dk)), per_seq((halo, W3))],
        out_shape=[jax.ShapeDtypeStruct((G, T, H * dk), F32), jax.ShapeDtypeStruct((G, H, dk, dk), F32),
                   jax.ShapeDtypeStruct((G, halo, W3), F32)],
        scratch_shapes=[pltpu.VMEM((nseq * W3 // dk, ct + 8, dk), F32), pltpu.VMEM((nseq * H, dk, dk), F32)],
        compiler_params=pltpu.CompilerParams(dimension_semantics=("parallel", "arbitrary"),
                                             vmem_limit_bytes=VMEM_LIMIT),
        name="gdn",
    )(qkv_pre, conv0, conv_w, ab, z, s0, gp, gnw)


def _alibi_slope(head_f):
    return jnp.exp2(-8.0 * (head_f + 1.0) / MOBA_HEADS)


def _moba_prompt_kernel(q_ref, kt_ref, vt_ref, o_ref, ka_ref, kb_ref, vb_ref, qa_ref, *, nb, blk):
    p = pl.program_id(1)
    hd = MOBA_HD
    T = nb * blk
    nr = 16
    nt = functools.partial(lax.dot_general, dimension_numbers=(((1,), (1,)), ((), ())),
                           preferred_element_type=F32)

    kt = kt_ref[0]
    rowk = lax.broadcasted_iota(jnp.int32, (2 * hd, T), 0)
    colk = lax.broadcasted_iota(jnp.int32, (2 * hd, T), 1)
    blk_of_col = colk // blk

    def key_rows(r):
        pos = jnp.where((r == nr) | (r == nr + 1), blk_of_col,
                        jnp.where((r == nr + 2) | (r == nr + 3), colk % blk, 0)).astype(F32)
        return jnp.where(r == blk_of_col, 1.0, pos)

    ka_ref[...] = jnp.where(rowk < hd, kt, key_rows(rowk - hd)).astype(BF16)
    kb_ref[...] = jnp.where(rowk >= hd, kt, key_rows(rowk)).astype(BF16)
    vb_ref[...] = vt_ref[0].astype(BF16)
    rown = lax.broadcasted_iota(jnp.int32, (nr, T), 0)
    coln = lax.broadcasted_iota(jnp.int32, (nr, T), 1) // blk
    avg = jnp.where(rown == coln, 1.0 / blk, 0.0).astype(BF16)
    k1 = kt.astype(BF16)
    r1 = kt - k1.astype(F32)
    k2 = r1.astype(BF16)
    k3 = (r1 - k2.astype(F32)).astype(BF16)
    kmt = nt(avg, k1) + (nt(avg, k2) + nt(avg, k3))
    kmt_hi, kmt_lo = _split(kmt)

    q = q_ref[0]
    lane = lax.broadcasted_iota(jnp.int32, (T, LANES), 1)
    pf = jnp.full((1, 1), p, jnp.int32).astype(F32)
    e_row = lax.broadcasted_iota(jnp.int32, (nr, LANES), 0)
    e_lane = lax.broadcasted_iota(jnp.int32, (nr, LANES), 1)
    nrank = 8 if nb <= 8 else nr
    rrow = lax.broadcasted_iota(jnp.int32, (nrank, T), 0)
    rcol = lax.broadcasted_iota(jnp.int32, (nrank, T), 1) // blk
    slope = []
    for hh in range(2):
        in_head = (lane >= hh * hd) & (lane < (hh + 1) * hd)
        qx = jnp.where(in_head, q, 0.0)
        slope.append(_alibi_slope(2.0 * pf + hh))
        q_hi, q_lo = _split(qx)
        gt = nt(kmt_hi, q_hi) + (nt(kmt_lo, q_hi) + nt(kmt_hi, q_lo))
        gt = jnp.where(rown < coln, gt, -jnp.inf)[:nrank]
        sel = jnp.where(rrow == rcol, 1.0, 0.0)
        for n in range(nb):
            gn = gt[n:n + 1, :]
            ahead = jnp.where((gt > gn) | ((gt == gn) & (rrow < n)), 1.0, 0.0)
            cnt = jnp.sum(ahead, axis=0, keepdims=True)
            sel = jnp.where((rrow == n) & (cnt < MOBA_TOPK) & (gn > -jnp.inf), 1.0, sel)
        if nrank < nr:
            sel = jnp.concatenate([sel, jnp.zeros((nr - nrank, T), F32)], axis=0)
        off = hd if hh == 0 else 0
        place = jnp.where((e_lane == e_row + off) & (e_row < nb), 1.0, 0.0).astype(BF16)
        placed = lax.dot_general(sel.astype(BF16), place, (((0,), (0,)), ((), ())),
                                 preferred_element_type=F32)
        s2 = slope[hh] * LOG2E
        s2_hi = s2.astype(BF16).astype(F32)
        s2_lo = s2 - s2_hi
        bias = jnp.where(placed > 0.5, 0.0, NEG)
        for i, val in enumerate((s2_hi * blk, s2_lo * blk, s2_hi, s2_lo)):
            bias = jnp.where(lane == off + nr + i, val, bias)
        qa_ref[hh] = jnp.where(in_head, q * LOG2E, bias).astype(BF16)

    lane_blk = lax.broadcasted_iota(jnp.int32, (blk, LANES), 1)
    ri = lax.broadcasted_iota(jnp.int32, (blk, blk), 0)
    ci = lax.broadcasted_iota(jnp.int32, (blk, blk), 1)
    k_refs = (ka_ref, kb_ref)
    def scores(j, hh):
        lo, hi = j * blk, (j + 1) * blk
        qj = qa_ref[hh, lo:hi, :]
        s_own = jnp.dot(qj, k_refs[hh][:, lo:hi], preferred_element_type=F32)
        s_past = jnp.dot(qj, k_refs[hh][:, :lo], preferred_element_type=F32) if j > 0 else None
        return s_own, s_past

    def softmax(s_own, s_past):
        s_own = jnp.where(ri >= ci, s_own, NEG)
        m = jnp.max(s_own, axis=1, keepdims=True)
        if s_past is not None:
            m = jnp.maximum(m, jnp.max(s_past, axis=1, keepdims=True))
        p_own = jnp.exp2(s_own - m)
        den = jnp.sum(p_own, axis=1, keepdims=True)
        p_past = None
        if s_past is not None:
            p_past = jnp.exp2(s_past - m)
            den = den + jnp.sum(p_past, axis=1, keepdims=True)
            p_past = p_past.astype(BF16)
        return p_own.astype(BF16), p_past, den

    def values(j, p_own, p_past, den):
        lo, hi = j * blk, (j + 1) * blk
        acc = nt(p_own, vb_ref[:, lo:hi])
        if p_past is not None:
            acc = acc + nt(p_past, vb_ref[:, :lo])
        return acc / den

    sc = [scores(0, 0), scores(0, 1)]
    for j in range(nb):
        outs = []
        for hh in range(2):
            sc_next = scores(j + 1, hh) if j + 1 < nb else None
            outs.append(values(j, *softmax(*sc[hh])))
            sc[hh] = sc_next
        o_ref[0, j * blk:(j + 1) * blk, :] = jnp.where(lane_blk < hd, outs[0], outs[1])


def _moba_prompt(mq, kt, vt):
    B, T, W = mq.shape
    blk = MOBA_BLOCK
    nb = T // blk
    assert nb <= 16
    qspec = pl.BlockSpec((1, T, LANES), lambda b, p: (b, 0, p))
    kspec = pl.BlockSpec((1, LANES, T), lambda b, p: (b, p, 0))
    return pl.pallas_call(
        functools.partial(_moba_prompt_kernel, nb=nb, blk=blk),
        grid=(B, W // LANES),
        in_specs=[qspec, kspec, kspec],
        out_specs=qspec,
        out_shape=jax.ShapeDtypeStruct((B, T, W), F32),
        scratch_shapes=[pltpu.VMEM((LANES, T), BF16)] * 3 + [pltpu.VMEM((2, T, LANES), BF16)],
        compiler_params=pltpu.CompilerParams(dimension_semantics=("parallel", "parallel"),
                                             vmem_limit_bytes=VMEM_LIMIT),
        name="moba_prompt",
    )(mq, kt, vt)


def _kmean_topk_kernel(pt_ref, *refs, n_pg, n_steps, ppb, nblk):
    pages = refs[:n_pg]
    q_ref, idx_ref, km_ref = refs[n_pg:]
    i = pl.program_id(1)
    W = MOBA_HEADS * MOBA_HD
    lane1 = lax.broadcasted_iota(jnp.int32, (1, LANES), 1)

    @pl.when(i == 0)
    def _():
        km_ref[...] = jnp.zeros((W, LANES), F32)

    km = km_ref[...]
    for bb in range(n_pg // ppb):
        tot = pages[bb * ppb][...]
        for r in range(1, ppb):
            tot = tot + pages[bb * ppb + r][...]
        col = jnp.sum(tot.reshape(W, PAGE_SIZE), axis=1, keepdims=True) * (1.0 / (ppb * PAGE_SIZE))
        km = jnp.where(lane1 == i * (n_pg // ppb) + bb, col, km)
    km_ref[...] = km

    @pl.when(i == n_steps - 1)
    def _():
        rows = q_ref.shape[1]
        lanef = lax.broadcasted_iota(jnp.int32, (rows, LANES), 1).astype(F32)
        for h in range(MOBA_HEADS):
            qh = q_ref[0, :, h * MOBA_HD:(h + 1) * MOBA_HD]
            gate = _dot3(qh, km[h * MOBA_HD:(h + 1) * MOBA_HD, :])
            gate = jnp.where(lanef < nblk, gate, -jnp.inf)
            out = jnp.zeros((rows, LANES), F32)
            for r in range(MOBA_TOPK):
                mx = jnp.max(gate, axis=1, keepdims=True)
                idx = jnp.min(jnp.where(gate == mx, lanef, 1e9), axis=1, keepdims=True)
                out = jnp.where(lanef == r, idx, out)
                gate = jnp.where(lanef == idx, -jnp.inf, gate)
            idx_ref[0, h] = out.astype(jnp.int32)


def _kmean_topk(page_table, cache_kt, mq_pad, *, n_pg=32):
    DB, n_pages = page_table.shape
    ppb = MOBA_BLOCK // PAGE_SIZE
    nblk = n_pages // ppb
    n_pg = min(n_pg, n_pages)
    assert n_pages % n_pg == 0 and n_pg % ppb == 0 and nblk <= LANES
    n_steps = n_pages // n_pg
    rows = mq_pad.shape[1]
    W = MOBA_HEADS * MOBA_HD

    def page_spec(r):
        return pl.BlockSpec((None, MOBA_HEADS, MOBA_HD, PAGE_SIZE),
                            lambda b, i, pt: (pt[b * n_pages + i * n_pg + r], 0, 0, 0))

    grid_spec = pltpu.PrefetchScalarGridSpec(
        num_scalar_prefetch=1,
        grid=(DB, n_steps),
        in_specs=[page_spec(r) for r in range(n_pg)] + [pl.BlockSpec((1, rows, W), lambda b, i, pt: (b, 0, 0))],
        out_specs=pl.BlockSpec((1, MOBA_HEADS, rows, LANES), lambda b, i, pt: (b, 0, 0, 0)),
        scratch_shapes=[pltpu.VMEM((W, LANES), F32)],
    )
    return pl.pallas_call(
        functools.partial(_kmean_topk_kernel, n_pg=n_pg, n_steps=n_steps, ppb=ppb, nblk=nblk),
        grid_spec=grid_spec,
        out_shape=jax.ShapeDtypeStruct((DB, MOBA_HEADS, rows, LANES), jnp.int32),
        compiler_params=pltpu.CompilerParams(dimension_semantics=("parallel", "arbitrary"),
                                             vmem_limit_bytes=VMEM_LIMIT),
        name="kmean_topk",
    )(page_table.reshape(-1), *([cache_kt] * n_pg), mq_pad)


def _decode_attn_kernel(phys_ref, idx_ref, qt_ref, kown_ref, vown_ref, ck_hbm, cv_hbm, o_ref,
                        kbuf, vbuf, sem, *, ds, past, n_slab):
    b = pl.program_id(0)
    nb = pl.num_programs(0)
    hd = MOBA_HD
    ppb = MOBA_BLOCK // PAGE_SIZE

    def copies(bi, slot, u):
        slab = phys_ref[bi * n_slab + u]
        return (pltpu.make_async_copy(ck_hbm.at[slab], kbuf.at[slot, u], sem.at[slot, 0]),
                pltpu.make_async_copy(cv_hbm.at[slab], vbuf.at[slot, u], sem.at[slot, 1]))

    def start_all(bi, slot):
        def body(u, carry):
            ck, cv = copies(bi, slot, u)
            ck.start(priority=0)
            cv.start(priority=1)
            return carry
        lax.fori_loop(0, n_slab, body, 0, unroll=8)

    def wait_all(bi, slot):
        def body(u, carry):
            ck, cv = copies(bi, slot, u)
            ck.wait()
            cv.wait()
            return carry
        lax.fori_loop(0, n_slab, body, 0, unroll=8)

    slot = b % 2

    @pl.when(b == 0)
    def _():
        start_all(b, slot)

    @pl.when(b + 1 < nb)
    def _():
        start_all(b + 1, 1 - slot)

    wait_all(b, slot)

    lane = lax.broadcasted_iota(jnp.int32, (1, LANES), 1)
    lanef = lane.astype(F32)

    def head_body(h, carry):
        r0 = pl.multiple_of(h * hd, hd)
        slope = _alibi_slope(jnp.full((1, 1), h, jnp.int32).astype(F32))
        kown = kown_ref[0, pl.ds(r0, hd), :]
        vown = vown_ref[0, pl.ds(r0, hd), :]
        qt = qt_ref[0, pl.ds(r0, hd), :]
        otile = jnp.zeros((hd, LANES), F32)
        for t in range(ds):
            qb = jnp.broadcast_to(qt[:, t:t + 1], (hd, LANES))
            s_own = jnp.sum(qb * kown, axis=0, keepdims=True)
            s_own = jnp.where(lane <= t, s_own - slope * (t - lanef), NEG)
            scores = [s_own]
            for s in range(MOBA_TOPK):
                blk_id = idx_ref[(b * MOBA_HEADS + h) * (ds * MOBA_TOPK) + t * MOBA_TOPK + s]
                for r in range(ppb):
                    u = ((h * ds + t) * MOBA_TOPK + s) * ppb + r
                    sc = jnp.sum(qb * kbuf[slot, u], axis=0, keepdims=True)
                    start = (blk_id * MOBA_BLOCK + r * PAGE_SIZE).astype(F32)
                    scores.append(sc - slope * ((past + t) - start - lanef))
            mx = scores[0]
            for sc in scores[1:]:
                mx = jnp.maximum(mx, sc)
            mx = jnp.max(mx, axis=1, keepdims=True)
            pr = jnp.exp(scores[0] - mx)
            den = pr
            acc = pr * vown
            i = 1
            for s in range(MOBA_TOPK):
                for r in range(ppb):
                    u = ((h * ds + t) * MOBA_TOPK + s) * ppb + r
                    pr = jnp.exp(scores[i] - mx)
                    den = den + pr
                    acc = acc + pr * vbuf[slot, u]
                    i += 1
            o_col = jnp.sum(acc, axis=1, keepdims=True) / jnp.sum(den, axis=1, keepdims=True)
            otile = jnp.where(lane == t, o_col, otile)
        o_ref[0, pl.ds(r0, hd), :] = otile
        return carry

    lax.fori_loop(0, MOBA_HEADS, head_body, 0)


def _decode_attn(phys, idx, qt, kown_t, vown_t, cache_kt, cache_vt, *, ds, past):
    DB, W, _ = qt.shape
    n_slab = MOBA_HEADS * ds * MOBA_TOPK * (MOBA_BLOCK // PAGE_SIZE)
    tile = pl.BlockSpec((1, W, LANES), lambda b, ph, ix: (b, 0, 0))
    grid_spec = pltpu.PrefetchScalarGridSpec(
        num_scalar_prefetch=2,
        grid=(DB,),
        in_specs=[tile, tile, tile, pl.BlockSpec(memory_space=pl.ANY), pl.BlockSpec(memory_space=pl.ANY)],
        out_specs=tile,
        scratch_shapes=[pltpu.VMEM((2, n_slab, MOBA_HD, PAGE_SIZE), F32),
                        pltpu.VMEM((2, n_slab, MOBA_HD, PAGE_SIZE), F32),
                        pltpu.SemaphoreType.DMA((2, 2))],
    )
    return pl.pallas_call(
        functools.partial(_decode_attn_kernel, ds=ds, past=past, n_slab=n_slab),
        grid_spec=grid_spec,
        out_shape=jax.ShapeDtypeStruct((DB, W, LANES), F32),
        compiler_params=pltpu.CompilerParams(dimension_semantics=("arbitrary",),
                                             vmem_limit_bytes=VMEM_LIMIT),
        name="decode_attn",
    )(phys, idx, qt, kown_t, vown_t, cache_kt, cache_vt)


def _tokens_on_lanes(a, db, ds):
    w = a.shape[0]
    a = a.reshape(w, db, ds).transpose(1, 0, 2)
    return jnp.pad(a, ((0, 0), (0, 0), (0, LANES - ds)))


def _layer_weights(w_in, conv_w, a_log, dt_bias, ffn_up, ffn_down, w_out):
    hq = GDN_HEADS * GDN_DK
    n_qkv = 3 * hq
    o_a = n_qkv
    o_z = o_a + 2 * GDN_HEADS
    o_m = o_z + hq
    n_m = MOBA_HEADS * MOBA_HD
    wa = jnp.concatenate([w_in[:, :n_qkv], w_in[:, o_z:o_m + n_m]], axis=1).astype(BF16)
    wkv_t = w_in[:, o_m + n_m:].T.astype(BF16)
    wab = jnp.pad(w_in[:, o_a:o_z], ((0, 0), (0, LANES - 2 * GDN_HEADS)))
    wab_hi = wab.astype(BF16)
    wab_lo = (wab - wab_hi.astype(F32)).astype(BF16)
    gp = jnp.zeros((8, LANES), F32).at[0, :GDN_HEADS].set(a_log).at[1, :GDN_HEADS].set(dt_bias)
    return dict(wa=wa, wkv_t=wkv_t, wab=jnp.concatenate([wab_hi, wab_lo], axis=1), gp=gp, conv_w=conv_w,
                up=[ffn_up[i].astype(BF16) for i in range(2)],
                down=[ffn_down[i].astype(BF16) for i in range(2)],
                wo=w_out.astype(BF16), n_qkv=n_qkv, n_z=hq, n_m=n_m)


def _trunk(x, mods, lw, norm_w, gdn_norm_w, conv0, s0, attend, *, tm, ct, nseq, t_valid, final_w):
    sh1, sc1, g1, sh2, sc2, g2, sh3, sc3, g3 = mods
    h = _ffn(x, sh1, sc1, g1, norm_w[0], lw["up"][0], lw["down"][0], tm=tm)
    qkv_pre, z, mq, kt, vt, ab = _win(h, sh2, sc2, norm_w[1], lw["wa"], lw["wkv_t"], lw["wab"],
                                      tm=tm, n_qkv=lw["n_qkv"], n_z=lw["n_z"], n_m=lw["n_m"])
    go, s_new, conv_new = _gdn(*attend["gdn_in"](qkv_pre, ab, z), conv0, s0, lw["conv_w"], lw["gp"],
                               gdn_norm_w.reshape(1, -1), ct=ct, t_valid=t_valid, nseq=nseq)
    go = attend["gdn_out"](go)
    mo = attend["moba"](mq, kt, vt)
    h = _ffn(h, sh3, sc3, g3, norm_w[2], lw["up"][1], lw["down"][1], tm=tm, mix=(go, mo, g2, lw["wo"]),
             final_w=final_w)
    return h, s_new, conv_new, kt, vt


def kernel(x_prompt, x_sample, c_prompt, c_sample, cache_k, cache_v, page_table, state_gdn, state_conv,
           w_ada, b_ada, norm_w, ffn_up, ffn_down, w_in, conv_w, a_log, dt_bias, gdn_norm_w, w_out,
           final_norm_w):
    B, T, D = x_prompt.shape
    DB, DS, _ = x_sample.shape
    depth = w_ada.shape[0]
    past = page_table.shape[1] * PAGE_SIZE
    ppb = MOBA_BLOCK // PAGE_SIZE
    assert past % MOBA_BLOCK == 0 and past // MOBA_BLOCK >= MOBA_TOPK and DS <= PAGE_SIZE
    assert T % MOBA_BLOCK == 0 and (T - 1) // MOBA_BLOCK >= MOBA_TOPK
    NS = DB * DS
    hq = GDN_HEADS * GDN_DK
    n_m = MOBA_HEADS * MOBA_HD
    ct_s = GDN_CHUNK

    hp = x_prompt
    hs = x_sample.reshape(1, NS, D)
    outs = [[] for _ in range(8)]
    for l in range(depth):
        lw = _layer_weights(w_in[l], conv_w[l], a_log[l], dt_bias[l], ffn_up[l], ffn_down[l], w_out[l])
        mod = _ada_mod(jnp.concatenate([c_prompt, c_sample], axis=0), w_ada[l], b_ada[l])
        mod_p = mod[:B].reshape(B, N_MOD, 1, D)
        mod_s = jnp.repeat(mod[B:].reshape(DB, N_MOD, D), DS, axis=0).reshape(1, NS, N_MOD, D)
        mods_p = [mod_p[:, i] for i in range(N_MOD)]
        mods_s = [mod_s[:, :, i] for i in range(N_MOD)]
        last = l == depth - 1
        fw = final_norm_w if last else None

        attend_p = dict(gdn_in=lambda qkv, ab, z: (qkv, ab, z), gdn_out=lambda go: go, moba=_moba_prompt)
        hp, s_p, c_p, kt_p, vt_p = _trunk(
            hp, mods_p, lw, norm_w[l], gdn_norm_w[l], jnp.zeros((B, GDN_CONV - 1, 3 * hq), F32),
            jnp.zeros((B, GDN_HEADS, GDN_DK, GDN_DK), F32), attend_p, tm=512, ct=128, nseq=4, t_valid=T,
            final_w=fw)

        cache_kt = jnp.transpose(cache_k[l], (0, 2, 3, 1))
        cache_vt = jnp.transpose(cache_v[l], (0, 2, 3, 1))
        page_tbl = page_table

        def pad_seq(a):
            return jnp.pad(a.reshape(DB, DS, a.shape[-1]), ((0, 0), (0, ct_s - DS), (0, 0)))

        def moba_s(mq, kt, vt, cache_kt=cache_kt, cache_vt=cache_vt, page_tbl=page_tbl):
            mq_pad = jnp.pad(mq.reshape(DB, DS, n_m), ((0, 0), (0, 16 - DS), (0, 0)))
            idx = _kmean_topk(page_tbl, cache_kt, mq_pad)[:, :, :DS, :MOBA_TOPK]
            lpage = idx[..., None] * ppb + jnp.arange(ppb, dtype=jnp.int32)
            phys = jnp.take_along_axis(page_tbl, lpage.reshape(DB, -1), axis=1).reshape(lpage.shape)
            slab = phys * MOBA_HEADS + jnp.arange(MOBA_HEADS, dtype=jnp.int32)[None, :, None, None, None]
            qt = _tokens_on_lanes(mq[0].T, DB, DS)
            slabs = lambda c: c.reshape((-1,) + c.shape[2:])
            ot = _decode_attn(slab.reshape(-1), idx.reshape(-1), qt, _tokens_on_lanes(kt[0], DB, DS),
                              _tokens_on_lanes(vt[0], DB, DS), slabs(cache_kt), slabs(cache_vt),
                              ds=DS, past=past)
            return ot[:, :, :DS].transpose(0, 2, 1).reshape(1, NS, n_m)

        attend_s = dict(gdn_in=lambda qkv, ab, z: (pad_seq(qkv), pad_seq(ab), pad_seq(z)),
                        gdn_out=lambda go: go[:, :DS].reshape(1, NS, hq), moba=moba_s)
        hs, s_s, c_s, kt_s, vt_s = _trunk(
            hs, mods_s, lw, norm_w[l], gdn_norm_w[l], state_conv[l], state_gdn[l], attend_s,
            tm=NS, ct=ct_s, nseq=8, t_valid=DS, final_w=fw)

        def heads_last(a_t, b, t):
            return a_t.reshape(b, MOBA_HEADS, MOBA_HD, t).transpose(0, 3, 1, 2)

        new = [heads_last(kt_p, B, T), heads_last(vt_p, B, T), s_p, c_p,
               heads_last(kt_s.reshape(n_m, DB, DS).transpose(1, 0, 2), DB, DS),
               heads_last(vt_s.reshape(n_m, DB, DS).transpose(1, 0, 2), DB, DS), s_s, c_s]
        for o, v in zip(outs, new):
            o.append(v)
    y_prompt = hp
    y_sample = hs.reshape(DB, DS, D)
    return (y_prompt, y_sample) + tuple(jnp.stack(o) for o in outs)
```
